```python
import math
import jax, jax.numpy as jnp
from jax import lax
import numpy as np

D_MODEL = 1024
BATCH = 2
SEQ = 8192
DEPTH = 1

MEM_LEN = 256
HG_HEADS = 8
HG_DK = 128
HG_DV = D_MODEL // HG_HEADS
HG_WIDTH = HG_HEADS * HG_DK
HG_VWIDTH = HG_HEADS * HG_DV
HG_CHUNK = 64
SWA_HEADS = 16
SWA_KV_HEADS = 2
SWA_HEAD_DIM = 64
SWA_GROUP = SWA_HEADS // SWA_KV_HEADS
SWA_WIDTH = SWA_HEADS * SWA_HEAD_DIM
SWA_KV_WIDTH = SWA_KV_HEADS * SWA_HEAD_DIM
SWA_WINDOW = 128
SWA_BLOCK = 128
MEM_HEADS = 4
MEM_HEAD_DIM = 256
MEM_WIDTH = MEM_HEADS * MEM_HEAD_DIM
N_BRANCHES = 3
NUM_BUCKETS = 32
MAX_DISTANCE = 128
D_FF = 4 * D_MODEL
LN_EPS = 1e-5
RMS_EPS = 1e-6
IN_SPLITS = (HG_WIDTH, HG_WIDTH, HG_VWIDTH, HG_VWIDTH, SWA_WIDTH, SWA_KV_WIDTH, SWA_KV_WIDTH, MEM_WIDTH, N_BRANCHES * D_MODEL)
IN_COLS = sum(IN_SPLITS)

kernel_name = "hybrid_hgrn2_swa_sink_memory_deepnorm"


def split_cols(z, sizes):
    out = []
    start = 0
    for s in sizes:
        out.append(z[..., start:start + s])
        start += s
    return out


def layer_norm(x, g, b):
    x = x.astype(jnp.float32)
    mu = jnp.mean(x, axis=-1, keepdims=True)
    xc = x - mu
    var = jnp.mean(xc * xc, axis=-1, keepdims=True)
    return xc * lax.rsqrt(var + LN_EPS) * g.astype(jnp.float32) + b.astype(jnp.float32)


def t5_bucket(n):
    max_exact = NUM_BUCKETS // 2
    nf = jnp.maximum(n, 1).astype(jnp.float32)
    large = max_exact + (jnp.log(nf / max_exact) / math.log(MAX_DISTANCE / max_exact)
                         * (NUM_BUCKETS - max_exact)).astype(jnp.int32)
    large = jnp.minimum(large, NUM_BUCKETS - 1)
    return jnp.where(n < max_exact, n, large)


def hgrn2(q, f_logit, v, lb):
    B, S = q.shape[0], q.shape[1]
    n_chunks = S // HG_CHUNK
    f = lb + (1.0 - lb) * jax.nn.sigmoid(f_logit.astype(jnp.float32))
    log_f = jnp.log(f)
    k = 1.0 - f

    def to_chunks(t):
        return t.astype(jnp.float32).reshape(B, n_chunks, HG_CHUNK, HG_HEADS, -1).transpose(1, 0, 3, 2, 4)

    qc, kc, vc, gc = to_chunks(q), to_chunks(k), to_chunks(v), to_chunks(log_f)
    b = jnp.cumsum(gc, axis=3)
    b_last = b[:, :, :, -1:, :]
    q_in = qc * jnp.exp(b)
    k_in = kc * jnp.exp(-b)
    k_out = kc * jnp.exp(b_last - b)
    causal = jnp.tril(jnp.ones((HG_CHUNK, HG_CHUNK), dtype=bool))
    attn = jnp.einsum('nbhcd,nbhsd->nbhcs', q_in, k_in)
    attn = jnp.where(causal, attn, 0.0)
    o_intra = jnp.einsum('nbhcs,nbhsv->nbhcv', attn, vc)

    def step(state, inp):
        k_o, v_n, decay = inp
        new = decay[..., :, None] * state + jnp.einsum('bhsd,bhsv->bhdv', k_o, v_n)
        return new, state

    init = jnp.zeros((B, HG_HEADS, HG_DK, HG_DV), jnp.float32)
    _, states = lax.scan(step, init, (k_out, vc, jnp.exp(b_last[:, :, :, 0, :])))
    o_inter = jnp.einsum('nbhcd,nbhdv->nbhcv', q_in, states)
    o = o_intra + o_inter
    return o.transpose(1, 0, 3, 2, 4).reshape(B, S, HG_HEADS, HG_DV)


def sliding_window_attention(q, k, v, rel_bias, sinks):
    B, S = q.shape[0], q.shape[1]
    nb = S // SWA_BLOCK
    scale = SWA_HEAD_DIM ** -0.5
    qb = q.astype(jnp.float32).reshape(B, nb, SWA_BLOCK, SWA_KV_HEADS, SWA_GROUP, SWA_HEAD_DIM) * scale
    kb = k.astype(jnp.float32).reshape(B, nb, SWA_BLOCK, SWA_KV_HEADS, SWA_HEAD_DIM)
    vb = v.astype(jnp.float32).reshape(B, nb, SWA_BLOCK, SWA_KV_HEADS, SWA_HEAD_DIM)
    pad = jnp.zeros_like(kb[:, :1])
    kk = jnp.concatenate([jnp.concatenate([pad, kb[:, :-1]], axis=1), kb], axis=2)
    vv = jnp.concatenate([jnp.concatenate([pad, vb[:, :-1]], axis=1), vb], axis=2)
    s = jnp.einsum('bnqkgd,bnskd->bkgnqs', qb, kk)
    qi = jnp.arange(SWA_BLOCK)[:, None] + SWA_BLOCK
    kj = jnp.arange(2 * SWA_BLOCK)[None, :]
    dist = qi - kj
    band = (dist >= 0) & (dist < SWA_WINDOW)
    valid = band[None] & ((jnp.arange(nb)[:, None, None] > 0) | (kj[None] >= SWA_BLOCK))
    bucket = t5_bucket(jnp.clip(dist, 0, SWA_WINDOW - 1))
    bias = rel_bias.astype(jnp.float32)[bucket].transpose(2, 0, 1)
    bias = bias.reshape(SWA_KV_HEADS, SWA_GROUP, 1, SWA_BLOCK, 2 * SWA_BLOCK)
    s = jnp.where(valid, s + bias, -jnp.inf)
    sink = sinks.astype(jnp.float32).reshape(SWA_KV_HEADS, SWA_GROUP, 1, 1, 1)
    m = jnp.maximum(jnp.max(s, axis=-1, keepdims=True), sink)
    p = jnp.exp(s - m)
    p = p / (jnp.sum(p, axis=-1, keepdims=True) + jnp.exp(sink - m))
    o = jnp.einsum('bkgnqs,bnskd->bnqkgd', p, vv)
    return o.reshape(B, S, SWA_WIDTH)


def memory_attention(q, mk, mv):
    B, S = q.shape[0], q.shape[1]
    M = mk.shape[1]
    qh = q.astype(jnp.float32).reshape(B, S, MEM_HEADS, MEM_HEAD_DIM) * (MEM_HEAD_DIM ** -0.5)
    kh = mk.astype(jnp.float32).reshape(B, M, MEM_HEADS, MEM_HEAD_DIM)
    vh = mv.astype(jnp.float32).reshape(B, M, MEM_HEADS, MEM_HEAD_DIM)
    p = jax.nn.softmax(jnp.einsum('bshd,bmhd->bhsm', qh, kh), axis=-1)
    return jnp.einsum('bhsm,bmhd->bshd', p, vh).reshape(B, S, MEM_WIDTH)


def setup_inputs(seed: int = 0) -> dict:
    key = jax.random.key(seed)
    ks = jax.random.split(key, 18)
    beta = (8.0 * DEPTH) ** -0.25

    def nrm(k, shape, scale):
        return jax.random.normal(k, shape, jnp.float32) * scale

    return {
        "x": nrm(ks[0], (BATCH, SEQ, D_MODEL), 1.0),
        "mem": nrm(ks[1], (BATCH, MEM_LEN, D_MODEL), 1.0),
        "w_in": nrm(ks[2], (DEPTH, D_MODEL, IN_COLS), D_MODEL ** -0.5),
        "lb_logits": nrm(ks[3], (DEPTH + 1, HG_WIDTH), 0.1),
        "hg_norm_gain": 1.0 + nrm(ks[4], (DEPTH, HG_VWIDTH), 0.02),
        "swa_sinks": nrm(ks[5], (DEPTH, SWA_HEADS), 0.5),
        "rel_bias": nrm(ks[6], (NUM_BUCKETS, SWA_HEADS), 0.5),
        "w_mem_kv": nrm(ks[7], (DEPTH, D_MODEL, 2 * MEM_WIDTH), D_MODEL ** -0.5),
        "w_branch_hg": nrm(ks[8], (DEPTH, HG_VWIDTH, D_MODEL), HG_VWIDTH ** -0.5),
        "w_branch_swa": nrm(ks[9], (DEPTH, SWA_WIDTH, D_MODEL), SWA_WIDTH ** -0.5),
        "w_branch_mem": nrm(ks[10], (DEPTH, MEM_WIDTH, D_MODEL), MEM_WIDTH ** -0.5),
        "w_out": nrm(ks[11], (DEPTH, D_MODEL, D_MODEL), (D_MODEL ** -0.5) * beta),
        "ln1_g": 1.0 + nrm(ks[12], (DEPTH, D_MODEL), 0.02),
        "ln1_b": nrm(ks[13], (DEPTH, D_MODEL), 0.02),
        "w_up": nrm(ks[14], (DEPTH, D_MODEL, D_FF), D_MODEL ** -0.5),
        "w_down": nrm(ks[15], (DEPTH, D_FF, D_MODEL), (D_FF ** -0.5) * beta),
        "ln2_g": 1.0 + nrm(ks[16], (DEPTH, D_MODEL), 0.02),
        "ln2_b": nrm(ks[17], (DEPTH, D_MODEL), 0.02),
    }


def reference(x, mem, w_in, lb_logits, hg_norm_gain, swa_sinks, rel_bias, w_mem_kv,
              w_branch_hg, w_branch_swa, w_branch_mem, w_out, ln1_g, ln1_b,
              w_up, w_down, ln2_g, ln2_b):
    B, S = x.shape[0], x.shape[1]
    out_dtype = x.dtype
    alpha = (2.0 * DEPTH) ** 0.25
    lb_all = jnp.cumsum(jax.nn.softmax(lb_logits.astype(jnp.float32), axis=0), axis=0)
    memf = mem.astype(jnp.float32)
    h = x.astype(jnp.float32)
    for layer in range(DEPTH):
        z = h @ w_in[layer]
        hq, hf, hi, hg, sq, sk, sv, mq, gl = split_cols(z, IN_SPLITS)
        o_a = hgrn2(hq.reshape(B, S, HG_HEADS, HG_DK), hf.reshape(B, S, HG_HEADS, HG_DK),
                    hi.reshape(B, S, HG_HEADS, HG_DV), lb_all[layer].reshape(HG_HEADS, HG_DK))
        o_a = o_a * lax.rsqrt(jnp.mean(o_a * o_a, axis=-1, keepdims=True) + RMS_EPS)
        o_a = o_a.reshape(B, S, HG_VWIDTH) * hg_norm_gain[layer].astype(jnp.float32) * jax.nn.silu(hg.astype(jnp.float32))
        o_b = sliding_window_attention(sq, sk, sv, rel_bias, swa_sinks[layer])
        mk, mv = split_cols(memf @ w_mem_kv[layer], (MEM_WIDTH, MEM_WIDTH))
        o_c = memory_attention(mq, mk, mv)
        gates = jax.nn.sigmoid(gl.astype(jnp.float32).reshape(B, S, N_BRANCHES, D_MODEL))
        merged = (gates[:, :, 0] * (o_a @ w_branch_hg[layer])
                  + gates[:, :, 1] * (o_b @ w_branch_swa[layer])
                  + gates[:, :, 2] * (o_c @ w_branch_mem[layer]))
        mix = merged @ w_out[layer]
        h = layer_norm(alpha * h + mix, ln1_g[layer], ln1_b[layer])
        ff = jnp.square(jax.nn.relu(h @ w_up[layer])) @ w_down[layer]
        h = layer_norm(alpha * h + ff, ln2_g[layer], ln2_b[layer])
    return h.astype(out_dtype)
```

```python
import functools
import math

import numpy as np
import jax
import jax.numpy as jnp
from jax import lax
from jax.experimental import pallas as pl
from jax.experimental.pallas import tpu as pltpu

F32 = jnp.float32
BF16 = jnp.bfloat16

D_MODEL = 1024
HG_HEADS = 8
HG_DK = 128
HG_DV = 128
HG_CHUNK = 64
SWA_HEADS = 16
SWA_KV_HEADS = 2
SWA_GROUP = SWA_HEADS // SWA_KV_HEADS
SWA_HEAD_DIM = 64
SWA_BLOCK = 128
MEM_HEADS = 4
MEM_HEAD_DIM = 256
NUM_BUCKETS = 32
MAX_DISTANCE = 128
D_FF = 4 * D_MODEL
LN_EPS = 1e-5
RMS_EPS = 1e-6

Z_HQ, Z_HF, Z_HI, Z_HG, Z_SQ, Z_MQ = (i * D_MODEL for i in range(6))
Z_GL = 6 * D_MODEL
Z_KV = 9 * D_MODEL
Z_COLS = Z_KV + 2 * SWA_KV_HEADS * SWA_HEAD_DIM

VMEM_LIMIT_BYTES = 56 * 1024 * 1024

_NT = (((1,), (1,)), ((), ()))
_TN = (((0,), (0,)), ((), ()))


def _dot(a, b):
    return jnp.dot(a, b, preferred_element_type=F32)


def _resident(shape):
    nd = len(shape)
    return pl.BlockSpec(shape, lambda *_: (0,) * nd, pipeline_mode=pl.Buffered(1))


PROJ_TM = 512
PROJ_TN = 1024


def _proj_in_kernel(x_ref, w_ref, z_ref):
    xb = x_ref[...].astype(BF16)
    for n0 in range(0, Z_COLS, PROJ_TN):
        n1 = min(n0 + PROJ_TN, Z_COLS)
        z_ref[:, n0:n1] = _dot(xb, w_ref[:, n0:n1]).astype(BF16)


def _proj_in(x2d, w_in_b):
    t = x2d.shape[0]
    return pl.pallas_call(
        _proj_in_kernel,
        out_shape=jax.ShapeDtypeStruct((t, Z_COLS), BF16),
        grid=(t // PROJ_TM,),
        in_specs=[pl.BlockSpec((PROJ_TM, D_MODEL), lambda i: (i, 0)),
                  _resident((D_MODEL, Z_COLS))],
        out_specs=pl.BlockSpec((PROJ_TM, Z_COLS), lambda i: (i, 0)),
        compiler_params=pltpu.CompilerParams(
            dimension_semantics=("arbitrary",), vmem_limit_bytes=VMEM_LIMIT_BYTES),
        name="proj_in",
    )(x2d, w_in_b)


HG_TB = 512


def _hgrn_kernel(lbl_ref, gain_ref, tril_ref, zq_ref, zf_ref, zi_ref, zg_ref, o_ref, st_ref):
    @pl.when(pl.program_id(1) == 0)
    def _():
        st_ref[...] = jnp.zeros_like(st_ref)

    l0 = lbl_ref[0:1, :]
    l1 = lbl_ref[1:2, :]
    lmax = jnp.maximum(l0, l1)
    e0 = jnp.exp(l0 - lmax)
    e1 = jnp.exp(l1 - lmax)
    lb_all = e0 / (e0 + e1)
    tril = tril_ref[...]
    causal = tril > 0

    def chunk(c, carry):
        r0 = pl.multiple_of(c * HG_CHUNK, HG_CHUNK)
        rows = pl.ds(r0, HG_CHUNK)
        for h in range(HG_HEADS):
            cols = slice(h * HG_DK, (h + 1) * HG_DK)
            lb = lb_all[:, cols]
            q = zq_ref[rows, cols].astype(F32)
            fl = zf_ref[rows, cols].astype(F32)
            v = zi_ref[rows, cols]
            g = zg_ref[rows, cols].astype(F32)
            f = lb + (1.0 - lb) * jax.nn.sigmoid(fl)
            logf = jnp.log(f)
            k = 1.0 - f
            hi = logf.astype(BF16)
            lo = (logf - hi.astype(F32)).astype(BF16)
            bb = _dot(tril, jnp.concatenate([hi, lo], axis=1))
            b = bb[:, :HG_DK] + bb[:, HG_DK:]
            b_last = b[HG_CHUNK - 1:HG_CHUNK, :]
            q_in = (q * jnp.exp(b)).astype(BF16)
            k_in = (k * jnp.exp(-b)).astype(BF16)
            k_out = (k * jnp.exp(b_last - b)).astype(BF16)
            attn = lax.dot_general(q_in, k_in, _NT, preferred_element_type=F32)
            attn = jnp.where(causal, attn, 0.0).astype(BF16)
            st = st_ref[h]
            o = _dot(attn, v) + lax.dot_general(q_in, st.astype(BF16), _NT, preferred_element_type=F32)
            st_ref[h] = st * jnp.exp(b_last) + lax.dot_general(v, k_out, _TN, preferred_element_type=F32)
            ms = jnp.mean(o * o, axis=-1, keepdims=True)
            o = o * lax.rsqrt(ms + RMS_EPS) * gain_ref[:, cols] * (g * jax.nn.sigmoid(g))
            o_ref[rows, cols] = o.astype(BF16)
        return carry

    lax.fori_loop(0, HG_TB // HG_CHUNK, chunk, 0)


def _hgrn(z, lb_logits, gain, batch, seq):
    nt = seq // HG_TB
    tril = jnp.asarray(np.tril(np.ones((HG_CHUNK, HG_CHUNK), np.float32)), BF16)

    def zspec(col):
        return pl.BlockSpec((HG_TB, D_MODEL), lambda b, t, col=col: (b * nt + t, col // D_MODEL))

    return pl.pallas_call(
        _hgrn_kernel,
        out_shape=jax.ShapeDtypeStruct((batch * seq, D_MODEL), BF16),
        grid=(batch, nt),
        in_specs=[_resident(lb_logits.shape), _resident(gain.shape), _resident(tril.shape),
                  zspec(Z_HQ), zspec(Z_HF), zspec(Z_HI), zspec(Z_HG)],
        out_specs=pl.BlockSpec((HG_TB, D_MODEL), lambda b, t: (b * nt + t, 0)),
        scratch_shapes=[pltpu.VMEM((HG_HEADS, HG_DV, HG_DK), F32)],
        compiler_params=pltpu.CompilerParams(
            dimension_semantics=("arbitrary", "arbitrary"), vmem_limit_bytes=VMEM_LIMIT_BYTES),
        name="hgrn",
    )(lb_logits, gain, tril, z, z, z, z)


def _swa_bucket_table():
    qi = np.arange(SWA_BLOCK)[:, None]
    kj = np.arange(SWA_BLOCK)[None, :]
    n = np.where(kj <= qi, qi - kj, qi + SWA_BLOCK - kj).astype(np.int32)
    max_exact = NUM_BUCKETS // 2
    nf = np.maximum(n, 1).astype(np.float32)
    large = max_exact + (np.log(nf / np.float32(max_exact)) / np.float32(math.log(MAX_DISTANCE / max_exact))
                         * np.float32(NUM_BUCKETS - max_exact)).astype(np.int32)
    large = np.minimum(large, NUM_BUCKETS - 1)
    return np.where(n < max_exact, n, large).astype(np.int32)


def _swa_kernel(relb_ref, sink_ref, bucket_ref, q_ref, kvc_ref, kvp_ref, o_ref, bias_ref, kz_ref, vz_ref):
    n = pl.program_id(1)

    @pl.when((pl.program_id(0) == 0) & (n == 0))
    def _():
        bucket = bucket_ref[...]
        for h in range(SWA_HEADS):
            acc = jnp.zeros((SWA_BLOCK, SWA_BLOCK), F32)
            for bk in range(NUM_BUCKETS):
                acc = jnp.where(bucket == bk, relb_ref[bk, h], acc)
            bias_ref[h] = acc

    lane = lax.broadcasted_iota(jnp.int32, (SWA_BLOCK, 2 * SWA_HEAD_DIM), 1)
    low = lane < SWA_HEAD_DIM
    for blk, ref in ((0, kvp_ref), (1, kvc_ref)):
        rows = slice(blk * SWA_BLOCK, (blk + 1) * SWA_BLOCK)
        kk = ref[:, 0:128].astype(F32)
        vv = ref[:, 128:256].astype(F32)
        kk_sw = pltpu.roll(kk, SWA_HEAD_DIM, 1)
        vv_sw = pltpu.roll(vv, SWA_HEAD_DIM, 1)
        for g in range(SWA_KV_HEADS):
            k_lo, k_hi = (kk, kk_sw) if g == 0 else (kk_sw, kk)
            v_lo, v_hi = (vv, vv_sw) if g == 0 else (vv_sw, vv)
            kz_ref[2 * g + 0, rows, :] = jnp.where(low, k_lo, 0.0).astype(BF16)
            kz_ref[2 * g + 1, rows, :] = jnp.where(low, 0.0, k_hi).astype(BF16)
            vz_ref[2 * g + 0, rows, :] = jnp.where(low, v_lo, 0.0).astype(BF16)
            vz_ref[2 * g + 1, rows, :] = jnp.where(low, 0.0, v_hi).astype(BF16)

    qi = lax.broadcasted_iota(jnp.int32, (SWA_BLOCK, SWA_BLOCK), 0)
    kj = lax.broadcasted_iota(jnp.int32, (SWA_BLOCK, SWA_BLOCK), 1)
    tri = kj <= qi
    valid = tri | (n > 0)
    scale = SWA_HEAD_DIM ** -0.5

    for pair in range(SWA_HEADS // 2):
        g = (2 * pair) // SWA_GROUP
        qp = q_ref[:, pair * 128:(pair + 1) * 128]
        o_pair = jnp.zeros((SWA_BLOCK, 128), F32)
        for half in range(2):
            h = 2 * pair + half
            slot = 2 * g + half
            s2 = lax.dot_general(qp, kz_ref[slot], _NT, preferred_element_type=F32)
            s = jnp.where(tri, s2[:, SWA_BLOCK:], s2[:, :SWA_BLOCK]) * scale + bias_ref[h]
            s = jnp.where(valid, s, -jnp.inf)
            sink = sink_ref[h]
            m = jnp.maximum(jnp.max(s, axis=-1, keepdims=True), sink)
            p = jnp.exp(s - m)
            denom = jnp.sum(p, axis=-1, keepdims=True) + jnp.exp(sink - m)
            p = p * (1.0 / denom)
            p2 = jnp.concatenate([jnp.where(tri, 0.0, p), jnp.where(tri, p, 0.0)], axis=1).astype(BF16)
            o_pair = o_pair + _dot(p2, vz_ref[slot])
        o_ref[:, pair * 128:(pair + 1) * 128] = o_pair.astype(BF16)


def _swa(z, rel_bias, sinks, batch, seq):
    nb = seq // SWA_BLOCK
    bucket = jnp.asarray(_swa_bucket_table())
    kv_col = Z_KV // 256
    smem = pl.BlockSpec(memory_space=pltpu.SMEM)
    return pl.pallas_call(
        _swa_kernel,
        out_shape=jax.ShapeDtypeStruct((batch * seq, D_MODEL), BF16),
        grid=(batch, nb),
        in_specs=[smem, smem, _resident(bucket.shape),
                  pl.BlockSpec((SWA_BLOCK, D_MODEL), lambda b, n: (b * nb + n, Z_SQ // D_MODEL)),
                  pl.BlockSpec((SWA_BLOCK, 256), lambda b, n: (b * nb + n, kv_col)),
                  pl.BlockSpec((SWA_BLOCK, 256), lambda b, n: (b * nb + jnp.maximum(n - 1, 0), kv_col))],
        out_specs=pl.BlockSpec((SWA_BLOCK, D_MODEL), lambda b, n: (b * nb + n, 0)),
        scratch_shapes=[pltpu.VMEM((SWA_HEADS, SWA_BLOCK, SWA_BLOCK), F32),
                        pltpu.VMEM((2 * SWA_KV_HEADS, 2 * SWA_BLOCK, 128), BF16),
                        pltpu.VMEM((2 * SWA_KV_HEADS, 2 * SWA_BLOCK, 128), BF16)],
        compiler_params=pltpu.CompilerParams(
            dimension_semantics=("arbitrary", "arbitrary"), vmem_limit_bytes=VMEM_LIMIT_BYTES),
        name="swa",
    )(rel_bias, sinks, bucket, z, z, z)


def _mem_kv_kernel(mem_ref, w_ref, kv_ref):
    kv_ref[...] = _dot(mem_ref[...].astype(BF16), w_ref[...]).astype(BF16)


def _mem_kv(mem2d, w_b):
    return pl.pallas_call(
        _mem_kv_kernel,
        out_shape=jax.ShapeDtypeStruct((mem2d.shape[0], w_b.shape[1]), BF16),
        compiler_params=pltpu.CompilerParams(vmem_limit_bytes=VMEM_LIMIT_BYTES),
        name="mem_kv",
    )(mem2d, w_b)


MERGE_TM = 256


def _layer_norm(y, g, b):
    mu = jnp.mean(y, axis=-1, keepdims=True)
    yc = y - mu
    var = jnp.mean(yc * yc, axis=-1, keepdims=True)
    return yc * lax.rsqrt(var + LN_EPS) * g + b


def _merge_kernel(alpha, x_ref, oa_ref, ob_ref, mq_ref, gl_ref, kv_ref, wa_ref, wb_ref, wc_ref, wo_ref,
                  g_ref, b_ref, h_ref):
    scale = MEM_HEAD_DIM ** -0.5
    merged = jnp.zeros((MERGE_TM, D_MODEL), F32)
    oc = []
    for h in range(MEM_HEADS):
        cols = slice(h * MEM_HEAD_DIM, (h + 1) * MEM_HEAD_DIM)
        vcols = slice(MEM_HEADS * MEM_HEAD_DIM + h * MEM_HEAD_DIM, MEM_HEADS * MEM_HEAD_DIM + (h + 1) * MEM_HEAD_DIM)
        s = lax.dot_general(mq_ref[:, cols], kv_ref[:, cols], _NT, preferred_element_type=F32) * scale
        m = jnp.max(s, axis=-1, keepdims=True)
        p = jnp.exp(s - m)
        p = p * (1.0 / jnp.sum(p, axis=-1, keepdims=True))
        oc.append(_dot(p.astype(BF16), kv_ref[:, vcols]).astype(BF16))
    o_c = jnp.concatenate(oc, axis=1)
    branches = ((oa_ref[...], wa_ref), (ob_ref[...], wb_ref), (o_c, wc_ref))
    for i, (o, w_ref) in enumerate(branches):
        gate = jax.nn.sigmoid(gl_ref[:, i * D_MODEL:(i + 1) * D_MODEL].astype(F32))
        merged = merged + gate * _dot(o, w_ref[...])
    mix = _dot(merged.astype(BF16), wo_ref[...])
    h_ref[...] = _layer_norm(alpha * x_ref[...] + mix, g_ref[...], b_ref[...])


def _merge(alpha, x2d, o_a, o_b, z, kv, wa, wb, wc, wo, ln_g, ln_b, batch, seq):
    t = x2d.shape[0]
    per_batch = seq // MERGE_TM
    row = lambda i: (i, 0)
    return pl.pallas_call(
        functools.partial(_merge_kernel, alpha),
        out_shape=jax.ShapeDtypeStruct((t, D_MODEL), F32),
        grid=(t // MERGE_TM,),
        in_specs=[pl.BlockSpec((MERGE_TM, D_MODEL), row),
                  pl.BlockSpec((MERGE_TM, D_MODEL), row),
                  pl.BlockSpec((MERGE_TM, D_MODEL), row),
                  pl.BlockSpec((MERGE_TM, D_MODEL), lambda i: (i, Z_MQ // D_MODEL)),
                  pl.BlockSpec((MERGE_TM, 3 * D_MODEL), lambda i: (i, Z_GL // (3 * D_MODEL))),
                  pl.BlockSpec((kv.shape[0] // batch, kv.shape[1]), lambda i: (i // per_batch, 0)),
                  _resident(wa.shape), _resident(wb.shape), _resident(wc.shape), _resident(wo.shape),
                  _resident(ln_g.shape), _resident(ln_b.shape)],
        out_specs=pl.BlockSpec((MERGE_TM, D_MODEL), row),
        compiler_params=pltpu.CompilerParams(
            dimension_semantics=("arbitrary",), vmem_limit_bytes=VMEM_LIMIT_BYTES),
        name="merge",
    )(x2d, o_a, o_b, z, z, kv, wa, wb, wc, wo, ln_g, ln_b)


FFN_TM = 512
FFN_TF = 1024


def _ffn_kernel(alpha, h_ref, wu_ref, wd_ref, g_ref, b_ref, o_ref):
    h = h_ref[...]
    hb = h.astype(BF16)
    ff = jnp.zeros((FFN_TM, D_MODEL), F32)
    for f0 in range(0, D_FF, FFN_TF):
        u = jnp.maximum(_dot(hb, wu_ref[:, f0:f0 + FFN_TF]), 0.0)
        ff = ff + _dot((u * u).astype(BF16), wd_ref[f0:f0 + FFN_TF, :])
    o_ref[...] = _layer_norm(alpha * h + ff, g_ref[...], b_ref[...])


def _ffn(alpha, h1, wu, wd, ln_g, ln_b):
    t = h1.shape[0]
    row = lambda i: (i, 0)
    return pl.pallas_call(
        functools.partial(_ffn_kernel, alpha),
        out_shape=jax.ShapeDtypeStruct((t, D_MODEL), F32),
        grid=(t // FFN_TM,),
        in_specs=[pl.BlockSpec((FFN_TM, D_MODEL), row),
                  _resident(wu.shape), _resident(wd.shape), _resident(ln_g.shape), _resident(ln_b.shape)],
        out_specs=pl.BlockSpec((FFN_TM, D_MODEL), row),
        compiler_params=pltpu.CompilerParams(
            dimension_semantics=("arbitrary",), vmem_limit_bytes=VMEM_LIMIT_BYTES),
        name="ffn",
    )(h1, wu, wd, ln_g, ln_b)


def kernel(x, mem, w_in, lb_logits, hg_norm_gain, swa_sinks, rel_bias, w_mem_kv, w_branch_hg, w_branch_swa,
           w_branch_mem, w_out, ln1_g, ln1_b, w_up, w_down, ln2_g, ln2_b):
    batch, seq, _ = x.shape
    depth = w_in.shape[0]
    assert depth == 1 and lb_logits.shape[0] == depth + 1
    alpha = (2.0 * depth) ** 0.25
    out_dtype = x.dtype

    w = w_in[0]
    o_sk = 5 * D_MODEL
    o_mq = o_sk + 2 * SWA_KV_HEADS * SWA_HEAD_DIM
    o_gl = o_mq + D_MODEL
    w_in_b = jnp.concatenate(
        [w[:, :o_sk], w[:, o_mq:o_gl], w[:, o_gl:], w[:, o_sk:o_mq]], axis=1).astype(BF16)

    x2d = x.reshape(batch * seq, D_MODEL).astype(F32)
    z = _proj_in(x2d, w_in_b)
    o_a = _hgrn(z, lb_logits.astype(F32), hg_norm_gain.astype(F32), batch, seq)
    o_b = _swa(z, rel_bias.astype(F32), swa_sinks[0].astype(F32), batch, seq)
    kv = _mem_kv(mem.reshape(batch * mem.shape[1], D_MODEL).astype(F32), w_mem_kv[0].astype(BF16))
    h1 = _merge(alpha, x2d, o_a, o_b, z, kv,
                w_branch_hg[0].astype(BF16), w_branch_swa[0].astype(BF16), w_branch_mem[0].astype(BF16),
                w_out[0].astype(BF16), ln1_g.astype(F32), ln1_b.astype(F32), batch, seq)
    h2 = _ffn(alpha, h1, w_up[0].astype(BF16), w_down[0].astype(BF16), ln2_g.astype(F32), ln2_b.astype(F32))
    return h2.reshape(batch, seq, D_MODEL).astype(out_dtype)
```

```python
import functools
import math

import numpy as np
import jax
import jax.numpy as jnp
from jax import lax
from jax.experimental import pallas as pl
from jax.experimental.pallas import tpu as pltpu

F32 = jnp.float32
BF16 = jnp.bfloat16

D_MODEL = 1024
HG_HEADS = 8
HG_DK = 128
HG_DV = 128
HG_CHUNK = 64
SWA_HEADS = 16
SWA_KV_HEADS = 2
SWA_GROUP = SWA_HEADS // SWA_KV_HEADS
SWA_HEAD_DIM = 64
SWA_BLOCK = 128
MEM_HEADS = 4
MEM_HEAD_DIM = 256
NUM_BUCKETS = 32
MAX_DISTANCE = 128
D_FF = 4 * D_MODEL
LN_EPS = 1e-5
RMS_EPS = 1e-6

Z_HQ, Z_HF, Z_HI, Z_HG, Z_SQ, Z_MQ = (i * D_MODEL for i in range(6))
Z_GL = 6 * D_MODEL
Z_KV = 9 * D_MODEL
Z_COLS = Z_KV + 2 * SWA_KV_HEADS * SWA_HEAD_DIM

VMEM_LIMIT_BYTES = 56 * 1024 * 1024

_NT = (((1,), (1,)), ((), ()))
_TN = (((0,), (0,)), ((), ()))


def _dot(a, b):
    return jnp.dot(a, b, preferred_element_type=F32)


def _resident(shape):
    nd = len(shape)
    return pl.BlockSpec(shape, lambda *_: (0,) * nd, pipeline_mode=pl.Buffered(1))


PROJ_TM = 512
PROJ_TN = 1024


def _proj_in_kernel(x_ref, w_ref, z_ref):
    xb = x_ref[...].astype(BF16)
    for n0 in range(0, Z_COLS, PROJ_TN):
        n1 = min(n0 + PROJ_TN, Z_COLS)
        z_ref[:, n0:n1] = _dot(xb, w_ref[:, n0:n1]).astype(BF16)


def _proj_in(x2d, w_in_b):
    t = x2d.shape[0]
    return pl.pallas_call(
        _proj_in_kernel,
        out_shape=jax.ShapeDtypeStruct((t, Z_COLS), BF16),
        grid=(t // PROJ_TM,),
        in_specs=[pl.BlockSpec((PROJ_TM, D_MODEL), lambda i: (i, 0)),
                  _resident((D_MODEL, Z_COLS))],
        out_specs=pl.BlockSpec((PROJ_TM, Z_COLS), lambda i: (i, 0)),
        compiler_params=pltpu.CompilerParams(
            dimension_semantics=("arbitrary",), vmem_limit_bytes=VMEM_LIMIT_BYTES),
        name="proj_in",
    )(x2d, w_in_b)


HG_TB = 512


def _hgrn_kernel(lbl_ref, gain_ref, tril_ref, zq_ref, zf_ref, zi_ref, zg_ref, o_ref, st_ref):
    @pl.when(pl.program_id(1) == 0)
    def _():
        st_ref[...] = jnp.zeros_like(st_ref)

    l0 = lbl_ref[0:1, :]
    l1 = lbl_ref[1:2, :]
    lmax = jnp.maximum(l0, l1)
    e0 = jnp.exp(l0 - lmax)
    e1 = jnp.exp(l1 - lmax)
    lb_all = e0 / (e0 + e1)
    tril = tril_ref[...]
    causal = tril > 0

    def chunk(c, carry):
        r0 = pl.multiple_of(c * HG_CHUNK, HG_CHUNK)
        rows = pl.ds(r0, HG_CHUNK)
        for h in range(HG_HEADS):
            cols = slice(h * HG_DK, (h + 1) * HG_DK)
            lb = lb_all[:, cols]
            q = zq_ref[rows, cols].astype(F32)
            fl = zf_ref[rows, cols].astype(F32)
            v = zi_ref[rows, cols]
            g = zg_ref[rows, cols].astype(F32)
            f = lb + (1.0 - lb) * jax.nn.sigmoid(fl)
            logf = jnp.log(f)
            k = 1.0 - f
            hi = logf.astype(BF16)
            lo = (logf - hi.astype(F32)).astype(BF16)
            bb = _dot(tril, jnp.concatenate([hi, lo], axis=1))
            b = bb[:, :HG_DK] + bb[:, HG_DK:]
            b_last = b[HG_CHUNK - 1:HG_CHUNK, :]
            q_in = (q * jnp.exp(b)).astype(BF16)
            k_in = (k * jnp.exp(-b)).astype(BF16)
            k_out = (k * jnp.exp(b_last - b)).astype(BF16)
            attn = lax.dot_general(q_in, k_in, _NT, preferred_element_type=F32)
            attn = jnp.where(causal, attn, 0.0).astype(BF16)
            st = st_ref[h]
            o = _dot(attn, v) + lax.dot_general(q_in, st.astype(BF16), _NT, preferred_element_type=F32)
            st_ref[h] = st * jnp.exp(b_last) + lax.dot_general(v, k_out, _TN, preferred_element_type=F32)
            ms = jnp.mean(o * o, axis=-1, keepdims=True)
            o = o * lax.rsqrt(ms + RMS_EPS) * gain_ref[:, cols] * (g * jax.nn.sigmoid(g))
            o_ref[rows, cols] = o.astype(BF16)
        return carry

    lax.fori_loop(0, HG_TB // HG_CHUNK, chunk, 0)


def _hgrn(z, lb_logits, gain, batch, seq):
    nt = seq // HG_TB
    tril = jnp.asarray(np.tril(np.ones((HG_CHUNK, HG_CHUNK), np.float32)), BF16)

    def zspec(col):
        return pl.BlockSpec((HG_TB, D_MODEL), lambda b, t, col=col: (b * nt + t, col // D_MODEL))

    return pl.pallas_call(
        _hgrn_kernel,
        out_shape=jax.ShapeDtypeStruct((batch * seq, D_MODEL), BF16),
        grid=(batch, nt),
        in_specs=[_resident(lb_logits.shape), _resident(gain.shape), _resident(tril.shape),
                  zspec(Z_HQ), zspec(Z_HF), zspec(Z_HI), zspec(Z_HG)],
        out_specs=pl.BlockSpec((HG_TB, D_MODEL), lambda b, t: (b * nt + t, 0)),
        scratch_shapes=[pltpu.VMEM((HG_HEADS, HG_DV, HG_DK), F32)],
        compiler_params=pltpu.CompilerParams(
            dimension_semantics=("arbitrary", "arbitrary"), vmem_limit_bytes=VMEM_LIMIT_BYTES),
        name="hgrn",
    )(lb_logits, gain, tril, z, z, z, z)


def _swa_bucket_table():
    qi = np.arange(SWA_BLOCK)[:, None]
    kj = np.arange(SWA_BLOCK)[None, :]
    n = np.where(kj <= qi, qi - kj, qi + SWA_BLOCK - kj).astype(np.int32)
    max_exact = NUM_BUCKETS // 2
    nf = np.maximum(n, 1).astype(np.float32)
    large = max_exact + (np.log(nf / np.float32(max_exact)) / np.float32(math.log(MAX_DISTANCE / max_exact))
                         * np.float32(NUM_BUCKETS - max_exact)).astype(np.int32)
    large = np.minimum(large, NUM_BUCKETS - 1)
    return np.where(n < max_exact, n, large).astype(np.int32)


SWA_TQ = 512
SWA_NQ = SWA_TQ // SWA_BLOCK
SWA_MASKED = -1e30


def _swa_kernel(relb_ref, sink_ref, bucket_ref, q_ref, kvc_ref, kvp_ref, o_ref, bias_ref, kz_ref, vz_ref):
    t = pl.program_id(1)
    qi = lax.broadcasted_iota(jnp.int32, (SWA_BLOCK, SWA_BLOCK), 0)
    kj = lax.broadcasted_iota(jnp.int32, (SWA_BLOCK, SWA_BLOCK), 1)
    tri = kj <= qi

    @pl.when((pl.program_id(0) == 0) & (t == 0))
    def _():
        bucket = bucket_ref[...]
        for h in range(SWA_HEADS):
            acc = jnp.zeros((SWA_BLOCK, SWA_BLOCK), F32)
            for bk in range(NUM_BUCKETS):
                acc = jnp.where(bucket == bk, relb_ref[bk, h], acc)
            bias_ref[h] = acc
            bias_ref[SWA_HEADS + h] = jnp.where(tri, acc, SWA_MASKED)
        vz_ref[:, :, 128:256] = jnp.ones((2 * SWA_KV_HEADS, SWA_TQ + SWA_BLOCK, 128), BF16)

    lane = lax.broadcasted_iota(jnp.int32, (SWA_BLOCK, 2 * SWA_HEAD_DIM), 1)
    low = lane < SWA_HEAD_DIM
    scale = SWA_HEAD_DIM ** -0.5
    for blk in range(SWA_NQ + 1):
        rows = slice(blk * SWA_BLOCK, (blk + 1) * SWA_BLOCK)
        if blk == 0:
            kk = kvp_ref[:, 0:128].astype(F32) * scale
            vv = kvp_ref[:, 128:256].astype(F32)
        else:
            src = slice((blk - 1) * SWA_BLOCK, blk * SWA_BLOCK)
            kk = kvc_ref[src, 0:128].astype(F32) * scale
            vv = kvc_ref[src, 128:256].astype(F32)
        kk_sw = pltpu.roll(kk, SWA_HEAD_DIM, 1)
        vv_sw = pltpu.roll(vv, SWA_HEAD_DIM, 1)
        for g in range(SWA_KV_HEADS):
            k_lo, k_hi = (kk, kk_sw) if g == 0 else (kk_sw, kk)
            v_lo, v_hi = (vv, vv_sw) if g == 0 else (vv_sw, vv)
            kz_ref[2 * g + 0, rows, :] = jnp.where(low, k_lo, 0.0).astype(BF16)
            kz_ref[2 * g + 1, rows, :] = jnp.where(low, 0.0, k_hi).astype(BF16)
            vz_ref[2 * g + 0, rows, 0:128] = jnp.where(low, v_lo, 0.0).astype(BF16)
            vz_ref[2 * g + 1, rows, 0:128] = jnp.where(low, 0.0, v_hi).astype(BF16)

    def sub_block(i, carry):
        r0 = pl.multiple_of(i * SWA_BLOCK, SWA_BLOCK)
        qrows = pl.ds(r0, SWA_BLOCK)
        krows = pl.ds(r0, 2 * SWA_BLOCK)
        bias_off = jnp.where((t == 0) & (i == 0), SWA_HEADS, 0)
        for pair in range(SWA_HEADS // 2):
            g = (2 * pair) // SWA_GROUP
            qp = q_ref[qrows, pair * 128:(pair + 1) * 128]
            o_pair = None
            for half in range(2):
                h = 2 * pair + half
                slot = 2 * g + half
                s2 = lax.dot_general(qp, kz_ref[slot, krows, :], _NT, preferred_element_type=F32)
                s = jnp.where(tri, s2[:, SWA_BLOCK:], s2[:, :SWA_BLOCK]) + bias_ref[bias_off + h]
                sink = sink_ref[h]
                m = jnp.maximum(jnp.max(s, axis=-1, keepdims=True), sink)
                p = jnp.exp(s - m)
                p2 = jnp.concatenate([jnp.where(tri, 0.0, p), jnp.where(tri, p, 0.0)], axis=1).astype(BF16)
                od = _dot(p2, vz_ref[slot, krows, :])
                den = od[:, 128:] + jnp.exp(sink - m)
                o_h = od[:, :128] * (1.0 / den)
                o_pair = o_h if o_pair is None else o_pair + o_h
            o_ref[qrows, pair * 128:(pair + 1) * 128] = o_pair.astype(BF16)
        return carry

    lax.fori_loop(0, SWA_NQ, sub_block, 0)


def _swa(z, rel_bias, sinks, batch, seq):
    nt = seq // SWA_TQ
    bucket = jnp.asarray(_swa_bucket_table())
    kv_col = Z_KV // 256
    smem = pl.BlockSpec(memory_space=pltpu.SMEM)
    return pl.pallas_call(
        _swa_kernel,
        out_shape=jax.ShapeDtypeStruct((batch * seq, D_MODEL), BF16),
        grid=(batch, nt),
        in_specs=[smem, smem, _resident(bucket.shape),
                  pl.BlockSpec((SWA_TQ, D_MODEL), lambda b, t: (b * nt + t, Z_SQ // D_MODEL)),
                  pl.BlockSpec((SWA_TQ, 256), lambda b, t: (b * nt + t, kv_col)),
                  pl.BlockSpec((SWA_BLOCK, 256),
                               lambda b, t: ((b * nt + t) * SWA_NQ - jnp.minimum(t, 1), kv_col))],
        out_specs=pl.BlockSpec((SWA_TQ, D_MODEL), lambda b, t: (b * nt + t, 0)),
        scratch_shapes=[pltpu.VMEM((2 * SWA_HEADS, SWA_BLOCK, SWA_BLOCK), F32),
                        pltpu.VMEM((2 * SWA_KV_HEADS, SWA_TQ + SWA_BLOCK, 128), BF16),
                        pltpu.VMEM((2 * SWA_KV_HEADS, SWA_TQ + SWA_BLOCK, 256), BF16)],
        compiler_params=pltpu.CompilerParams(
            dimension_semantics=("arbitrary", "arbitrary"), vmem_limit_bytes=VMEM_LIMIT_BYTES),
        name="swa",
    )(rel_bias, sinks, bucket, z, z, z)


def _mem_kv_kernel(mem_ref, w_ref, kv_ref):
    kv_ref[...] = _dot(mem_ref[...].astype(BF16), w_ref[...]).astype(BF16)


def _mem_kv(mem2d, w_b):
    return pl.pallas_call(
        _mem_kv_kernel,
        out_shape=jax.ShapeDtypeStruct((mem2d.shape[0], w_b.shape[1]), BF16),
        compiler_params=pltpu.CompilerParams(vmem_limit_bytes=VMEM_LIMIT_BYTES),
        name="mem_kv",
    )(mem2d, w_b)


MERGE_TM = 256


def _layer_norm(y, g, b):
    mu = jnp.mean(y, axis=-1, keepdims=True)
    yc = y - mu
    var = jnp.mean(yc * yc, axis=-1, keepdims=True)
    return yc * lax.rsqrt(var + LN_EPS) * g + b


def _merge_kernel(alpha, x_ref, oa_ref, ob_ref, mq_ref, gl_ref, kv_ref, wa_ref, wb_ref, wc_ref, wo_ref,
                  g_ref, b_ref, h_ref):
    scale = MEM_HEAD_DIM ** -0.5
    merged = jnp.zeros((MERGE_TM, D_MODEL), F32)
    oc = []
    for h in range(MEM_HEADS):
        cols = slice(h * MEM_HEAD_DIM, (h + 1) * MEM_HEAD_DIM)
        vcols = slice(MEM_HEADS * MEM_HEAD_DIM + h * MEM_HEAD_DIM, MEM_HEADS * MEM_HEAD_DIM + (h + 1) * MEM_HEAD_DIM)
        s = lax.dot_general(mq_ref[:, cols], kv_ref[:, cols], _NT, preferred_element_type=F32) * scale
        m = jnp.max(s, axis=-1, keepdims=True)
        p = jnp.exp(s - m)
        p = p * (1.0 / jnp.sum(p, axis=-1, keepdims=True))
        oc.append(_dot(p.astype(BF16), kv_ref[:, vcols]).astype(BF16))
    o_c = jnp.concatenate(oc, axis=1)
    branches = ((oa_ref[...], wa_ref), (ob_ref[...], wb_ref), (o_c, wc_ref))
    for i, (o, w_ref) in enumerate(branches):
        gate = jax.nn.sigmoid(gl_ref[:, i * D_MODEL:(i + 1) * D_MODEL].astype(F32))
        merged = merged + gate * _dot(o, w_ref[...])
    mix = _dot(merged.astype(BF16), wo_ref[...])
    h_ref[...] = _layer_norm(alpha * x_ref[...] + mix, g_ref[...], b_ref[...])


def _merge(alpha, x2d, o_a, o_b, z, kv, wa, wb, wc, wo, ln_g, ln_b, batch, seq):
    t = x2d.shape[0]
    per_batch = seq // MERGE_TM
    row = lambda i: (i, 0)
    return pl.pallas_call(
        functools.partial(_merge_kernel, alpha),
        out_shape=jax.ShapeDtypeStruct((t, D_MODEL), F32),
        grid=(t // MERGE_TM,),
        in_specs=[pl.BlockSpec((MERGE_TM, D_MODEL), row),
                  pl.BlockSpec((MERGE_TM, D_MODEL), row),
                  pl.BlockSpec((MERGE_TM, D_MODEL), row),
                  pl.BlockSpec((MERGE_TM, D_MODEL), lambda i: (i, Z_MQ // D_MODEL)),
                  pl.BlockSpec((MERGE_TM, 3 * D_MODEL), lambda i: (i, Z_GL // (3 * D_MODEL))),
                  pl.BlockSpec((kv.shape[0] // batch, kv.shape[1]), lambda i: (i // per_batch, 0)),
                  _resident(wa.shape), _resident(wb.shape), _resident(wc.shape), _resident(wo.shape),
                  _resident(ln_g.shape), _resident(ln_b.shape)],
        out_specs=pl.BlockSpec((MERGE_TM, D_MODEL), row),
        compiler_params=pltpu.CompilerParams(
            dimension_semantics=("arbitrary",), vmem_limit_bytes=VMEM_LIMIT_BYTES),
        name="merge",
    )(x2d, o_a, o_b, z, z, kv, wa, wb, wc, wo, ln_g, ln_b)


FFN_TM = 512
FFN_TF = 1024


def _ffn_kernel(alpha, h_ref, wu_ref, wd_ref, g_ref, b_ref, o_ref):
    h = h_ref[...]
    hb = h.astype(BF16)
    ff = jnp.zeros((FFN_TM, D_MODEL), F32)
    for f0 in range(0, D_FF, FFN_TF):
        u = jnp.maximum(_dot(hb, wu_ref[:, f0:f0 + FFN_TF]), 0.0)
        ff = ff + _dot((u * u).astype(BF16), wd_ref[f0:f0 + FFN_TF, :])
    o_ref[...] = _layer_norm(alpha * h + ff, g_ref[...], b_ref[...])


def _ffn(alpha, h1, wu, wd, ln_g, ln_b):
    t = h1.shape[0]
    row = lambda i: (i, 0)
    return pl.pallas_call(
        functools.partial(_ffn_kernel, alpha),
        out_shape=jax.ShapeDtypeStruct((t, D_MODEL), F32),
        grid=(t // FFN_TM,),
        in_specs=[pl.BlockSpec((FFN_TM, D_MODEL), row),
                  _resident(wu.shape), _resident(wd.shape), _resident(ln_g.shape), _resident(ln_b.shape)],
        out_specs=pl.BlockSpec((FFN_TM, D_MODEL), row),
        compiler_params=pltpu.CompilerParams(
            dimension_semantics=("arbitrary",), vmem_limit_bytes=VMEM_LIMIT_BYTES),
        name="ffn",
    )(h1, wu, wd, ln_g, ln_b)


def kernel(x, mem, w_in, lb_logits, hg_norm_gain, swa_sinks, rel_bias, w_mem_kv, w_branch_hg, w_branch_swa,
           w_branch_mem, w_out, ln1_g, ln1_b, w_up, w_down, ln2_g, ln2_b):
    batch, seq, _ = x.shape
    depth = w_in.shape[0]
    assert depth == 1 and lb_logits.shape[0] == depth + 1
    alpha = (2.0 * depth) ** 0.25
    out_dtype = x.dtype

    w = w_in[0]
    o_sk = 5 * D_MODEL
    o_mq = o_sk + 2 * SWA_KV_HEADS * SWA_HEAD_DIM
    o_gl = o_mq + D_MODEL
    w_in_b = jnp.concatenate(
        [w[:, :o_sk], w[:, o_mq:o_gl], w[:, o_gl:], w[:, o_sk:o_mq]], axis=1).astype(BF16)

    x2d = x.reshape(batch * seq, D_MODEL).astype(F32)
    z = _proj_in(x2d, w_in_b)
    o_a = _hgrn(z, lb_logits.astype(F32), hg_norm_gain.astype(F32), batch, seq)
    o_b = _swa(z, rel_bias.astype(F32), swa_sinks[0].astype(F32), batch, seq)
    kv = _mem_kv(mem.reshape(batch * mem.shape[1], D_MODEL).astype(F32), w_mem_kv[0].astype(BF16))
    h1 = _merge(alpha, x2d, o_a, o_b, z, kv,
                w_branch_hg[0].astype(BF16), w_branch_swa[0].astype(BF16), w_branch_mem[0].astype(BF16),
                w_out[0].astype(BF16), ln1_g.astype(F32), ln1_b.astype(F32), batch, seq)
    h2 = _ffn(alpha, h1, w_up[0].astype(BF16), w_down[0].astype(BF16), ln2_g.astype(F32), ln2_b.astype(F32))
    return h2.reshape(batch, seq, D_MODEL).astype(out_dtype)
```

```python
import functools
import math

import numpy as np
import jax
import jax.numpy as jnp
from jax import lax
from jax.experimental import pallas as pl
from jax.experimental.pallas import tpu as pltpu

F32 = jnp.float32
BF16 = jnp.bfloat16

D_MODEL = 1024
HG_HEADS = 8
HG_DK = 128
HG_DV = 128
HG_CHUNK = 64
SWA_HEADS = 16
SWA_KV_HEADS = 2
SWA_GROUP = SWA_HEADS // SWA_KV_HEADS
SWA_HEAD_DIM = 64
SWA_BLOCK = 128
MEM_HEADS = 4
MEM_HEAD_DIM = 256
NUM_BUCKETS = 32
MAX_DISTANCE = 128
D_FF = 4 * D_MODEL
LN_EPS = 1e-5
RMS_EPS = 1e-6
LOG2E = 1.4426950408889634

Z_HQ, Z_HF, Z_HI, Z_HG, Z_SQ, Z_MQ = (i * D_MODEL for i in range(6))
Z_GL = 6 * D_MODEL
Z_KV = 9 * D_MODEL
Z_COLS = Z_KV + 2 * SWA_KV_HEADS * SWA_HEAD_DIM

VMEM_LIMIT_BYTES = 56 * 1024 * 1024

_NT = (((1,), (1,)), ((), ()))
_TN = (((0,), (0,)), ((), ()))


def _dot(a, b):
    return jnp.dot(a, b, preferred_element_type=F32)


def _resident(shape):
    nd = len(shape)
    return pl.BlockSpec(shape, lambda *_: (0,) * nd, pipeline_mode=pl.Buffered(1))


PROJ_TM = 512
PROJ_TN = 1024


def _proj_in_kernel(x_ref, w_ref, z_ref):
    xb = x_ref[...].astype(BF16)
    for n0 in range(0, Z_COLS, PROJ_TN):
        n1 = min(n0 + PROJ_TN, Z_COLS)
        z_ref[:, n0:n1] = _dot(xb, w_ref[:, n0:n1]).astype(BF16)


def _proj_in(x2d, w_in_b):
    t = x2d.shape[0]
    return pl.pallas_call(
        _proj_in_kernel,
        out_shape=jax.ShapeDtypeStruct((t, Z_COLS), BF16),
        grid=(t // PROJ_TM,),
        in_specs=[pl.BlockSpec((PROJ_TM, D_MODEL), lambda i: (i, 0)),
                  _resident((D_MODEL, Z_COLS))],
        out_specs=pl.BlockSpec((PROJ_TM, Z_COLS), lambda i: (i, 0)),
        compiler_params=pltpu.CompilerParams(
            dimension_semantics=("arbitrary",), vmem_limit_bytes=VMEM_LIMIT_BYTES),
        name="proj_in",
    )(x2d, w_in_b)


HG_TB = 512


def _hgrn_kernel(lbl_ref, gain_ref, tril_ref, zq_ref, zf_ref, zi_ref, zg_ref, o_ref, st_ref, stb_ref):
    @pl.when(pl.program_id(1) == 0)
    def _():
        st_ref[...] = jnp.zeros_like(st_ref)
        stb_ref[...] = jnp.zeros_like(stb_ref)

    l0 = lbl_ref[0:1, :]
    l1 = lbl_ref[1:2, :]
    lmax = jnp.maximum(l0, l1)
    e0 = jnp.exp(l0 - lmax)
    e1 = jnp.exp(l1 - lmax)
    lb_all = e0 / (e0 + e1)
    f_mid = 0.5 * (1.0 + lb_all)
    f_amp = 0.5 * (1.0 - lb_all)
    tril2 = tril_ref[...]
    causal = tril2[:, :HG_CHUNK] > 0
    heads = range(HG_HEADS)
    hcols = [slice(h * HG_DK, (h + 1) * HG_DK) for h in heads]

    def chunk(c, carry):
        r0 = pl.multiple_of(c * HG_CHUNK, HG_CHUNK)
        rows = pl.ds(r0, HG_CHUNK)

        ks, bs = [], []
        for h in heads:
            f = f_mid[:, hcols[h]] + f_amp[:, hcols[h]] * jnp.tanh(0.5 * zf_ref[rows, hcols[h]].astype(F32))
            logf = jnp.log(f)
            ks.append(1.0 - f)
            hi = logf.astype(BF16)
            lo = (logf - hi.astype(F32)).astype(BF16)
            bs.append(_dot(tril2, jnp.concatenate([hi, lo], axis=0)))

        attns, upds, inters, ebls = [], [], [], []
        for h in heads:
            b = bs[h]
            ebl = jnp.exp(b[HG_CHUNK - 1:HG_CHUNK, :])
            q_in = (zq_ref[rows, hcols[h]].astype(F32) * jnp.exp2(b * LOG2E)).astype(BF16)
            kin = ks[h] * jnp.exp2(b * (-LOG2E))
            k_in = kin.astype(BF16)
            k_out = (kin * ebl).astype(BF16)
            v = zi_ref[rows, hcols[h]]
            attns.append(lax.dot_general(q_in, k_in, _NT, preferred_element_type=F32))
            upds.append(lax.dot_general(v, k_out, _TN, preferred_element_type=F32))
            inters.append(lax.dot_general(q_in, stb_ref[h], _NT, preferred_element_type=F32))
            ebls.append(ebl)

        outs = []
        for h in heads:
            attn = jnp.where(causal, attns[h], 0.0).astype(BF16)
            outs.append(_dot(attn, zi_ref[rows, hcols[h]]) + inters[h])
            st = st_ref[h] * ebls[h] + upds[h]
            st_ref[h] = st
            stb_ref[h] = st.astype(BF16)

        for h in heads:
            o = outs[h]
            hg = 0.5 * zg_ref[rows, hcols[h]].astype(F32)
            ms = jnp.mean(o * o, axis=-1, keepdims=True)
            gate = gain_ref[:, hcols[h]] * (hg + hg * jnp.tanh(hg))
            o_ref[rows, hcols[h]] = (o * lax.rsqrt(ms + RMS_EPS) * gate).astype(BF16)
        return carry

    lax.fori_loop(0, HG_TB // HG_CHUNK, chunk, 0, unroll=2)


def _hgrn(z, lb_logits, gain, batch, seq):
    nt = seq // HG_TB
    tril = np.tril(np.ones((HG_CHUNK, HG_CHUNK), np.float32))
    tril = jnp.asarray(np.concatenate([tril, tril], axis=1), BF16)

    def zspec(col):
        return pl.BlockSpec((HG_TB, D_MODEL), lambda b, t, col=col: (b * nt + t, col // D_MODEL))

    return pl.pallas_call(
        _hgrn_kernel,
        out_shape=jax.ShapeDtypeStruct((batch * seq, D_MODEL), BF16),
        grid=(batch, nt),
        in_specs=[_resident(lb_logits.shape), _resident(gain.shape), _resident(tril.shape),
                  zspec(Z_HQ), zspec(Z_HF), zspec(Z_HI), zspec(Z_HG)],
        out_specs=pl.BlockSpec((HG_TB, D_MODEL), lambda b, t: (b * nt + t, 0)),
        scratch_shapes=[pltpu.VMEM((HG_HEADS, HG_DV, HG_DK), F32),
                        pltpu.VMEM((HG_HEADS, HG_DV, HG_DK), BF16)],
        compiler_params=pltpu.CompilerParams(
            dimension_semantics=("arbitrary", "arbitrary"), vmem_limit_bytes=VMEM_LIMIT_BYTES),
        name="hgrn",
    )(lb_logits, gain, tril, z, z, z, z)


def _swa_bucket_table():
    qi = np.arange(SWA_BLOCK)[:, None]
    kj = np.arange(SWA_BLOCK)[None, :]
    n = np.where(kj <= qi, qi - kj, qi + SWA_BLOCK - kj).astype(np.int32)
    max_exact = NUM_BUCKETS // 2
    nf = np.maximum(n, 1).astype(np.float32)
    large = max_exact + (np.log(nf / np.float32(max_exact)) / np.float32(math.log(MAX_DISTANCE / max_exact))
                         * np.float32(NUM_BUCKETS - max_exact)).astype(np.int32)
    large = np.minimum(large, NUM_BUCKETS - 1)
    return np.where(n < max_exact, n, large).astype(np.int32)


SWA_TQ = 512
SWA_NQ = SWA_TQ // SWA_BLOCK
SWA_MASKED = -1e30


def _swa_kernel(relb_ref, sink_ref, bucket_ref, q_ref, kvc_ref, kvp_ref, o_ref, bias_ref, kz_ref, vz_ref):
    t = pl.program_id(1)
    qi = lax.broadcasted_iota(jnp.int32, (SWA_BLOCK, SWA_BLOCK), 0)
    kj = lax.broadcasted_iota(jnp.int32, (SWA_BLOCK, SWA_BLOCK), 1)
    tri = kj <= qi

    @pl.when((pl.program_id(0) == 0) & (t == 0))
    def _():
        bucket = bucket_ref[...]
        for h in range(SWA_HEADS):
            acc = jnp.zeros((SWA_BLOCK, SWA_BLOCK), F32)
            for bk in range(NUM_BUCKETS):
                acc = jnp.where(bucket == bk, relb_ref[bk, h], acc)
            bias_ref[h] = acc
            bias_ref[SWA_HEADS + h] = jnp.where(tri, acc, SWA_MASKED)
        vz_ref[:, :, 128:256] = jnp.ones((2 * SWA_KV_HEADS, SWA_TQ + SWA_BLOCK, 128), BF16)

    lane = lax.broadcasted_iota(jnp.int32, (SWA_BLOCK, 2 * SWA_HEAD_DIM), 1)
    low = lane < SWA_HEAD_DIM
    scale = SWA_HEAD_DIM ** -0.5
    for blk in range(SWA_NQ + 1):
        rows = slice(blk * SWA_BLOCK, (blk + 1) * SWA_BLOCK)
        if blk == 0:
            kk = kvp_ref[:, 0:128].astype(F32) * scale
            vv = kvp_ref[:, 128:256].astype(F32)
        else:
            src = slice((blk - 1) * SWA_BLOCK, blk * SWA_BLOCK)
            kk = kvc_ref[src, 0:128].astype(F32) * scale
            vv = kvc_ref[src, 128:256].astype(F32)
        kk_sw = pltpu.roll(kk, SWA_HEAD_DIM, 1)
        vv_sw = pltpu.roll(vv, SWA_HEAD_DIM, 1)
        for g in range(SWA_KV_HEADS):
            k_lo, k_hi = (kk, kk_sw) if g == 0 else (kk_sw, kk)
            v_lo, v_hi = (vv, vv_sw) if g == 0 else (vv_sw, vv)
            kz_ref[2 * g + 0, rows, :] = jnp.where(low, k_lo, 0.0).astype(BF16)
            kz_ref[2 * g + 1, rows, :] = jnp.where(low, 0.0, k_hi).astype(BF16)
            vz_ref[2 * g + 0, rows, 0:128] = jnp.where(low, v_lo, 0.0).astype(BF16)
            vz_ref[2 * g + 1, rows, 0:128] = jnp.where(low, 0.0, v_hi).astype(BF16)

    def sub_block(i, carry):
        r0 = pl.multiple_of(i * SWA_BLOCK, SWA_BLOCK)
        qrows = pl.ds(r0, SWA_BLOCK)
        krows = pl.ds(r0, 2 * SWA_BLOCK)
        bias_off = jnp.where((t == 0) & (i == 0), SWA_HEADS, 0)
        for pair in range(SWA_HEADS // 2):
            g = (2 * pair) // SWA_GROUP
            qp = q_ref[qrows, pair * 128:(pair + 1) * 128]
            o_pair = None
            for half in range(2):
                h = 2 * pair + half
                slot = 2 * g + half
                s2 = lax.dot_general(qp, kz_ref[slot, krows, :], _NT, preferred_element_type=F32)
                s = jnp.where(tri, s2[:, SWA_BLOCK:], s2[:, :SWA_BLOCK]) + bias_ref[bias_off + h]
                sink = sink_ref[h]
                m = jnp.maximum(jnp.max(s, axis=-1, keepdims=True), sink)
                p = jnp.exp(s - m)
                p2 = jnp.concatenate([jnp.where(tri, 0.0, p), jnp.where(tri, p, 0.0)], axis=1).astype(BF16)
                od = _dot(p2, vz_ref[slot, krows, :])
                den = od[:, 128:] + jnp.exp(sink - m)
                o_h = od[:, :128] * (1.0 / den)
                o_pair = o_h if o_pair is None else o_pair + o_h
            o_ref[qrows, pair * 128:(pair + 1) * 128] = o_pair.astype(BF16)
        return carry

    lax.fori_loop(0, SWA_NQ, sub_block, 0)


def _swa(z, rel_bias, sinks, batch, seq):
    nt = seq // SWA_TQ
    bucket = jnp.asarray(_swa_bucket_table())
    kv_col = Z_KV // 256
    smem = pl.BlockSpec(memory_space=pltpu.SMEM)
    return pl.pallas_call(
        _swa_kernel,
        out_shape=jax.ShapeDtypeStruct((batch * seq, D_MODEL), BF16),
        grid=(batch, nt),
        in_specs=[smem, smem, _resident(bucket.shape),
                  pl.BlockSpec((SWA_TQ, D_MODEL), lambda b, t: (b * nt + t, Z_SQ // D_MODEL)),
                  pl.BlockSpec((SWA_TQ, 256), lambda b, t: (b * nt + t, kv_col)),
                  pl.BlockSpec((SWA_BLOCK, 256),
                               lambda b, t: ((b * nt + t) * SWA_NQ - jnp.minimum(t, 1), kv_col))],
        out_specs=pl.BlockSpec((SWA_TQ, D_MODEL), lambda b, t: (b * nt + t, 0)),
        scratch_shapes=[pltpu.VMEM((2 * SWA_HEADS, SWA_BLOCK, SWA_BLOCK), F32),
                        pltpu.VMEM((2 * SWA_KV_HEADS, SWA_TQ + SWA_BLOCK, 128), BF16),
                        pltpu.VMEM((2 * SWA_KV_HEADS, SWA_TQ + SWA_BLOCK, 256), BF16)],
        compiler_params=pltpu.CompilerParams(
            dimension_semantics=("arbitrary", "arbitrary"), vmem_limit_bytes=VMEM_LIMIT_BYTES),
        name="swa",
    )(rel_bias, sinks, bucket, z, z, z)


def _mem_kv_kernel(mem_ref, w_ref, kv_ref):
    kv_ref[...] = _dot(mem_ref[...].astype(BF16), w_ref[...]).astype(BF16)


def _mem_kv(mem2d, w_b):
    return pl.pallas_call(
        _mem_kv_kernel,
        out_shape=jax.ShapeDtypeStruct((mem2d.shape[0], w_b.shape[1]), BF16),
        compiler_params=pltpu.CompilerParams(vmem_limit_bytes=VMEM_LIMIT_BYTES),
        name="mem_kv",
    )(mem2d, w_b)


MERGE_TM = 256


def _layer_norm(y, g, b):
    mu = jnp.mean(y, axis=-1, keepdims=True)
    yc = y - mu
    var = jnp.mean(yc * yc, axis=-1, keepdims=True)
    return yc * lax.rsqrt(var + LN_EPS) * g + b


def _merge_kernel(alpha, x_ref, oa_ref, ob_ref, mq_ref, gl_ref, kv_ref, wa_ref, wb_ref, wc_ref, wo_ref,
                  g_ref, b_ref, h_ref):
    scale = MEM_HEAD_DIM ** -0.5
    merged = jnp.zeros((MERGE_TM, D_MODEL), F32)
    oc = []
    for h in range(MEM_HEADS):
        cols = slice(h * MEM_HEAD_DIM, (h + 1) * MEM_HEAD_DIM)
        vcols = slice(MEM_HEADS * MEM_HEAD_DIM + h * MEM_HEAD_DIM, MEM_HEADS * MEM_HEAD_DIM + (h + 1) * MEM_HEAD_DIM)
        s = lax.dot_general(mq_ref[:, cols], kv_ref[:, cols], _NT, preferred_element_type=F32) * scale
        m = jnp.max(s, axis=-1, keepdims=True)
        p = jnp.exp(s - m)
        p = p * (1.0 / jnp.sum(p, axis=-1, keepdims=True))
        oc.append(_dot(p.astype(BF16), kv_ref[:, vcols]).astype(BF16))
    o_c = jnp.concatenate(oc, axis=1)
    branches = ((oa_ref[...], wa_ref), (ob_ref[...], wb_ref), (o_c, wc_ref))
    for i, (o, w_ref) in enumerate(branches):
        gate = jax.nn.sigmoid(gl_ref[:, i * D_MODEL:(i + 1) * D_MODEL].astype(F32))
        merged = merged + gate * _dot(o, w_ref[...])
    mix = _dot(merged.astype(BF16), wo_ref[...])
    h_ref[...] = _layer_norm(alpha * x_ref[...] + mix, g_ref[...], b_ref[...])


def _merge(alpha, x2d, o_a, o_b, z, kv, wa, wb, wc, wo, ln_g, ln_b, batch, seq):
    t = x2d.shape[0]
    per_batch = seq // MERGE_TM
    row = lambda i: (i, 0)
    return pl.pallas_call(
        functools.partial(_merge_kernel, alpha),
        out_shape=jax.ShapeDtypeStruct((t, D_MODEL), F32),
        grid=(t // MERGE_TM,),
        in_specs=[pl.BlockSpec((MERGE_TM, D_MODEL), row),
                  pl.BlockSpec((MERGE_TM, D_MODEL), row),
                  pl.BlockSpec((MERGE_TM, D_MODEL), row),
                  pl.BlockSpec((MERGE_TM, D_MODEL), lambda i: (i, Z_MQ // D_MODEL)),
                  pl.BlockSpec((MERGE_TM, 3 * D_MODEL), lambda i: (i, Z_GL // (3 * D_MODEL))),
                  pl.BlockSpec((kv.shape[0] // batch, kv.shape[1]), lambda i: (i // per_batch, 0)),
                  _resident(wa.shape), _resident(wb.shape), _resident(wc.shape), _resident(wo.shape),
                  _resident(ln_g.shape), _resident(ln_b.shape)],
        out_specs=pl.BlockSpec((MERGE_TM, D_MODEL), row),
        compiler_params=pltpu.CompilerParams(
            dimension_semantics=("arbitrary",), vmem_limit_bytes=VMEM_LIMIT_BYTES),
        name="merge",
    )(x2d, o_a, o_b, z, z, kv, wa, wb, wc, wo, ln_g, ln_b)


FFN_TM = 512
FFN_TF = 1024


def _ffn_kernel(alpha, h_ref, wu_ref, wd_ref, g_ref, b_ref, o_ref):
    h = h_ref[...]
    hb = h.astype(BF16)
    ff = jnp.zeros((FFN_TM, D_MODEL), F32)
    for f0 in range(0, D_FF, FFN_TF):
        u = jnp.maximum(_dot(hb, wu_ref[:, f0:f0 + FFN_TF]), 0.0)
        ff = ff + _dot((u * u).astype(BF16), wd_ref[f0:f0 + FFN_TF, :])
    o_ref[...] = _layer_norm(alpha * h + ff, g_ref[...], b_ref[...])


def _ffn(alpha, h1, wu, wd, ln_g, ln_b):
    t = h1.shape[0]
    row = lambda i: (i, 0)
    return pl.pallas_call(
        functools.partial(_ffn_kernel, alpha),
        out_shape=jax.ShapeDtypeStruct((t, D_MODEL), F32),
        grid=(t // FFN_TM,),
        in_specs=[pl.BlockSpec((FFN_TM, D_MODEL), row),
                  _resident(wu.shape), _resident(wd.shape), _resident(ln_g.shape), _resident(ln_b.shape)],
        out_specs=pl.BlockSpec((FFN_TM, D_MODEL), row),
        compiler_params=pltpu.CompilerParams(
            dimension_semantics=("arbitrary",), vmem_limit_bytes=VMEM_LIMIT_BYTES),
        name="ffn",
    )(h1, wu, wd, ln_g, ln_b)


def kernel(x, mem, w_in, lb_logits, hg_norm_gain, swa_sinks, rel_bias, w_mem_kv, w_branch_hg, w_branch_swa,
           w_branch_mem, w_out, ln1_g, ln1_b, w_up, w_down, ln2_g, ln2_b):
    batch, seq, _ = x.shape
    depth = w_in.shape[0]
    assert depth == 1 and lb_logits.shape[0] == depth + 1
    alpha = (2.0 * depth) ** 0.25
    out_dtype = x.dtype

    w = w_in[0]
    o_sk = 5 * D_MODEL
    o_mq = o_sk + 2 * SWA_KV_HEADS * SWA_HEAD_DIM
    o_gl = o_mq + D_MODEL
    w_in_b = jnp.concatenate(
        [w[:, :o_sk], w[:, o_mq:o_gl], w[:, o_gl:], w[:, o_sk:o_mq]], axis=1).astype(BF16)

    x2d = x.reshape(batch * seq, D_MODEL).astype(F32)
    z = _proj_in(x2d, w_in_b)
    o_a = _hgrn(z, lb_logits.astype(F32), hg_norm_gain.astype(F32), batch, seq)
    o_b = _swa(z, rel_bias.astype(F32), swa_sinks[0].astype(F32), batch, seq)
    kv = _mem_kv(mem.reshape(batch * mem.shape[1], D_MODEL).astype(F32), w_mem_kv[0].astype(BF16))
    h1 = _merge(alpha, x2d, o_a, o_b, z, kv,
                w_branch_hg[0].astype(BF16), w_branch_swa[0].astype(BF16), w_branch_mem[0].astype(BF16),
                w_out[0].astype(BF16), ln1_g.astype(F32), ln1_b.astype(F32), batch, seq)
    h2 = _ffn(alpha, h1, w_up[0].astype(BF16), w_down[0].astype(BF16), ln2_g.astype(F32), ln2_b.astype(F32))
    return h2.reshape(batch, seq, D_MODEL).astype(out_dtype)
```

```python
import functools
import math

import numpy as np
import jax
import jax.numpy as jnp
from jax import lax
from jax.experimental import pallas as pl
from jax.experimental.pallas import tpu as pltpu

F32 = jnp.float32
BF16 = jnp.bfloat16

D_MODEL = 1024
HG_HEADS = 8
HG_DK = 128
HG_DV = 128
HG_CHUNK = 64
SWA_HEADS = 16
SWA_KV_HEADS = 2
SWA_GROUP = SWA_HEADS // SWA_KV_HEADS
SWA_HEAD_DIM = 64
SWA_BLOCK = 128
MEM_HEADS = 4
MEM_HEAD_DIM = 256
NUM_BUCKETS = 32
MAX_DISTANCE = 128
D_FF = 4 * D_MODEL
LN_EPS = 1e-5
RMS_EPS = 1e-6
LOG2E = 1.4426950408889634

Z_HQ, Z_HF, Z_HI, Z_HG, Z_SQ, Z_MQ = (i * D_MODEL for i in range(6))
Z_GL = 6 * D_MODEL
Z_KV = 9 * D_MODEL
Z_COLS = Z_KV + 2 * SWA_KV_HEADS * SWA_HEAD_DIM

VMEM_LIMIT_BYTES = 56 * 1024 * 1024

_NT = (((1,), (1,)), ((), ()))
_TN = (((0,), (0,)), ((), ()))


def _dot(a, b):
    return jnp.dot(a, b, preferred_element_type=F32)


def _resident(shape):
    nd = len(shape)
    return pl.BlockSpec(shape, lambda *_: (0,) * nd, pipeline_mode=pl.Buffered(1))


PROJ_TM = 512
PROJ_TN = 1024


def _proj_col_chunks():
    kv_w = 2 * SWA_KV_HEADS * SWA_HEAD_DIM
    src_kv = Z_MQ
    segments = ((0, 0, Z_MQ), (src_kv + kv_w, Z_MQ, Z_KV - Z_MQ), (src_kv, Z_KV, kv_w))
    chunks = []
    for src, dst, width in segments:
        for off in range(0, width, PROJ_TN):
            chunks.append((src + off, dst + off, min(PROJ_TN, width - off)))
    return chunks


def _proj_in_kernel(x_ref, w_ref, z_ref):
    xb = x_ref[...].astype(BF16)
    for src, dst, width in _proj_col_chunks():
        z_ref[:, dst:dst + width] = _dot(xb, w_ref[:, src:src + width]).astype(BF16)


def _proj_in(x2d, w_in_b):
    t = x2d.shape[0]
    return pl.pallas_call(
        _proj_in_kernel,
        out_shape=jax.ShapeDtypeStruct((t, Z_COLS), BF16),
        grid=(t // PROJ_TM,),
        in_specs=[pl.BlockSpec((PROJ_TM, D_MODEL), lambda i: (i, 0)),
                  _resident((D_MODEL, Z_COLS))],
        out_specs=pl.BlockSpec((PROJ_TM, Z_COLS), lambda i: (i, 0)),
        compiler_params=pltpu.CompilerParams(
            dimension_semantics=("arbitrary",), vmem_limit_bytes=VMEM_LIMIT_BYTES),
        name="proj_in",
    )(x2d, w_in_b)


HG_TB = 512


def _hgrn_kernel(lbl_ref, gain_ref, tril_ref, zq_ref, zf_ref, zi_ref, zg_ref, o_ref, st_ref, stb_ref):
    @pl.when(pl.program_id(1) == 0)
    def _():
        st_ref[...] = jnp.zeros_like(st_ref)
        stb_ref[...] = jnp.zeros_like(stb_ref)

    l0 = lbl_ref[0:1, :]
    l1 = lbl_ref[1:2, :]
    lmax = jnp.maximum(l0, l1)
    e0 = jnp.exp(l0 - lmax)
    e1 = jnp.exp(l1 - lmax)
    lb_all = e0 / (e0 + e1)
    f_mid = 0.5 * (1.0 + lb_all)
    f_amp = 0.5 * (1.0 - lb_all)
    tril2 = tril_ref[...]
    causal = tril2[:, :HG_CHUNK] > 0
    heads = range(HG_HEADS)
    hcols = [slice(h * HG_DK, (h + 1) * HG_DK) for h in heads]

    def chunk(c, carry):
        r0 = pl.multiple_of(c * HG_CHUNK, HG_CHUNK)
        rows = pl.ds(r0, HG_CHUNK)

        ks, bs = [], []
        for h in heads:
            f = f_mid[:, hcols[h]] + f_amp[:, hcols[h]] * jnp.tanh(0.5 * zf_ref[rows, hcols[h]].astype(F32))
            logf = jnp.log(f)
            ks.append(1.0 - f)
            hi = logf.astype(BF16)
            lo = (logf - hi.astype(F32)).astype(BF16)
            bs.append(_dot(tril2, jnp.concatenate([hi, lo], axis=0)))

        attns, upds, inters, ebls = [], [], [], []
        for h in heads:
            b = bs[h]
            ebl = jnp.exp(b[HG_CHUNK - 1:HG_CHUNK, :])
            q_in = (zq_ref[rows, hcols[h]].astype(F32) * jnp.exp2(b * LOG2E)).astype(BF16)
            kin = ks[h] * jnp.exp2(b * (-LOG2E))
            k_in = kin.astype(BF16)
            k_out = (kin * ebl).astype(BF16)
            v = zi_ref[rows, hcols[h]]
            attns.append(lax.dot_general(q_in, k_in, _NT, preferred_element_type=F32))
            upds.append(lax.dot_general(v, k_out, _TN, preferred_element_type=F32))
            inters.append(lax.dot_general(q_in, stb_ref[h], _NT, preferred_element_type=F32))
            ebls.append(ebl)

        outs = []
        for h in heads:
            attn = jnp.where(causal, attns[h], 0.0).astype(BF16)
            outs.append(_dot(attn, zi_ref[rows, hcols[h]]) + inters[h])
            st = st_ref[h] * ebls[h] + upds[h]
            st_ref[h] = st
            stb_ref[h] = st.astype(BF16)

        for h in heads:
            o = outs[h]
            hg = 0.5 * zg_ref[rows, hcols[h]].astype(F32)
            ms = jnp.mean(o * o, axis=-1, keepdims=True)
            gate = gain_ref[:, hcols[h]] * (hg + hg * jnp.tanh(hg))
            o_ref[rows, hcols[h]] = (o * lax.rsqrt(ms + RMS_EPS) * gate).astype(BF16)
        return carry

    lax.fori_loop(0, HG_TB // HG_CHUNK, chunk, 0, unroll=2)


def _hgrn(z, lb_logits, gain, batch, seq):
    nt = seq // HG_TB
    tril = np.tril(np.ones((HG_CHUNK, HG_CHUNK), np.float32))
    tril = jnp.asarray(np.concatenate([tril, tril], axis=1), BF16)

    def zspec(col):
        return pl.BlockSpec((HG_TB, D_MODEL), lambda b, t, col=col: (b * nt + t, col // D_MODEL))

    return pl.pallas_call(
        _hgrn_kernel,
        out_shape=jax.ShapeDtypeStruct((batch * seq, D_MODEL), BF16),
        grid=(batch, nt),
        in_specs=[_resident(lb_logits.shape), _resident(gain.shape), _resident(tril.shape),
                  zspec(Z_HQ), zspec(Z_HF), zspec(Z_HI), zspec(Z_HG)],
        out_specs=pl.BlockSpec((HG_TB, D_MODEL), lambda b, t: (b * nt + t, 0)),
        scratch_shapes=[pltpu.VMEM((HG_HEADS, HG_DV, HG_DK), F32),
                        pltpu.VMEM((HG_HEADS, HG_DV, HG_DK), BF16)],
        compiler_params=pltpu.CompilerParams(
            dimension_semantics=("arbitrary", "arbitrary"), vmem_limit_bytes=VMEM_LIMIT_BYTES),
        name="hgrn",
    )(lb_logits, gain, tril, z, z, z, z)


def _swa_bucket_table():
    qi = np.arange(SWA_BLOCK)[:, None]
    kj = np.arange(SWA_BLOCK)[None, :]
    n = np.where(kj <= qi, qi - kj, qi + SWA_BLOCK - kj).astype(np.int32)
    max_exact = NUM_BUCKETS // 2
    nf = np.maximum(n, 1).astype(np.float32)
    large = max_exact + (np.log(nf / np.float32(max_exact)) / np.float32(math.log(MAX_DISTANCE / max_exact))
                         * np.float32(NUM_BUCKETS - max_exact)).astype(np.int32)
    large = np.minimum(large, NUM_BUCKETS - 1)
    return np.where(n < max_exact, n, large).astype(np.int32)


SWA_TQ = 512
SWA_NQ = SWA_TQ // SWA_BLOCK
SWA_MASKED = -1e30


def _swa_kernel(relb_ref, sink_ref, bucket_ref, q_ref, kvc_ref, kvp_ref, o_ref, bias_ref, kz_ref, vz_ref):
    t = pl.program_id(1)
    qi = lax.broadcasted_iota(jnp.int32, (SWA_BLOCK, SWA_BLOCK), 0)
    kj = lax.broadcasted_iota(jnp.int32, (SWA_BLOCK, SWA_BLOCK), 1)
    tri = kj <= qi

    @pl.when((pl.program_id(0) == 0) & (t == 0))
    def _():
        bucket = bucket_ref[...]
        for h in range(SWA_HEADS):
            acc = jnp.zeros((SWA_BLOCK, SWA_BLOCK), F32)
            for bk in range(NUM_BUCKETS):
                acc = jnp.where(bucket == bk, relb_ref[bk, h], acc)
            bias_ref[h] = acc
            bias_ref[SWA_HEADS + h] = jnp.where(tri, acc, SWA_MASKED)
        vz_ref[:, :, 128:256] = jnp.ones((2 * SWA_KV_HEADS, SWA_TQ + SWA_BLOCK, 128), BF16)

    lane = lax.broadcasted_iota(jnp.int32, (SWA_BLOCK, 2 * SWA_HEAD_DIM), 1)
    low = lane < SWA_HEAD_DIM
    scale = SWA_HEAD_DIM ** -0.5
    for blk in range(SWA_NQ + 1):
        rows = slice(blk * SWA_BLOCK, (blk + 1) * SWA_BLOCK)
        if blk == 0:
            kk = kvp_ref[:, 0:128].astype(F32) * scale
            vv = kvp_ref[:, 128:256].astype(F32)
        else:
            src = slice((blk - 1) * SWA_BLOCK, blk * SWA_BLOCK)
            kk = kvc_ref[src, 0:128].astype(F32) * scale
            vv = kvc_ref[src, 128:256].astype(F32)
        kk_sw = pltpu.roll(kk, SWA_HEAD_DIM, 1)
        vv_sw = pltpu.roll(vv, SWA_HEAD_DIM, 1)
        for g in range(SWA_KV_HEADS):
            k_lo, k_hi = (kk, kk_sw) if g == 0 else (kk_sw, kk)
            v_lo, v_hi = (vv, vv_sw) if g == 0 else (vv_sw, vv)
            kz_ref[2 * g + 0, rows, :] = jnp.where(low, k_lo, 0.0).astype(BF16)
            kz_ref[2 * g + 1, rows, :] = jnp.where(low, 0.0, k_hi).astype(BF16)
            vz_ref[2 * g + 0, rows, 0:128] = jnp.where(low, v_lo, 0.0).astype(BF16)
            vz_ref[2 * g + 1, rows, 0:128] = jnp.where(low, 0.0, v_hi).astype(BF16)

    def sub_block(i, carry):
        r0 = pl.multiple_of(i * SWA_BLOCK, SWA_BLOCK)
        qrows = pl.ds(r0, SWA_BLOCK)
        krows = pl.ds(r0, 2 * SWA_BLOCK)
        bias_off = jnp.where((t == 0) & (i == 0), SWA_HEADS, 0)
        for pair in range(SWA_HEADS // 2):
            g = (2 * pair) // SWA_GROUP
            qp = q_ref[qrows, pair * 128:(pair + 1) * 128]
            o_pair = None
            for half in range(2):
                h = 2 * pair + half
                slot = 2 * g + half
                s2 = lax.dot_general(qp, kz_ref[slot, krows, :], _NT, preferred_element_type=F32)
                s = jnp.where(tri, s2[:, SWA_BLOCK:], s2[:, :SWA_BLOCK]) + bias_ref[bias_off + h]
                sink = sink_ref[h]
                m = jnp.maximum(jnp.max(s, axis=-1, keepdims=True), sink)
                p = jnp.exp(s - m)
                p2 = jnp.concatenate([jnp.where(tri, 0.0, p), jnp.where(tri, p, 0.0)], axis=1).astype(BF16)
                od = _dot(p2, vz_ref[slot, krows, :])
                den = od[:, 128:] + jnp.exp(sink - m)
                o_h = od[:, :128] * (1.0 / den)
                o_pair = o_h if o_pair is None else o_pair + o_h
            o_ref[qrows, pair * 128:(pair + 1) * 128] = o_pair.astype(BF16)
        return carry

    lax.fori_loop(0, SWA_NQ, sub_block, 0)


def _swa(z, rel_bias, sinks, batch, seq):
    nt = seq // SWA_TQ
    bucket = jnp.asarray(_swa_bucket_table())
    kv_col = Z_KV // 256
    smem = pl.BlockSpec(memory_space=pltpu.SMEM)
    return pl.pallas_call(
        _swa_kernel,
        out_shape=jax.ShapeDtypeStruct((batch * seq, D_MODEL), BF16),
        grid=(batch, nt),
        in_specs=[smem, smem, _resident(bucket.shape),
                  pl.BlockSpec((SWA_TQ, D_MODEL), lambda b, t: (b * nt + t, Z_SQ // D_MODEL)),
                  pl.BlockSpec((SWA_TQ, 256), lambda b, t: (b * nt + t, kv_col)),
                  pl.BlockSpec((SWA_BLOCK, 256),
                               lambda b, t: ((b * nt + t) * SWA_NQ - jnp.minimum(t, 1), kv_col))],
        out_specs=pl.BlockSpec((SWA_TQ, D_MODEL), lambda b, t: (b * nt + t, 0)),
        scratch_shapes=[pltpu.VMEM((2 * SWA_HEADS, SWA_BLOCK, SWA_BLOCK), F32),
                        pltpu.VMEM((2 * SWA_KV_HEADS, SWA_TQ + SWA_BLOCK, 128), BF16),
                        pltpu.VMEM((2 * SWA_KV_HEADS, SWA_TQ + SWA_BLOCK, 256), BF16)],
        compiler_params=pltpu.CompilerParams(
            dimension_semantics=("arbitrary", "arbitrary"), vmem_limit_bytes=VMEM_LIMIT_BYTES),
        name="swa",
    )(rel_bias, sinks, bucket, z, z, z)


def _mem_kv_kernel(mem_ref, w_ref, kv_ref):
    mem_w = MEM_HEADS * MEM_HEAD_DIM
    mb = mem_ref[...].astype(BF16)
    kv_ref[:, :mem_w] = (_dot(mb, w_ref[:, :mem_w]) * MEM_HEAD_DIM ** -0.5).astype(BF16)
    kv_ref[:, mem_w:] = _dot(mb, w_ref[:, mem_w:]).astype(BF16)


def _mem_kv(mem2d, w_b):
    return pl.pallas_call(
        _mem_kv_kernel,
        out_shape=jax.ShapeDtypeStruct((mem2d.shape[0], w_b.shape[1]), BF16),
        compiler_params=pltpu.CompilerParams(vmem_limit_bytes=VMEM_LIMIT_BYTES),
        name="mem_kv",
    )(mem2d, w_b)


MERGE_TM = 512
MERGE_SUB = 256


def _layer_norm(y, g, b):
    mu = jnp.mean(y, axis=-1, keepdims=True)
    yc = y - mu
    var = jnp.mean(yc * yc, axis=-1, keepdims=True)
    return yc * lax.rsqrt(var + LN_EPS) * g + b


def _merge_kernel(alpha, x_ref, oa_ref, ob_ref, mq_ref, gl_ref, kv_ref, wa_ref, wb_ref, wc_ref, wo_ref,
                  g_ref, b_ref, h_ref):
    subs = [slice(r, r + MERGE_SUB) for r in range(0, MERGE_TM, MERGE_SUB)]
    mem_w = MEM_HEADS * MEM_HEAD_DIM
    hcols = [slice(h * MEM_HEAD_DIM, (h + 1) * MEM_HEAD_DIM) for h in range(MEM_HEADS)]

    def gate(rows, i):
        return jax.nn.sigmoid(gl_ref[rows, i * D_MODEL:(i + 1) * D_MODEL].astype(F32))

    scores = [[lax.dot_general(mq_ref[rows, c], kv_ref[:, c], _NT, preferred_element_type=F32) for c in hcols]
              for rows in subs]
    merged = [gate(rows, 0) * _dot(oa_ref[rows, :], wa_ref[...]) for rows in subs]
    probs = []
    for s_heads in scores:
        ps = []
        for s in s_heads:
            p = jnp.exp(s - jnp.max(s, axis=-1, keepdims=True))
            ps.append((p * (1.0 / jnp.sum(p, axis=-1, keepdims=True))).astype(BF16))
        probs.append(ps)
    o_c = [jnp.concatenate([_dot(p, kv_ref[:, mem_w + h * MEM_HEAD_DIM:mem_w + (h + 1) * MEM_HEAD_DIM]).astype(BF16)
                            for h, p in enumerate(ps)], axis=1) for ps in probs]
    merged = [m + gate(rows, 1) * _dot(ob_ref[rows, :], wb_ref[...]) for m, rows in zip(merged, subs)]
    merged = [m + gate(rows, 2) * _dot(oc, wc_ref[...]) for m, rows, oc in zip(merged, subs, o_c)]
    for m, rows in zip(merged, subs):
        mix = _dot(m.astype(BF16), wo_ref[...])
        h_ref[rows, :] = _layer_norm(alpha * x_ref[rows, :] + mix, g_ref[...], b_ref[...])


def _merge(alpha, x2d, o_a, o_b, z, kv, wa, wb, wc, wo, ln_g, ln_b, batch, seq):
    t = x2d.shape[0]
    per_batch = seq // MERGE_TM
    row = lambda i: (i, 0)
    return pl.pallas_call(
        functools.partial(_merge_kernel, alpha),
        out_shape=jax.ShapeDtypeStruct((t, D_MODEL), F32),
        grid=(t // MERGE_TM,),
        in_specs=[pl.BlockSpec((MERGE_TM, D_MODEL), row),
                  pl.BlockSpec((MERGE_TM, D_MODEL), row),
                  pl.BlockSpec((MERGE_TM, D_MODEL), row),
                  pl.BlockSpec((MERGE_TM, D_MODEL), lambda i: (i, Z_MQ // D_MODEL)),
                  pl.BlockSpec((MERGE_TM, 3 * D_MODEL), lambda i: (i, Z_GL // (3 * D_MODEL))),
                  pl.BlockSpec((kv.shape[0] // batch, kv.shape[1]), lambda i: (i // per_batch, 0)),
                  _resident(wa.shape), _resident(wb.shape), _resident(wc.shape), _resident(wo.shape),
                  _resident(ln_g.shape), _resident(ln_b.shape)],
        out_specs=pl.BlockSpec((MERGE_TM, D_MODEL), row),
        compiler_params=pltpu.CompilerParams(
            dimension_semantics=("arbitrary",), vmem_limit_bytes=VMEM_LIMIT_BYTES),
        name="merge",
    )(x2d, o_a, o_b, z, z, kv, wa, wb, wc, wo, ln_g, ln_b)


FFN_TM = 512
FFN_TF = 1024


def _ffn_kernel(alpha, h_ref, wu_ref, wd_ref, g_ref, b_ref, o_ref):
    h = h_ref[...]
    hb = h.astype(BF16)
    ff = jnp.zeros((FFN_TM, D_MODEL), F32)
    for f0 in range(0, D_FF, FFN_TF):
        u = jnp.maximum(_dot(hb, wu_ref[:, f0:f0 + FFN_TF]), 0.0)
        ff = ff + _dot((u * u).astype(BF16), wd_ref[f0:f0 + FFN_TF, :])
    o_ref[...] = _layer_norm(alpha * h + ff, g_ref[...], b_ref[...])


def _ffn(alpha, h1, wu, wd, ln_g, ln_b):
    t = h1.shape[0]
    row = lambda i: (i, 0)
    return pl.pallas_call(
        functools.partial(_ffn_kernel, alpha),
        out_shape=jax.ShapeDtypeStruct((t, D_MODEL), F32),
        grid=(t // FFN_TM,),
        in_specs=[pl.BlockSpec((FFN_TM, D_MODEL), row),
                  _resident(wu.shape), _resident(wd.shape), _resident(ln_g.shape), _resident(ln_b.shape)],
        out_specs=pl.BlockSpec((FFN_TM, D_MODEL), row),
        compiler_params=pltpu.CompilerParams(
            dimension_semantics=("arbitrary",), vmem_limit_bytes=VMEM_LIMIT_BYTES),
        name="ffn",
    )(h1, wu, wd, ln_g, ln_b)


def kernel(x, mem, w_in, lb_logits, hg_norm_gain, swa_sinks, rel_bias, w_mem_kv, w_branch_hg, w_branch_swa,
           w_branch_mem, w_out, ln1_g, ln1_b, w_up, w_down, ln2_g, ln2_b):
    batch, seq, _ = x.shape
    depth = w_in.shape[0]
    assert depth == 1 and lb_logits.shape[0] == depth + 1
    alpha = (2.0 * depth) ** 0.25
    out_dtype = x.dtype

    w_in_b = w_in[0].astype(BF16)
    x2d = x.reshape(batch * seq, D_MODEL).astype(F32)
    z = _proj_in(x2d, w_in_b)
    o_a = _hgrn(z, lb_logits.astype(F32), hg_norm_gain.astype(F32), batch, seq)
    o_b = _swa(z, rel_bias.astype(F32), swa_sinks[0].astype(F32), batch, seq)
    kv = _mem_kv(mem.reshape(batch * mem.shape[1], D_MODEL).astype(F32), w_mem_kv[0].astype(BF16))
    h1 = _merge(alpha, x2d, o_a, o_b, z, kv,
                w_branch_hg[0].astype(BF16), w_branch_swa[0].astype(BF16), w_branch_mem[0].astype(BF16),
                w_out[0].astype(BF16), ln1_g.astype(F32), ln1_b.astype(F32), batch, seq)
    h2 = _ffn(alpha, h1, w_up[0].astype(BF16), w_down[0].astype(BF16), ln2_g.astype(F32), ln2_b.astype(F32))
    return h2.reshape(batch, seq, D_MODEL).astype(out_dtype)
```

```python
import functools
import math

import numpy as np
import jax
import jax.numpy as jnp
from jax import lax
from jax.experimental import pallas as pl
from jax.experimental.pallas import tpu as pltpu

F32 = jnp.float32
BF16 = jnp.bfloat16

D_MODEL = 1024
HG_HEADS = 8
HG_DK = 128
HG_DV = 128
HG_CHUNK = 64
SWA_HEADS = 16
SWA_KV_HEADS = 2
SWA_GROUP = SWA_HEADS // SWA_KV_HEADS
SWA_HEAD_DIM = 64
SWA_BLOCK = 128
MEM_HEADS = 4
MEM_HEAD_DIM = 256
NUM_BUCKETS = 32
MAX_DISTANCE = 128
D_FF = 4 * D_MODEL
LN_EPS = 1e-5
RMS_EPS = 1e-6
LOG2E = 1.4426950408889634

Z_HQ, Z_HF, Z_HI, Z_HG, Z_SQ, Z_MQ = (i * D_MODEL for i in range(6))
Z_GL = 6 * D_MODEL
Z_KV = 9 * D_MODEL
Z_COLS = Z_KV + 2 * SWA_KV_HEADS * SWA_HEAD_DIM

VMEM_LIMIT_BYTES = 56 * 1024 * 1024

_NT = (((1,), (1,)), ((), ()))
_TN = (((0,), (0,)), ((), ()))


def _dot(a, b):
    return jnp.dot(a, b, preferred_element_type=F32)


def _resident(shape):
    nd = len(shape)
    return pl.BlockSpec(shape, lambda *_: (0,) * nd, pipeline_mode=pl.Buffered(1))


PROJ_TM = 512
PROJ_TN = 1024


def _proj_col_chunks():
    kv_w = 2 * SWA_KV_HEADS * SWA_HEAD_DIM
    src_kv = Z_MQ
    segments = ((0, 0, Z_MQ), (src_kv + kv_w, Z_MQ, Z_KV - Z_MQ), (src_kv, Z_KV, kv_w))
    chunks = []
    for src, dst, width in segments:
        for off in range(0, width, PROJ_TN):
            chunks.append((src + off, dst + off, min(PROJ_TN, width - off)))
    return chunks


def _proj_in_kernel(x_ref, w_ref, z_ref):
    xb = x_ref[...].astype(BF16)
    for src, dst, width in _proj_col_chunks():
        z_ref[:, dst:dst + width] = _dot(xb, w_ref[:, src:src + width]).astype(BF16)


def _proj_in(x2d, w_in_b):
    t = x2d.shape[0]
    return pl.pallas_call(
        _proj_in_kernel,
        out_shape=jax.ShapeDtypeStruct((t, Z_COLS), BF16),
        grid=(t // PROJ_TM,),
        in_specs=[pl.BlockSpec((PROJ_TM, D_MODEL), lambda i: (i, 0)),
                  _resident((D_MODEL, Z_COLS))],
        out_specs=pl.BlockSpec((PROJ_TM, Z_COLS), lambda i: (i, 0)),
        compiler_params=pltpu.CompilerParams(
            dimension_semantics=("arbitrary",), vmem_limit_bytes=VMEM_LIMIT_BYTES),
        name="proj_in",
    )(x2d, w_in_b)


HG_TB = 512


def _hgrn_kernel(lbl_ref, gain_ref, tril_ref, zq_ref, zf_ref, zi_ref, zg_ref, o_ref, st_ref, stb_ref):
    @pl.when(pl.program_id(1) == 0)
    def _():
        st_ref[...] = jnp.zeros_like(st_ref)
        stb_ref[...] = jnp.zeros_like(stb_ref)

    l0 = lbl_ref[0:1, :]
    l1 = lbl_ref[1:2, :]
    lmax = jnp.maximum(l0, l1)
    e0 = jnp.exp(l0 - lmax)
    e1 = jnp.exp(l1 - lmax)
    lb_all = e0 / (e0 + e1)
    f_mid = 0.5 * (1.0 + lb_all)
    f_amp = 0.5 * (1.0 - lb_all)
    tril2 = tril_ref[...]
    causal = tril2[:, :HG_CHUNK] > 0
    heads = range(HG_HEADS)
    hcols = [slice(h * HG_DK, (h + 1) * HG_DK) for h in heads]

    def chunk(c, carry):
        r0 = pl.multiple_of(c * HG_CHUNK, HG_CHUNK)
        rows = pl.ds(r0, HG_CHUNK)

        ks, bs = [], []
        for h in heads:
            f = f_mid[:, hcols[h]] + f_amp[:, hcols[h]] * jnp.tanh(0.5 * zf_ref[rows, hcols[h]].astype(F32))
            logf = jnp.log(f)
            ks.append(1.0 - f)
            hi = logf.astype(BF16)
            lo = (logf - hi.astype(F32)).astype(BF16)
            bs.append(_dot(tril2, jnp.concatenate([hi, lo], axis=0)))

        attns, upds, inters, ebls = [], [], [], []
        for h in heads:
            b = bs[h]
            ebl = jnp.exp(b[HG_CHUNK - 1:HG_CHUNK, :])
            q_in = (zq_ref[rows, hcols[h]].astype(F32) * jnp.exp2(b * LOG2E)).astype(BF16)
            kin = ks[h] * jnp.exp2(b * (-LOG2E))
            k_in = kin.astype(BF16)
            k_out = (kin * ebl).astype(BF16)
            v = zi_ref[rows, hcols[h]]
            attns.append(lax.dot_general(q_in, k_in, _NT, preferred_element_type=F32))
            upds.append(lax.dot_general(v, k_out, _TN, preferred_element_type=F32))
            inters.append(lax.dot_general(q_in, stb_ref[h], _NT, preferred_element_type=F32))
            ebls.append(ebl)

        outs = []
        for h in heads:
            attn = jnp.where(causal, attns[h], 0.0).astype(BF16)
            outs.append(_dot(attn, zi_ref[rows, hcols[h]]) + inters[h])
            st = st_ref[h] * ebls[h] + upds[h]
            st_ref[h] = st
            stb_ref[h] = st.astype(BF16)

        for h in heads:
            o = outs[h]
            hg = 0.5 * zg_ref[rows, hcols[h]].astype(F32)
            ms = jnp.mean(o * o, axis=-1, keepdims=True)
            gate = gain_ref[:, hcols[h]] * (hg + hg * jnp.tanh(hg))
            o_ref[rows, hcols[h]] = (o * lax.rsqrt(ms + RMS_EPS) * gate).astype(BF16)
        return carry

    lax.fori_loop(0, HG_TB // HG_CHUNK, chunk, 0, unroll=2)


def _hgrn(z, lb_logits, gain, batch, seq):
    nt = seq // HG_TB
    tril = np.tril(np.ones((HG_CHUNK, HG_CHUNK), np.float32))
    tril = jnp.asarray(np.concatenate([tril, tril], axis=1), BF16)

    def zspec(col):
        return pl.BlockSpec((HG_TB, D_MODEL), lambda b, t, col=col: (b * nt + t, col // D_MODEL))

    return pl.pallas_call(
        _hgrn_kernel,
        out_shape=jax.ShapeDtypeStruct((batch * seq, D_MODEL), BF16),
        grid=(batch, nt),
        in_specs=[_resident(lb_logits.shape), _resident(gain.shape), _resident(tril.shape),
                  zspec(Z_HQ), zspec(Z_HF), zspec(Z_HI), zspec(Z_HG)],
        out_specs=pl.BlockSpec((HG_TB, D_MODEL), lambda b, t: (b * nt + t, 0)),
        scratch_shapes=[pltpu.VMEM((HG_HEADS, HG_DV, HG_DK), F32),
                        pltpu.VMEM((HG_HEADS, HG_DV, HG_DK), BF16)],
        compiler_params=pltpu.CompilerParams(
            dimension_semantics=("arbitrary", "arbitrary"), vmem_limit_bytes=VMEM_LIMIT_BYTES),
        name="hgrn",
    )(lb_logits, gain, tril, z, z, z, z)


def _swa_bucket_table():
    qi = np.arange(SWA_BLOCK)[:, None]
    kj = np.arange(SWA_BLOCK)[None, :]
    n = np.where(kj <= qi, qi - kj, qi + SWA_BLOCK - kj).astype(np.int32)
    max_exact = NUM_BUCKETS // 2
    nf = np.maximum(n, 1).astype(np.float32)
    large = max_exact + (np.log(nf / np.float32(max_exact)) / np.float32(math.log(MAX_DISTANCE / max_exact))
                         * np.float32(NUM_BUCKETS - max_exact)).astype(np.int32)
    large = np.minimum(large, NUM_BUCKETS - 1)
    return np.where(n < max_exact, n, large).astype(np.int32)


SWA_TQ = 512
SWA_NQ = SWA_TQ // SWA_BLOCK
SWA_MASKED = -1e30


def _swa_kernel(relb_ref, sink_ref, bucket_ref, q_ref, kvc_ref, kvp_ref, o_ref, bias_ref, kz_ref, vz_ref):
    t = pl.program_id(1)
    qi = lax.broadcasted_iota(jnp.int32, (SWA_BLOCK, SWA_BLOCK), 0)
    kj = lax.broadcasted_iota(jnp.int32, (SWA_BLOCK, SWA_BLOCK), 1)
    tri = kj <= qi

    @pl.when((pl.program_id(0) == 0) & (t == 0))
    def _():
        bucket = bucket_ref[...]
        for h in range(SWA_HEADS):
            acc = jnp.zeros((SWA_BLOCK, SWA_BLOCK), F32)
            for bk in range(NUM_BUCKETS):
                acc = jnp.where(bucket == bk, relb_ref[bk, h], acc)
            bias_ref[h] = acc
            bias_ref[SWA_HEADS + h] = jnp.where(tri, acc, SWA_MASKED)
        vz_ref[:, :, 128:256] = jnp.ones((2 * SWA_KV_HEADS, SWA_TQ + SWA_BLOCK, 128), BF16)

    lane = lax.broadcasted_iota(jnp.int32, (SWA_BLOCK, 2 * SWA_HEAD_DIM), 1)
    low = lane < SWA_HEAD_DIM
    scale = SWA_HEAD_DIM ** -0.5
    for blk in range(SWA_NQ + 1):
        rows = slice(blk * SWA_BLOCK, (blk + 1) * SWA_BLOCK)
        if blk == 0:
            kk = kvp_ref[:, 0:128].astype(F32) * scale
            vv = kvp_ref[:, 128:256].astype(F32)
        else:
            src = slice((blk - 1) * SWA_BLOCK, blk * SWA_BLOCK)
            kk = kvc_ref[src, 0:128].astype(F32) * scale
            vv = kvc_ref[src, 128:256].astype(F32)
        kk_sw = pltpu.roll(kk, SWA_HEAD_DIM, 1)
        vv_sw = pltpu.roll(vv, SWA_HEAD_DIM, 1)
        for g in range(SWA_KV_HEADS):
            k_lo, k_hi = (kk, kk_sw) if g == 0 else (kk_sw, kk)
            v_lo, v_hi = (vv, vv_sw) if g == 0 else (vv_sw, vv)
            kz_ref[2 * g + 0, rows, :] = jnp.where(low, k_lo, 0.0).astype(BF16)
            kz_ref[2 * g + 1, rows, :] = jnp.where(low, 0.0, k_hi).astype(BF16)
            vz_ref[2 * g + 0, rows, 0:128] = jnp.where(low, v_lo, 0.0).astype(BF16)
            vz_ref[2 * g + 1, rows, 0:128] = jnp.where(low, 0.0, v_hi).astype(BF16)

    def sub_block(i, carry):
        r0 = pl.multiple_of(i * SWA_BLOCK, SWA_BLOCK)
        qrows = pl.ds(r0, SWA_BLOCK)
        krows = pl.ds(r0, 2 * SWA_BLOCK)
        bias_off = jnp.where((t == 0) & (i == 0), SWA_HEADS, 0)
        for pair in range(SWA_HEADS // 2):
            g = (2 * pair) // SWA_GROUP
            qp = q_ref[qrows, pair * 128:(pair + 1) * 128]
            o_pair = None
            for half in range(2):
                h = 2 * pair + half
                slot = 2 * g + half
                s2 = lax.dot_general(qp, kz_ref[slot, krows, :], _NT, preferred_element_type=F32)
                s = jnp.where(tri, s2[:, SWA_BLOCK:], s2[:, :SWA_BLOCK]) + bias_ref[bias_off + h]
                sink = sink_ref[h]
                m = jnp.maximum(jnp.max(s, axis=-1, keepdims=True), sink)
                p = jnp.exp(s - m)
                p2 = jnp.concatenate([jnp.where(tri, 0.0, p), jnp.where(tri, p, 0.0)], axis=1).astype(BF16)
                od = _dot(p2, vz_ref[slot, krows, :])
                den = od[:, 128:] + jnp.exp(sink - m)
                o_h = od[:, :128] * (1.0 / den)
                o_pair = o_h if o_pair is None else o_pair + o_h
            o_ref[qrows, pair * 128:(pair + 1) * 128] = o_pair.astype(BF16)
        return carry

    lax.fori_loop(0, SWA_NQ, sub_block, 0, unroll=2)


def _swa(z, rel_bias, sinks, batch, seq):
    nt = seq // SWA_TQ
    bucket = jnp.asarray(_swa_bucket_table())
    kv_col = Z_KV // 256
    smem = pl.BlockSpec(memory_space=pltpu.SMEM)
    return pl.pallas_call(
        _swa_kernel,
        out_shape=jax.ShapeDtypeStruct((batch * seq, D_MODEL), BF16),
        grid=(batch, nt),
        in_specs=[smem, smem, _resident(bucket.shape),
                  pl.BlockSpec((SWA_TQ, D_MODEL), lambda b, t: (b * nt + t, Z_SQ // D_MODEL)),
                  pl.BlockSpec((SWA_TQ, 256), lambda b, t: (b * nt + t, kv_col)),
                  pl.BlockSpec((SWA_BLOCK, 256),
                               lambda b, t: ((b * nt + t) * SWA_NQ - jnp.minimum(t, 1), kv_col))],
        out_specs=pl.BlockSpec((SWA_TQ, D_MODEL), lambda b, t: (b * nt + t, 0)),
        scratch_shapes=[pltpu.VMEM((2 * SWA_HEADS, SWA_BLOCK, SWA_BLOCK), F32),
                        pltpu.VMEM((2 * SWA_KV_HEADS, SWA_TQ + SWA_BLOCK, 128), BF16),
                        pltpu.VMEM((2 * SWA_KV_HEADS, SWA_TQ + SWA_BLOCK, 256), BF16)],
        compiler_params=pltpu.CompilerParams(
            dimension_semantics=("arbitrary", "arbitrary"), vmem_limit_bytes=VMEM_LIMIT_BYTES),
        name="swa",
    )(rel_bias, sinks, bucket, z, z, z)


def _mem_kv_kernel(mem_ref, w_ref, kv_ref):
    mem_w = MEM_HEADS * MEM_HEAD_DIM
    mb = mem_ref[...].astype(BF16)
    kv_ref[:, :mem_w] = (_dot(mb, w_ref[:, :mem_w]) * MEM_HEAD_DIM ** -0.5).astype(BF16)
    kv_ref[:, mem_w:] = _dot(mb, w_ref[:, mem_w:]).astype(BF16)


def _mem_kv(mem2d, w_b):
    return pl.pallas_call(
        _mem_kv_kernel,
        out_shape=jax.ShapeDtypeStruct((mem2d.shape[0], w_b.shape[1]), BF16),
        compiler_params=pltpu.CompilerParams(vmem_limit_bytes=VMEM_LIMIT_BYTES),
        name="mem_kv",
    )(mem2d, w_b)


MERGE_TM = 512
MERGE_SUB = 256


def _layer_norm(y, g, b):
    mu = jnp.mean(y, axis=-1, keepdims=True)
    yc = y - mu
    var = jnp.mean(yc * yc, axis=-1, keepdims=True)
    return yc * lax.rsqrt(var + LN_EPS) * g + b


def _merge_kernel(alpha, x_ref, oa_ref, ob_ref, mq_ref, gl_ref, kv_ref, wa_ref, wb_ref, wc_ref, wo_ref,
                  g_ref, b_ref, h_ref):
    subs = [slice(r, r + MERGE_SUB) for r in range(0, MERGE_TM, MERGE_SUB)]
    mem_w = MEM_HEADS * MEM_HEAD_DIM
    hcols = [slice(h * MEM_HEAD_DIM, (h + 1) * MEM_HEAD_DIM) for h in range(MEM_HEADS)]

    def gate(rows, i):
        return jax.nn.sigmoid(gl_ref[rows, i * D_MODEL:(i + 1) * D_MODEL].astype(F32))

    scores = [[lax.dot_general(mq_ref[rows, c], kv_ref[:, c], _NT, preferred_element_type=F32) for c in hcols]
              for rows in subs]
    merged = [gate(rows, 0) * _dot(oa_ref[rows, :], wa_ref[...]) for rows in subs]
    probs = []
    for s_heads in scores:
        ps = []
        for s in s_heads:
            p = jnp.exp(s - jnp.max(s, axis=-1, keepdims=True))
            ps.append((p * (1.0 / jnp.sum(p, axis=-1, keepdims=True))).astype(BF16))
        probs.append(ps)
    o_c = [jnp.concatenate([_dot(p, kv_ref[:, mem_w + h * MEM_HEAD_DIM:mem_w + (h + 1) * MEM_HEAD_DIM]).astype(BF16)
                            for h, p in enumerate(ps)], axis=1) for ps in probs]
    merged = [m + gate(rows, 1) * _dot(ob_ref[rows, :], wb_ref[...]) for m, rows in zip(merged, subs)]
    merged = [m + gate(rows, 2) * _dot(oc, wc_ref[...]) for m, rows, oc in zip(merged, subs, o_c)]
    for m, rows in zip(merged, subs):
        mix = _dot(m.astype(BF16), wo_ref[...])
        h_ref[rows, :] = _layer_norm(alpha * x_ref[rows, :] + mix, g_ref[...], b_ref[...])


def _merge(alpha, x2d, o_a, o_b, z, kv, wa, wb, wc, wo, ln_g, ln_b, batch, seq):
    t = x2d.shape[0]
    per_batch = seq // MERGE_TM
    row = lambda i: (i, 0)
    return pl.pallas_call(
        functools.partial(_merge_kernel, alpha),
        out_shape=jax.ShapeDtypeStruct((t, D_MODEL), F32),
        grid=(t // MERGE_TM,),
        in_specs=[pl.BlockSpec((MERGE_TM, D_MODEL), row),
                  pl.BlockSpec((MERGE_TM, D_MODEL), row),
                  pl.BlockSpec((MERGE_TM, D_MODEL), row),
                  pl.BlockSpec((MERGE_TM, D_MODEL), lambda i: (i, Z_MQ // D_MODEL)),
                  pl.BlockSpec((MERGE_TM, 3 * D_MODEL), lambda i: (i, Z_GL // (3 * D_MODEL))),
                  pl.BlockSpec((kv.shape[0] // batch, kv.shape[1]), lambda i: (i // per_batch, 0)),
                  _resident(wa.shape), _resident(wb.shape), _resident(wc.shape), _resident(wo.shape),
                  _resident(ln_g.shape), _resident(ln_b.shape)],
        out_specs=pl.BlockSpec((MERGE_TM, D_MODEL), row),
        compiler_params=pltpu.CompilerParams(
            dimension_semantics=("arbitrary",), vmem_limit_bytes=VMEM_LIMIT_BYTES),
        name="merge",
    )(x2d, o_a, o_b, z, z, kv, wa, wb, wc, wo, ln_g, ln_b)


FFN_TM = 1024
FFN_SUB = 512
FFN_TF = 1024


def _ffn_kernel(alpha, h_ref, wu_ref, wd_ref, g_ref, b_ref, o_ref):
    subs = [slice(r, r + FFN_SUB) for r in range(0, FFN_TM, FFN_SUB)]
    hb = [h_ref[rows, :].astype(BF16) for rows in subs]
    ff = [None] * len(subs)
    for f0 in range(0, D_FF, FFN_TF):
        us = [jnp.maximum(_dot(hb[i], wu_ref[:, f0:f0 + FFN_TF]), 0.0) for i in range(len(subs))]
        for i, u in enumerate(us):
            d = _dot((u * u).astype(BF16), wd_ref[f0:f0 + FFN_TF, :])
            ff[i] = d if ff[i] is None else ff[i] + d
    for i, rows in enumerate(subs):
        o_ref[rows, :] = _layer_norm(alpha * h_ref[rows, :] + ff[i], g_ref[...], b_ref[...])


def _ffn(alpha, h1, wu, wd, ln_g, ln_b):
    t = h1.shape[0]
    row = lambda i: (i, 0)
    return pl.pallas_call(
        functools.partial(_ffn_kernel, alpha),
        out_shape=jax.ShapeDtypeStruct((t, D_MODEL), F32),
        grid=(t // FFN_TM,),
        in_specs=[pl.BlockSpec((FFN_TM, D_MODEL), row),
                  _resident(wu.shape), _resident(wd.shape), _resident(ln_g.shape), _resident(ln_b.shape)],
        out_specs=pl.BlockSpec((FFN_TM, D_MODEL), row),
        compiler_params=pltpu.CompilerParams(
            dimension_semantics=("arbitrary",), vmem_limit_bytes=VMEM_LIMIT_BYTES),
        name="ffn",
    )(h1, wu, wd, ln_g, ln_b)


def kernel(x, mem, w_in, lb_logits, hg_norm_gain, swa_sinks, rel_bias, w_mem_kv, w_branch_hg, w_branch_swa,
           w_branch_mem, w_out, ln1_g, ln1_b, w_up, w_down, ln2_g, ln2_b):
    batch, seq, _ = x.shape
    depth = w_in.shape[0]
    assert depth == 1 and lb_logits.shape[0] == depth + 1
    alpha = (2.0 * depth) ** 0.25
    out_dtype = x.dtype

    w_in_b = w_in[0].astype(BF16)
    x2d = x.reshape(batch * seq, D_MODEL).astype(F32)
    z = _proj_in(x2d, w_in_b)
    o_a = _hgrn(z, lb_logits.astype(F32), hg_norm_gain.astype(F32), batch, seq)
    o_b = _swa(z, rel_bias.astype(F32), swa_sinks[0].astype(F32), batch, seq)
    kv = _mem_kv(mem.reshape(batch * mem.shape[1], D_MODEL).astype(F32), w_mem_kv[0].astype(BF16))
    h1 = _merge(alpha, x2d, o_a, o_b, z, kv,
                w_branch_hg[0].astype(BF16), w_branch_swa[0].astype(BF16), w_branch_mem[0].astype(BF16),
                w_out[0].astype(BF16), ln1_g.astype(F32), ln1_b.astype(F32), batch, seq)
    h2 = _ffn(alpha, h1, w_up[0].astype(BF16), w_down[0].astype(BF16), ln2_g.astype(F32), ln2_b.astype(F32))
    return h2.reshape(batch, seq, D_MODEL).astype(out_dtype)
```

```python
import functools
import math

import numpy as np
import jax
import jax.numpy as jnp
from jax import lax
from jax.experimental import pallas as pl
from jax.experimental.pallas import tpu as pltpu

F32 = jnp.float32
BF16 = jnp.bfloat16

D_MODEL = 1024
HG_HEADS = 8
HG_DK = 128
HG_DV = 128
HG_CHUNK = 64
SWA_HEADS = 16
SWA_KV_HEADS = 2
SWA_GROUP = SWA_HEADS // SWA_KV_HEADS
SWA_HEAD_DIM = 64
SWA_BLOCK = 128
MEM_HEADS = 4
MEM_HEAD_DIM = 256
NUM_BUCKETS = 32
MAX_DISTANCE = 128
D_FF = 4 * D_MODEL
LN_EPS = 1e-5
RMS_EPS = 1e-6
LOG2E = 1.4426950408889634

Z_HQ, Z_HF, Z_HI, Z_HG, Z_SQ, Z_MQ = (i * D_MODEL for i in range(6))
Z_GL = 6 * D_MODEL
Z_KV = 9 * D_MODEL
Z_COLS = Z_KV + 2 * SWA_KV_HEADS * SWA_HEAD_DIM

VMEM_LIMIT_BYTES = 56 * 1024 * 1024

_NT = (((1,), (1,)), ((), ()))
_TN = (((0,), (0,)), ((), ()))


def _dot(a, b):
    return jnp.dot(a, b, preferred_element_type=F32)


def _resident(shape):
    nd = len(shape)
    return pl.BlockSpec(shape, lambda *_: (0,) * nd, pipeline_mode=pl.Buffered(1))


PROJ_TM = 512
PROJ_TN = 1024


def _proj_col_chunks():
    kv_w = 2 * SWA_KV_HEADS * SWA_HEAD_DIM
    src_kv = Z_MQ
    segments = ((0, 0, Z_MQ), (src_kv + kv_w, Z_MQ, Z_KV - Z_MQ), (src_kv, Z_KV, kv_w))
    chunks = []
    for src, dst, width in segments:
        for off in range(0, width, PROJ_TN):
            chunks.append((src + off, dst + off, min(PROJ_TN, width - off)))
    return chunks


def _proj_in_kernel(x_ref, w_ref, z_ref):
    xb = x_ref[...].astype(BF16)
    for src, dst, width in _proj_col_chunks():
        z_ref[:, dst:dst + width] = _dot(xb, w_ref[:, src:src + width]).astype(BF16)


def _proj_in(x2d, w_in_b):
    t = x2d.shape[0]
    return pl.pallas_call(
        _proj_in_kernel,
        out_shape=jax.ShapeDtypeStruct((t, Z_COLS), BF16),
        grid=(t // PROJ_TM,),
        in_specs=[pl.BlockSpec((PROJ_TM, D_MODEL), lambda i: (i, 0)),
                  _resident((D_MODEL, Z_COLS))],
        out_specs=pl.BlockSpec((PROJ_TM, Z_COLS), lambda i: (i, 0)),
        compiler_params=pltpu.CompilerParams(
            dimension_semantics=("arbitrary",), vmem_limit_bytes=VMEM_LIMIT_BYTES),
        name="proj_in",
    )(x2d, w_in_b)


HG_TB = 512


def _hgrn_kernel(lbl_ref, gain_ref, tril_ref, zq_ref, zf_ref, zi_ref, zg_ref, o_ref, st_ref, stb_ref):
    @pl.when(pl.program_id(1) == 0)
    def _():
        st_ref[...] = jnp.zeros_like(st_ref)
        stb_ref[...] = jnp.zeros_like(stb_ref)

    l0 = lbl_ref[0:1, :]
    l1 = lbl_ref[1:2, :]
    lmax = jnp.maximum(l0, l1)
    e0 = jnp.exp(l0 - lmax)
    e1 = jnp.exp(l1 - lmax)
    lb_all = e0 / (e0 + e1)
    f_mid = 0.5 * (1.0 + lb_all)
    f_amp = 0.5 * (1.0 - lb_all)
    gain_s = gain_ref[...] * (HG_DV ** 0.5)
    tril2 = tril_ref[...]
    causal = tril2[:, :HG_CHUNK] > 0
    heads = range(HG_HEADS)
    hcols = [slice(h * HG_DK, (h + 1) * HG_DK) for h in heads]

    def chunk(c, carry):
        r0 = pl.multiple_of(c * HG_CHUNK, HG_CHUNK)
        rows = pl.ds(r0, HG_CHUNK)

        ks, bs = [], []
        for h in heads:
            f = f_mid[:, hcols[h]] + f_amp[:, hcols[h]] * jnp.tanh(0.5 * zf_ref[rows, hcols[h]].astype(F32))
            logf = jnp.log(f)
            ks.append(1.0 - f)
            hi = logf.astype(BF16)
            lo = (logf - hi.astype(F32)).astype(BF16)
            bs.append(_dot(tril2, jnp.concatenate([hi, lo], axis=0)))

        attns, upds, qins, ebls = [], [], [], []
        for h in heads:
            b = bs[h]
            ebl = jnp.exp(b[HG_CHUNK - 1:HG_CHUNK, :])
            eb = jnp.exp2(b * LOG2E)
            q_in = (zq_ref[rows, hcols[h]].astype(F32) * eb).astype(BF16)
            kin = ks[h] * (1.0 / eb)
            k_in = kin.astype(BF16)
            k_out = (kin * ebl).astype(BF16)
            attns.append(lax.dot_general(q_in, k_in, _NT, preferred_element_type=F32))
            upds.append(lax.dot_general(zi_ref[rows, hcols[h]], k_out, _TN, preferred_element_type=F32))
            qins.append(q_in)
            ebls.append(ebl)

        outs = []
        for h in heads:
            attn = jnp.where(causal, attns[h], 0.0).astype(BF16)
            lhs = jnp.concatenate([qins[h], attn], axis=1)
            rhs = jnp.concatenate([stb_ref[h], zi_ref[rows, hcols[h]]], axis=0)
            outs.append(_dot(lhs, rhs))
            st = st_ref[h] * ebls[h] + upds[h]
            st_ref[h] = st
            stb_ref[h] = st.astype(BF16).T

        for h in heads:
            o = outs[h]
            hg = 0.5 * zg_ref[rows, hcols[h]].astype(F32)
            ss = jnp.sum(o * o, axis=-1, keepdims=True)
            gate = gain_s[:, hcols[h]] * (hg + hg * jnp.tanh(hg))
            o_ref[rows, hcols[h]] = (o * lax.rsqrt(ss + HG_DV * RMS_EPS) * gate).astype(BF16)
        return carry

    lax.fori_loop(0, HG_TB // HG_CHUNK, chunk, 0, unroll=4)


def _hgrn(z, lb_logits, gain, batch, seq):
    nt = seq // HG_TB
    tril = np.tril(np.ones((HG_CHUNK, HG_CHUNK), np.float32))
    tril = jnp.asarray(np.concatenate([tril, tril], axis=1), BF16)

    def zspec(col):
        return pl.BlockSpec((HG_TB, D_MODEL), lambda b, t, col=col: (b * nt + t, col // D_MODEL))

    return pl.pallas_call(
        _hgrn_kernel,
        out_shape=jax.ShapeDtypeStruct((batch * seq, D_MODEL), BF16),
        grid=(batch, nt),
        in_specs=[_resident(lb_logits.shape), _resident(gain.shape), _resident(tril.shape),
                  zspec(Z_HQ), zspec(Z_HF), zspec(Z_HI), zspec(Z_HG)],
        out_specs=pl.BlockSpec((HG_TB, D_MODEL), lambda b, t: (b * nt + t, 0)),
        scratch_shapes=[pltpu.VMEM((HG_HEADS, HG_DV, HG_DK), F32),
                        pltpu.VMEM((HG_HEADS, HG_DV, HG_DK), BF16)],
        compiler_params=pltpu.CompilerParams(
            dimension_semantics=("arbitrary", "arbitrary"), vmem_limit_bytes=VMEM_LIMIT_BYTES),
        name="hgrn",
    )(lb_logits, gain, tril, z, z, z, z)


def _swa_bucket_table():
    qi = np.arange(SWA_BLOCK)[:, None]
    kj = np.arange(SWA_BLOCK)[None, :]
    n = np.where(kj <= qi, qi - kj, qi + SWA_BLOCK - kj).astype(np.int32)
    max_exact = NUM_BUCKETS // 2
    nf = np.maximum(n, 1).astype(np.float32)
    large = max_exact + (np.log(nf / np.float32(max_exact)) / np.float32(math.log(MAX_DISTANCE / max_exact))
                         * np.float32(NUM_BUCKETS - max_exact)).astype(np.int32)
    large = np.minimum(large, NUM_BUCKETS - 1)
    return np.where(n < max_exact, n, large).astype(np.int32)


SWA_TQ = 512
SWA_NQ = SWA_TQ // SWA_BLOCK
SWA_MASKED = -1e30


def _swa_kernel(relb_ref, sink_ref, bucket_ref, q_ref, kvc_ref, kvp_ref, o_ref, bias_ref, kz_ref, vz_ref):
    t = pl.program_id(1)
    qi = lax.broadcasted_iota(jnp.int32, (SWA_BLOCK, SWA_BLOCK), 0)
    kj = lax.broadcasted_iota(jnp.int32, (SWA_BLOCK, SWA_BLOCK), 1)
    tri = kj <= qi

    @pl.when((pl.program_id(0) == 0) & (t == 0))
    def _():
        bucket = bucket_ref[...]
        for h in range(SWA_HEADS):
            acc = jnp.zeros((SWA_BLOCK, SWA_BLOCK), F32)
            for bk in range(NUM_BUCKETS):
                acc = jnp.where(bucket == bk, relb_ref[bk, h], acc)
            bias_ref[h] = acc
            bias_ref[SWA_HEADS + h] = jnp.where(tri, acc, SWA_MASKED)
        vz_ref[:, :, 128:256] = jnp.ones((2 * SWA_KV_HEADS, SWA_TQ + SWA_BLOCK, 128), BF16)

    lane = lax.broadcasted_iota(jnp.int32, (SWA_BLOCK, 2 * SWA_HEAD_DIM), 1)
    low = lane < SWA_HEAD_DIM
    scale = SWA_HEAD_DIM ** -0.5
    for blk in range(SWA_NQ + 1):
        rows = slice(blk * SWA_BLOCK, (blk + 1) * SWA_BLOCK)
        if blk == 0:
            kk = kvp_ref[:, 0:128].astype(F32) * scale
            vv = kvp_ref[:, 128:256].astype(F32)
        else:
            src = slice((blk - 1) * SWA_BLOCK, blk * SWA_BLOCK)
            kk = kvc_ref[src, 0:128].astype(F32) * scale
            vv = kvc_ref[src, 128:256].astype(F32)
        kk_sw = pltpu.roll(kk, SWA_HEAD_DIM, 1)
        vv_sw = pltpu.roll(vv, SWA_HEAD_DIM, 1)
        for g in range(SWA_KV_HEADS):
            k_lo, k_hi = (kk, kk_sw) if g == 0 else (kk_sw, kk)
            v_lo, v_hi = (vv, vv_sw) if g == 0 else (vv_sw, vv)
            kz_ref[2 * g + 0, rows, :] = jnp.where(low, k_lo, 0.0).astype(BF16)
            kz_ref[2 * g + 1, rows, :] = jnp.where(low, 0.0, k_hi).astype(BF16)
            vz_ref[2 * g + 0, rows, 0:128] = jnp.where(low, v_lo, 0.0).astype(BF16)
            vz_ref[2 * g + 1, rows, 0:128] = jnp.where(low, 0.0, v_hi).astype(BF16)

    def sub_block(i, carry):
        r0 = pl.multiple_of(i * SWA_BLOCK, SWA_BLOCK)
        qrows = pl.ds(r0, SWA_BLOCK)
        krows = pl.ds(r0, 2 * SWA_BLOCK)
        bias_off = jnp.where((t == 0) & (i == 0), SWA_HEADS, 0)
        for pair in range(SWA_HEADS // 2):
            g = (2 * pair) // SWA_GROUP
            qp = q_ref[qrows, pair * 128:(pair + 1) * 128]
            o_pair = None
            for half in range(2):
                h = 2 * pair + half
                slot = 2 * g + half
                s2 = lax.dot_general(qp, kz_ref[slot, krows, :], _NT, preferred_element_type=F32)
                s = jnp.where(tri, s2[:, SWA_BLOCK:], s2[:, :SWA_BLOCK]) + bias_ref[bias_off + h]
                sink = sink_ref[h]
                m = jnp.maximum(jnp.max(s, axis=-1, keepdims=True), sink)
                p = jnp.exp(s - m)
                p2 = jnp.concatenate([jnp.where(tri, 0.0, p), jnp.where(tri, p, 0.0)], axis=1).astype(BF16)
                od = _dot(p2, vz_ref[slot, krows, :])
                den = od[:, 128:] + jnp.exp(sink - m)
                o_h = od[:, :128] * (1.0 / den)
                o_pair = o_h if o_pair is None else o_pair + o_h
            o_ref[qrows, pair * 128:(pair + 1) * 128] = o_pair.astype(BF16)
        return carry

    lax.fori_loop(0, SWA_NQ, sub_block, 0, unroll=2)


def _swa(z, rel_bias, sinks, batch, seq):
    nt = seq // SWA_TQ
    bucket = jnp.asarray(_swa_bucket_table())
    kv_col = Z_KV // 256
    smem = pl.BlockSpec(memory_space=pltpu.SMEM)
    return pl.pallas_call(
        _swa_kernel,
        out_shape=jax.ShapeDtypeStruct((batch * seq, D_MODEL), BF16),
        grid=(batch, nt),
        in_specs=[smem, smem, _resident(bucket.shape),
                  pl.BlockSpec((SWA_TQ, D_MODEL), lambda b, t: (b * nt + t, Z_SQ // D_MODEL)),
                  pl.BlockSpec((SWA_TQ, 256), lambda b, t: (b * nt + t, kv_col)),
                  pl.BlockSpec((SWA_BLOCK, 256),
                               lambda b, t: ((b * nt + t) * SWA_NQ - jnp.minimum(t, 1), kv_col))],
        out_specs=pl.BlockSpec((SWA_TQ, D_MODEL), lambda b, t: (b * nt + t, 0)),
        scratch_shapes=[pltpu.VMEM((2 * SWA_HEADS, SWA_BLOCK, SWA_BLOCK), F32),
                        pltpu.VMEM((2 * SWA_KV_HEADS, SWA_TQ + SWA_BLOCK, 128), BF16),
                        pltpu.VMEM((2 * SWA_KV_HEADS, SWA_TQ + SWA_BLOCK, 256), BF16)],
        compiler_params=pltpu.CompilerParams(
            dimension_semantics=("arbitrary", "arbitrary"), vmem_limit_bytes=VMEM_LIMIT_BYTES),
        name="swa",
    )(rel_bias, sinks, bucket, z, z, z)


def _mem_kv_kernel(mem_ref, w_ref, kv_ref):
    mem_w = MEM_HEADS * MEM_HEAD_DIM
    mb = mem_ref[...].astype(BF16)
    kv_ref[:, :mem_w] = (_dot(mb, w_ref[:, :mem_w]) * MEM_HEAD_DIM ** -0.5).astype(BF16)
    kv_ref[:, mem_w:] = _dot(mb, w_ref[:, mem_w:]).astype(BF16)


def _mem_kv(mem2d, w_b):
    return pl.pallas_call(
        _mem_kv_kernel,
        out_shape=jax.ShapeDtypeStruct((mem2d.shape[0], w_b.shape[1]), BF16),
        compiler_params=pltpu.CompilerParams(vmem_limit_bytes=VMEM_LIMIT_BYTES),
        name="mem_kv",
    )(mem2d, w_b)


MERGE_TM = 512
MERGE_SUB = 256


def _layer_norm(y, g, b):
    mu = jnp.mean(y, axis=-1, keepdims=True)
    yc = y - mu
    var = jnp.mean(yc * yc, axis=-1, keepdims=True)
    return yc * lax.rsqrt(var + LN_EPS) * g + b


def _merge_kernel(alpha, x_ref, oa_ref, ob_ref, mq_ref, gl_ref, kv_ref, wa_ref, wb_ref, wc_ref, wo_ref,
                  g_ref, b_ref, h_ref):
    subs = [slice(r, r + MERGE_SUB) for r in range(0, MERGE_TM, MERGE_SUB)]
    mem_w = MEM_HEADS * MEM_HEAD_DIM
    hcols = [slice(h * MEM_HEAD_DIM, (h + 1) * MEM_HEAD_DIM) for h in range(MEM_HEADS)]

    def gate(rows, i):
        return jax.nn.sigmoid(gl_ref[rows, i * D_MODEL:(i + 1) * D_MODEL].astype(F32))

    scores = [[lax.dot_general(mq_ref[rows, c], kv_ref[:, c], _NT, preferred_element_type=F32) for c in hcols]
              for rows in subs]
    merged = [gate(rows, 0) * _dot(oa_ref[rows, :], wa_ref[...]) for rows in subs]
    probs = []
    for s_heads in scores:
        ps = []
        for s in s_heads:
            p = jnp.exp(s - jnp.max(s, axis=-1, keepdims=True))
            ps.append((p * (1.0 / jnp.sum(p, axis=-1, keepdims=True))).astype(BF16))
        probs.append(ps)
    o_c = [jnp.concatenate([_dot(p, kv_ref[:, mem_w + h * MEM_HEAD_DIM:mem_w + (h + 1) * MEM_HEAD_DIM]).astype(BF16)
                            for h, p in enumerate(ps)], axis=1) for ps in probs]
    merged = [m + gate(rows, 1) * _dot(ob_ref[rows, :], wb_ref[...]) for m, rows in zip(merged, subs)]
    merged = [m + gate(rows, 2) * _dot(oc, wc_ref[...]) for m, rows, oc in zip(merged, subs, o_c)]
    for m, rows in zip(merged, subs):
        mix = _dot(m.astype(BF16), wo_ref[...])
        h_ref[rows, :] = _layer_norm(alpha * x_ref[rows, :] + mix, g_ref[...], b_ref[...])


def _merge(alpha, x2d, o_a, o_b, z, kv, wa, wb, wc, wo, ln_g, ln_b, batch, seq):
    t = x2d.shape[0]
    per_batch = seq // MERGE_TM
    row = lambda i: (i, 0)
    return pl.pallas_call(
        functools.partial(_merge_kernel, alpha),
        out_shape=jax.ShapeDtypeStruct((t, D_MODEL), F32),
        grid=(t // MERGE_TM,),
        in_specs=[pl.BlockSpec((MERGE_TM, D_MODEL), row),
                  pl.BlockSpec((MERGE_TM, D_MODEL), row),
                  pl.BlockSpec((MERGE_TM, D_MODEL), row),
                  pl.BlockSpec((MERGE_TM, D_MODEL), lambda i: (i, Z_MQ // D_MODEL)),
                  pl.BlockSpec((MERGE_TM, 3 * D_MODEL), lambda i: (i, Z_GL // (3 * D_MODEL))),
                  pl.BlockSpec((kv.shape[0] // batch, kv.shape[1]), lambda i: (i // per_batch, 0)),
                  _resident(wa.shape), _resident(wb.shape), _resident(wc.shape), _resident(wo.shape),
                  _resident(ln_g.shape), _resident(ln_b.shape)],
        out_specs=pl.BlockSpec((MERGE_TM, D_MODEL), row),
        compiler_params=pltpu.CompilerParams(
            dimension_semantics=("arbitrary",), vmem_limit_bytes=VMEM_LIMIT_BYTES),
        name="merge",
    )(x2d, o_a, o_b, z, z, kv, wa, wb, wc, wo, ln_g, ln_b)


FFN_TM = 1024
FFN_SUB = 512
FFN_TF = 1024


def _ffn_kernel(alpha, h_ref, wu_ref, wd_ref, g_ref, b_ref, o_ref):
    subs = [slice(r, r + FFN_SUB) for r in range(0, FFN_TM, FFN_SUB)]
    hb = [h_ref[rows, :].astype(BF16) for rows in subs]
    ff = [None] * len(subs)
    for f0 in range(0, D_FF, FFN_TF):
        us = [jnp.maximum(_dot(hb[i], wu_ref[:, f0:f0 + FFN_TF]), 0.0) for i in range(len(subs))]
        for i, u in enumerate(us):
            d = _dot((u * u).astype(BF16), wd_ref[f0:f0 + FFN_TF, :])
            ff[i] = d if ff[i] is None else ff[i] + d
    for i, rows in enumerate(subs):
        o_ref[rows, :] = _layer_norm(alpha * h_ref[rows, :] + ff[i], g_ref[...], b_ref[...])


def _ffn(alpha, h1, wu, wd, ln_g, ln_b):
    t = h1.shape[0]
    row = lambda i: (i, 0)
    return pl.pallas_call(
        functools.partial(_ffn_kernel, alpha),
        out_shape=jax.ShapeDtypeStruct((t, D_MODEL), F32),
        grid=(t // FFN_TM,),
        in_specs=[pl.BlockSpec((FFN_TM, D_MODEL), row),
                  _resident(wu.shape), _resident(wd.shape), _resident(ln_g.shape), _resident(ln_b.shape)],
        out_specs=pl.BlockSpec((FFN_TM, D_MODEL), row),
        compiler_params=pltpu.CompilerParams(
            dimension_semantics=("arbitrary",), vmem_limit_bytes=VMEM_LIMIT_BYTES),
        name="ffn",
    )(h1, wu, wd, ln_g, ln_b)


def kernel(x, mem, w_in, lb_logits, hg_norm_gain, swa_sinks, rel_bias, w_mem_kv, w_branch_hg, w_branch_swa,
           w_branch_mem, w_out, ln1_g, ln1_b, w_up, w_down, ln2_g, ln2_b):
    batch, seq, _ = x.shape
    depth = w_in.shape[0]
    assert depth == 1 and lb_logits.shape[0] == depth + 1
    alpha = (2.0 * depth) ** 0.25
    out_dtype = x.dtype

    w_in_b = w_in[0].astype(BF16)
    x2d = x.reshape(batch * seq, D_MODEL).astype(F32)
    z = _proj_in(x2d, w_in_b)
    o_a = _hgrn(z, lb_logits.astype(F32), hg_norm_gain.astype(F32), batch, seq)
    o_b = _swa(z, rel_bias.astype(F32), swa_sinks[0].astype(F32), batch, seq)
    kv = _mem_kv(mem.reshape(batch * mem.shape[1], D_MODEL).astype(F32), w_mem_kv[0].astype(BF16))
    h1 = _merge(alpha, x2d, o_a, o_b, z, kv,
                w_branch_hg[0].astype(BF16), w_branch_swa[0].astype(BF16), w_branch_mem[0].astype(BF16),
                w_out[0].astype(BF16), ln1_g.astype(F32), ln1_b.astype(F32), batch, seq)
    h2 = _ffn(alpha, h1, w_up[0].astype(BF16), w_down[0].astype(BF16), ln2_g.astype(F32), ln2_b.astype(F32))
    return h2.reshape(batch, seq, D_MODEL).astype(out_dtype)
```

```python
import functools
import math

import numpy as np
import jax
import jax.numpy as jnp
from jax import lax
from jax.experimental import pallas as pl
from jax.experimental.pallas import tpu as pltpu

F32 = jnp.float32
BF16 = jnp.bfloat16

D_MODEL = 1024
HG_HEADS = 8
HG_DK = 128
HG_DV = 128
HG_CHUNK = 64
SWA_HEADS = 16
SWA_KV_HEADS = 2
SWA_GROUP = SWA_HEADS // SWA_KV_HEADS
SWA_HEAD_DIM = 64
SWA_BLOCK = 128
MEM_HEADS = 4
MEM_HEAD_DIM = 256
NUM_BUCKETS = 32
MAX_DISTANCE = 128
D_FF = 4 * D_MODEL
LN_EPS = 1e-5
RMS_EPS = 1e-6
LOG2E = 1.4426950408889634

Z_HQ, Z_HF, Z_HI, Z_HG, Z_SQ, Z_MQ = (i * D_MODEL for i in range(6))
Z_GL = 6 * D_MODEL
Z_KV = 9 * D_MODEL
Z_COLS = Z_KV + 2 * SWA_KV_HEADS * SWA_HEAD_DIM

VMEM_LIMIT_BYTES = 56 * 1024 * 1024

_NT = (((1,), (1,)), ((), ()))
_TN = (((0,), (0,)), ((), ()))


def _dot(a, b):
    return jnp.dot(a, b, preferred_element_type=F32)


def _resident(shape):
    nd = len(shape)
    return pl.BlockSpec(shape, lambda *_: (0,) * nd, pipeline_mode=pl.Buffered(1))


PROJ_TM = 512
PROJ_TN = 1024


def _proj_col_chunks():
    kv_w = 2 * SWA_KV_HEADS * SWA_HEAD_DIM
    src_kv = Z_MQ
    segments = ((0, 0, Z_MQ), (src_kv + kv_w, Z_MQ, Z_KV - Z_MQ), (src_kv, Z_KV, kv_w))
    chunks = []
    for src, dst, width in segments:
        for off in range(0, width, PROJ_TN):
            chunks.append((src + off, dst + off, min(PROJ_TN, width - off)))
    return chunks


def _proj_in_kernel(x_ref, w_ref, z_ref):
    xb = x_ref[...].astype(BF16)
    for src, dst, width in _proj_col_chunks():
        z_ref[:, dst:dst + width] = _dot(xb, w_ref[:, src:src + width]).astype(BF16)


def _proj_in(x2d, w_in_b):
    t = x2d.shape[0]
    return pl.pallas_call(
        _proj_in_kernel,
        out_shape=jax.ShapeDtypeStruct((t, Z_COLS), BF16),
        grid=(t // PROJ_TM,),
        in_specs=[pl.BlockSpec((PROJ_TM, D_MODEL), lambda i: (i, 0)),
                  _resident((D_MODEL, Z_COLS))],
        out_specs=pl.BlockSpec((PROJ_TM, Z_COLS), lambda i: (i, 0)),
        compiler_params=pltpu.CompilerParams(
            dimension_semantics=("arbitrary",), vmem_limit_bytes=VMEM_LIMIT_BYTES),
        name="proj_in",
    )(x2d, w_in_b)


HG_TB = 512


def _hgrn_kernel(lbl_ref, gain_ref, tril_ref, zq_ref, zf_ref, zi_ref, zg_ref, o_ref, st_ref, stb_ref):
    @pl.when(pl.program_id(1) == 0)
    def _():
        st_ref[...] = jnp.zeros_like(st_ref)
        stb_ref[...] = jnp.zeros_like(stb_ref)

    l0 = lbl_ref[0:1, :]
    l1 = lbl_ref[1:2, :]
    lmax = jnp.maximum(l0, l1)
    e0 = jnp.exp(l0 - lmax)
    e1 = jnp.exp(l1 - lmax)
    lb_all = e0 / (e0 + e1)
    f_mid = 0.5 * (1.0 + lb_all)
    f_amp = 0.5 * (1.0 - lb_all)
    gain_s = gain_ref[...] * (HG_DV ** 0.5)
    tril2 = tril_ref[...]
    causal = tril2[:, :HG_CHUNK] > 0
    heads = range(HG_HEADS)
    hcols = [slice(h * HG_DK, (h + 1) * HG_DK) for h in heads]

    def chunk(c, carry):
        r0 = pl.multiple_of(c * HG_CHUNK, HG_CHUNK)
        rows = pl.ds(r0, HG_CHUNK)

        ks, bs = [], []
        for h in heads:
            f = f_mid[:, hcols[h]] + f_amp[:, hcols[h]] * jnp.tanh(0.5 * zf_ref[rows, hcols[h]].astype(F32))
            logf = jnp.log(f)
            ks.append(1.0 - f)
            hi = logf.astype(BF16)
            lo = (logf - hi.astype(F32)).astype(BF16)
            bs.append(_dot(tril2, jnp.concatenate([hi, lo], axis=0)))

        attns, upds, qins, ebls = [], [], [], []
        for h in heads:
            b = bs[h]
            ebl = jnp.exp(b[HG_CHUNK - 1:HG_CHUNK, :])
            eb = jnp.exp2(b * LOG2E)
            q_in = (zq_ref[rows, hcols[h]].astype(F32) * eb).astype(BF16)
            kin = ks[h] * (1.0 / eb)
            k_in = kin.astype(BF16)
            k_out = (kin * ebl).astype(BF16)
            attns.append(lax.dot_general(q_in, k_in, _NT, preferred_element_type=F32))
            upds.append(lax.dot_general(zi_ref[rows, hcols[h]], k_out, _TN, preferred_element_type=F32))
            qins.append(q_in)
            ebls.append(ebl)

        outs = []
        for h in heads:
            attn = jnp.where(causal, attns[h], 0.0).astype(BF16)
            lhs = jnp.concatenate([qins[h], attn], axis=1)
            rhs = jnp.concatenate([stb_ref[h], zi_ref[rows, hcols[h]]], axis=0)
            outs.append(_dot(lhs, rhs))
            st = st_ref[h] * ebls[h] + upds[h]
            st_ref[h] = st
            stb_ref[h] = st.astype(BF16).T

        for h in heads:
            o = outs[h]
            hg = 0.5 * zg_ref[rows, hcols[h]].astype(F32)
            ss = jnp.sum(o * o, axis=-1, keepdims=True)
            gate = gain_s[:, hcols[h]] * (hg + hg * jnp.tanh(hg))
            o_ref[rows, hcols[h]] = (o * lax.rsqrt(ss + HG_DV * RMS_EPS) * gate).astype(BF16)
        return carry

    lax.fori_loop(0, HG_TB // HG_CHUNK, chunk, 0, unroll=4)


def _hgrn(z, lb_logits, gain, batch, seq):
    nt = seq // HG_TB
    tril = np.tril(np.ones((HG_CHUNK, HG_CHUNK), np.float32))
    tril = jnp.asarray(np.concatenate([tril, tril], axis=1), BF16)

    def zspec(col):
        return pl.BlockSpec((HG_TB, D_MODEL), lambda b, t, col=col: (b * nt + t, col // D_MODEL))

    return pl.pallas_call(
        _hgrn_kernel,
        out_shape=jax.ShapeDtypeStruct((batch * seq, D_MODEL), BF16),
        grid=(batch, nt),
        in_specs=[_resident(lb_logits.shape), _resident(gain.shape), _resident(tril.shape),
                  zspec(Z_HQ), zspec(Z_HF), zspec(Z_HI), zspec(Z_HG)],
        out_specs=pl.BlockSpec((HG_TB, D_MODEL), lambda b, t: (b * nt + t, 0)),
        scratch_shapes=[pltpu.VMEM((HG_HEADS, HG_DV, HG_DK), F32),
                        pltpu.VMEM((HG_HEADS, HG_DV, HG_DK), BF16)],
        compiler_params=pltpu.CompilerParams(
            dimension_semantics=("arbitrary", "arbitrary"), vmem_limit_bytes=VMEM_LIMIT_BYTES),
        name="hgrn",
    )(lb_logits, gain, tril, z, z, z, z)


def _swa_bucket_table():
    qi = np.arange(SWA_BLOCK)[:, None]
    kj = np.arange(SWA_BLOCK)[None, :]
    n = np.where(kj <= qi, qi - kj, qi + SWA_BLOCK - kj).astype(np.int32)
    max_exact = NUM_BUCKETS // 2
    nf = np.maximum(n, 1).astype(np.float32)
    large = max_exact + (np.log(nf / np.float32(max_exact)) / np.float32(math.log(MAX_DISTANCE / max_exact))
                         * np.float32(NUM_BUCKETS - max_exact)).astype(np.int32)
    large = np.minimum(large, NUM_BUCKETS - 1)
    return np.where(n < max_exact, n, large).astype(np.int32)


SWA_TQ = 512
SWA_NQ = SWA_TQ // SWA_BLOCK
SWA_MASKED = -1e30


def _swa_kernel(relb_ref, sink_ref, bucket_ref, q_ref, kvc_ref, kvp_ref, o_ref, bias_ref, kz_ref, vz_ref):
    t = pl.program_id(1)
    qi = lax.broadcasted_iota(jnp.int32, (SWA_BLOCK, SWA_BLOCK), 0)
    kj = lax.broadcasted_iota(jnp.int32, (SWA_BLOCK, SWA_BLOCK), 1)
    tri = kj <= qi

    @pl.when((pl.program_id(0) == 0) & (t == 0))
    def _():
        bucket = bucket_ref[...]
        for h in range(SWA_HEADS):
            acc = jnp.zeros((SWA_BLOCK, SWA_BLOCK), F32)
            for bk in range(NUM_BUCKETS):
                acc = jnp.where(bucket == bk, relb_ref[bk, h], acc)
            bias_ref[h] = acc
            bias_ref[SWA_HEADS + h] = jnp.where(tri, acc, SWA_MASKED)
        vz_ref[:, :, 128:256] = jnp.ones((2 * SWA_KV_HEADS, SWA_TQ + SWA_BLOCK, 128), BF16)

    lane = lax.broadcasted_iota(jnp.int32, (SWA_BLOCK, 2 * SWA_HEAD_DIM), 1)
    low = lane < SWA_HEAD_DIM
    scale = SWA_HEAD_DIM ** -0.5
    for blk in range(SWA_NQ + 1):
        rows = slice(blk * SWA_BLOCK, (blk + 1) * SWA_BLOCK)
        if blk == 0:
            kk = kvp_ref[:, 0:128].astype(F32) * scale
            vv = kvp_ref[:, 128:256].astype(F32)
        else:
            src = slice((blk - 1) * SWA_BLOCK, blk * SWA_BLOCK)
            kk = kvc_ref[src, 0:128].astype(F32) * scale
            vv = kvc_ref[src, 128:256].astype(F32)
        kk_sw = pltpu.roll(kk, SWA_HEAD_DIM, 1)
        vv_sw = pltpu.roll(vv, SWA_HEAD_DIM, 1)
        for g in range(SWA_KV_HEADS):
            k_lo, k_hi = (kk, kk_sw) if g == 0 else (kk_sw, kk)
            v_lo, v_hi = (vv, vv_sw) if g == 0 else (vv_sw, vv)
            kz_ref[2 * g + 0, rows, :] = jnp.where(low, k_lo, 0.0).astype(BF16)
            kz_ref[2 * g + 1, rows, :] = jnp.where(low, 0.0, k_hi).astype(BF16)
            vz_ref[2 * g + 0, rows, 0:128] = jnp.where(low, v_lo, 0.0).astype(BF16)
            vz_ref[2 * g + 1, rows, 0:128] = jnp.where(low, 0.0, v_hi).astype(BF16)

    def sub_block(i, carry):
        r0 = pl.multiple_of(i * SWA_BLOCK, SWA_BLOCK)
        qrows = pl.ds(r0, SWA_BLOCK)
        krows = pl.ds(r0, 2 * SWA_BLOCK)
        bias_off = jnp.where((t == 0) & (i == 0), SWA_HEADS, 0)
        for pair in range(SWA_HEADS // 2):
            g = (2 * pair) // SWA_GROUP
            qp = q_ref[qrows, pair * 128:(pair + 1) * 128]
            o_pair = None
            for half in range(2):
                h = 2 * pair + half
                slot = 2 * g + half
                s2 = lax.dot_general(qp, kz_ref[slot, krows, :], _NT, preferred_element_type=F32)
                s = jnp.where(tri, s2[:, SWA_BLOCK:], s2[:, :SWA_BLOCK]) + bias_ref[bias_off + h]
                sink = sink_ref[h]
                m = jnp.maximum(jnp.max(s, axis=-1, keepdims=True), sink)
                p = jnp.exp(s - m)
                p2 = jnp.concatenate([jnp.where(tri, 0.0, p), jnp.where(tri, p, 0.0)], axis=1).astype(BF16)
                od = _dot(p2, vz_ref[slot, krows, :])
                den = od[:, 128:] + jnp.exp(sink - m)
                o_h = od[:, :128] * (1.0 / den)
                o_pair = o_h if o_pair is None else o_pair + o_h
            o_ref[qrows, pair * 128:(pair + 1) * 128] = o_pair.astype(BF16)
        return carry

    lax.fori_loop(0, SWA_NQ, sub_block, 0, unroll=2)


def _swa(z, rel_bias, sinks, batch, seq):
    nt = seq // SWA_TQ
    bucket = jnp.asarray(_swa_bucket_table())
    kv_col = Z_KV // 256
    smem = pl.BlockSpec(memory_space=pltpu.SMEM)
    return pl.pallas_call(
        _swa_kernel,
        out_shape=jax.ShapeDtypeStruct((batch * seq, D_MODEL), BF16),
        grid=(batch, nt),
        in_specs=[smem, smem, _resident(bucket.shape),
                  pl.BlockSpec((SWA_TQ, D_MODEL), lambda b, t: (b * nt + t, Z_SQ // D_MODEL)),
                  pl.BlockSpec((SWA_TQ, 256), lambda b, t: (b * nt + t, kv_col)),
                  pl.BlockSpec((SWA_BLOCK, 256),
                               lambda b, t: ((b * nt + t) * SWA_NQ - jnp.minimum(t, 1), kv_col))],
        out_specs=pl.BlockSpec((SWA_TQ, D_MODEL), lambda b, t: (b * nt + t, 0)),
        scratch_shapes=[pltpu.VMEM((2 * SWA_HEADS, SWA_BLOCK, SWA_BLOCK), F32),
                        pltpu.VMEM((2 * SWA_KV_HEADS, SWA_TQ + SWA_BLOCK, 128), BF16),
                        pltpu.VMEM((2 * SWA_KV_HEADS, SWA_TQ + SWA_BLOCK, 256), BF16)],
        compiler_params=pltpu.CompilerParams(
            dimension_semantics=("arbitrary", "arbitrary"), vmem_limit_bytes=VMEM_LIMIT_BYTES),
        name="swa",
    )(rel_bias, sinks, bucket, z, z, z)


def _mem_kv_kernel(mem_ref, w_ref, kv_ref):
    mem_w = MEM_HEADS * MEM_HEAD_DIM
    mb = mem_ref[...].astype(BF16)
    kv_ref[:, :mem_w] = (_dot(mb, w_ref[:, :mem_w]) * MEM_HEAD_DIM ** -0.5).astype(BF16)
    kv_ref[:, mem_w:] = _dot(mb, w_ref[:, mem_w:]).astype(BF16)


def _mem_kv(mem2d, w_b):
    return pl.pallas_call(
        _mem_kv_kernel,
        out_shape=jax.ShapeDtypeStruct((mem2d.shape[0], w_b.shape[1]), BF16),
        compiler_params=pltpu.CompilerParams(vmem_limit_bytes=VMEM_LIMIT_BYTES),
        name="mem_kv",
    )(mem2d, w_b)


TAIL_TM = 512
TAIL_SUB = 256
TAIL_TF = 1024


def _layer_norm(y, g, b):
    mu = jnp.mean(y, axis=-1, keepdims=True)
    yc = y - mu
    var = jnp.mean(yc * yc, axis=-1, keepdims=True)
    return yc * lax.rsqrt(var + LN_EPS) * g + b


def _tail_kernel(alpha, x_ref, oa_ref, ob_ref, mq_ref, gl_ref, kv_ref, wa_ref, wb_ref, wc_ref, wo_ref,
                 g1_ref, b1_ref, wu_ref, wd_ref, g2_ref, b2_ref, o_ref):
    subs = [slice(r, r + TAIL_SUB) for r in range(0, TAIL_TM, TAIL_SUB)]
    mem_w = MEM_HEADS * MEM_HEAD_DIM
    hcols = [slice(h * MEM_HEAD_DIM, (h + 1) * MEM_HEAD_DIM) for h in range(MEM_HEADS)]

    def gate(rows, i):
        return jax.nn.sigmoid(gl_ref[rows, i * D_MODEL:(i + 1) * D_MODEL].astype(F32))

    scores = [[lax.dot_general(mq_ref[rows, c], kv_ref[:, c], _NT, preferred_element_type=F32) for c in hcols]
              for rows in subs]
    merged = [gate(rows, 0) * _dot(oa_ref[rows, :], wa_ref[...]) for rows in subs]
    probs = []
    for s_heads in scores:
        ps = []
        for s in s_heads:
            p = jnp.exp(s - jnp.max(s, axis=-1, keepdims=True))
            ps.append((p * (1.0 / jnp.sum(p, axis=-1, keepdims=True))).astype(BF16))
        probs.append(ps)
    o_c = [jnp.concatenate([_dot(p, kv_ref[:, mem_w + h * MEM_HEAD_DIM:mem_w + (h + 1) * MEM_HEAD_DIM]).astype(BF16)
                            for h, p in enumerate(ps)], axis=1) for ps in probs]
    merged = [m + gate(rows, 1) * _dot(ob_ref[rows, :], wb_ref[...]) for m, rows in zip(merged, subs)]
    merged = [m + gate(rows, 2) * _dot(oc, wc_ref[...]) for m, rows, oc in zip(merged, subs, o_c)]

    h1 = []
    for m, rows in zip(merged, subs):
        mix = _dot(m.astype(BF16), wo_ref[...])
        h1.append(_layer_norm(alpha * x_ref[rows, :] + mix, g1_ref[...], b1_ref[...]))

    hb = [h.astype(BF16) for h in h1]
    ff = [None] * len(subs)
    for f0 in range(0, D_FF, TAIL_TF):
        us = [jnp.maximum(_dot(hb[i], wu_ref[:, f0:f0 + TAIL_TF]), 0.0) for i in range(len(subs))]
        for i, u in enumerate(us):
            d = _dot((u * u).astype(BF16), wd_ref[f0:f0 + TAIL_TF, :])
            ff[i] = d if ff[i] is None else ff[i] + d
    for i, rows in enumerate(subs):
        o_ref[rows, :] = _layer_norm(alpha * h1[i] + ff[i], g2_ref[...], b2_ref[...])


def _tail(alpha, x2d, o_a, o_b, z, kv, wa, wb, wc, wo, ln1_g, ln1_b, wu, wd, ln2_g, ln2_b, batch, seq):
    t = x2d.shape[0]
    per_batch = seq // TAIL_TM
    row = lambda i: (i, 0)
    weights = (wa, wb, wc, wo, ln1_g, ln1_b, wu, wd, ln2_g, ln2_b)
    return pl.pallas_call(
        functools.partial(_tail_kernel, alpha),
        out_shape=jax.ShapeDtypeStruct((t, D_MODEL), F32),
        grid=(t // TAIL_TM,),
        in_specs=[pl.BlockSpec((TAIL_TM, D_MODEL), row),
                  pl.BlockSpec((TAIL_TM, D_MODEL), row),
                  pl.BlockSpec((TAIL_TM, D_MODEL), row),
                  pl.BlockSpec((TAIL_TM, D_MODEL), lambda i: (i, Z_MQ // D_MODEL)),
                  pl.BlockSpec((TAIL_TM, 3 * D_MODEL), lambda i: (i, Z_GL // (3 * D_MODEL))),
                  pl.BlockSpec((kv.shape[0] // batch, kv.shape[1]), lambda i: (i // per_batch, 0))]
                 + [_resident(w.shape) for w in weights],
        out_specs=pl.BlockSpec((TAIL_TM, D_MODEL), row),
        compiler_params=pltpu.CompilerParams(
            dimension_semantics=("arbitrary",), vmem_limit_bytes=VMEM_LIMIT_BYTES),
        name="tail",
    )(x2d, o_a, o_b, z, z, kv, *weights)


def kernel(x, mem, w_in, lb_logits, hg_norm_gain, swa_sinks, rel_bias, w_mem_kv, w_branch_hg, w_branch_swa,
           w_branch_mem, w_out, ln1_g, ln1_b, w_up, w_down, ln2_g, ln2_b):
    batch, seq, _ = x.shape
    depth = w_in.shape[0]
    assert depth == 1 and lb_logits.shape[0] == depth + 1
    alpha = (2.0 * depth) ** 0.25
    out_dtype = x.dtype

    w_in_b = w_in[0].astype(BF16)
    x2d = x.reshape(batch * seq, D_MODEL).astype(F32)
    z = _proj_in(x2d, w_in_b)
    o_a = _hgrn(z, lb_logits.astype(F32), hg_norm_gain.astype(F32), batch, seq)
    o_b = _swa(z, rel_bias.astype(F32), swa_sinks[0].astype(F32), batch, seq)
    kv = _mem_kv(mem.reshape(batch * mem.shape[1], D_MODEL).astype(F32), w_mem_kv[0].astype(BF16))
    h2 = _tail(alpha, x2d, o_a, o_b, z, kv,
               w_branch_hg[0].astype(BF16), w_branch_swa[0].astype(BF16), w_branch_mem[0].astype(BF16),
               w_out[0].astype(BF16), ln1_g.astype(F32), ln1_b.astype(F32),
               w_up[0].astype(BF16), w_down[0].astype(BF16), ln2_g.astype(F32), ln2_b.astype(F32), batch, seq)
    return h2.reshape(batch, seq, D_MODEL).astype(out_dtype)
```

```python
import functools
import math

import numpy as np
import jax
import jax.numpy as jnp
from jax import lax
from jax.experimental import pallas as pl
from jax.experimental.pallas import tpu as pltpu

F32 = jnp.float32
BF16 = jnp.bfloat16

D_MODEL = 1024
HG_HEADS = 8
HG_DK = 128
HG_DV = 128
HG_CHUNK = 64
SWA_HEADS = 16
SWA_KV_HEADS = 2
SWA_GROUP = SWA_HEADS // SWA_KV_HEADS
SWA_HEAD_DIM = 64
SWA_BLOCK = 128
MEM_HEADS = 4
MEM_HEAD_DIM = 256
NUM_BUCKETS = 32
MAX_DISTANCE = 128
D_FF = 4 * D_MODEL
LN_EPS = 1e-5
RMS_EPS = 1e-6
LOG2E = 1.4426950408889634

Z_HQ, Z_HF, Z_HI, Z_HG, Z_SQ, Z_MQ = (i * D_MODEL for i in range(6))
Z_GL = 6 * D_MODEL
Z_KV = 9 * D_MODEL
Z_COLS = Z_KV + 2 * SWA_KV_HEADS * SWA_HEAD_DIM

VMEM_LIMIT_BYTES = 56 * 1024 * 1024

_NT = (((1,), (1,)), ((), ()))
_TN = (((0,), (0,)), ((), ()))


def _dot(a, b):
    return jnp.dot(a, b, preferred_element_type=F32)


def _resident(shape):
    nd = len(shape)
    return pl.BlockSpec(shape, lambda *_: (0,) * nd, pipeline_mode=pl.Buffered(1))


PROJ_TM = 512
PROJ_TN = 1024


def _proj_col_chunks():
    kv_w = 2 * SWA_KV_HEADS * SWA_HEAD_DIM
    src_kv = Z_MQ
    segments = ((0, 0, Z_MQ), (src_kv + kv_w, Z_MQ, Z_KV - Z_MQ), (src_kv, Z_KV, kv_w))
    chunks = []
    for src, dst, width in segments:
        for off in range(0, width, PROJ_TN):
            chunks.append((src + off, dst + off, min(PROJ_TN, width - off)))
    return chunks


def _proj_in_kernel(n_cast, x_ref, w_ref, *refs):
    cast_in, z_ref, cast_out = refs[:n_cast], refs[n_cast], refs[n_cast + 1:]
    xb = x_ref[...].astype(BF16)
    for src, dst, width in _proj_col_chunks():
        z_ref[:, dst:dst + width] = _dot(xb, w_ref[:, src:src + width]).astype(BF16)
    for src_ref, dst_ref in zip(cast_in, cast_out):
        dst_ref[...] = src_ref[...].astype(BF16)


def _proj_in(x2d, w_in_b, later_weights):
    t = x2d.shape[0]
    steps = t // PROJ_TM
    slab = lambda w: pl.BlockSpec((w.shape[0] // steps, w.shape[1]), lambda i: (i, 0))
    outs = pl.pallas_call(
        functools.partial(_proj_in_kernel, len(later_weights)),
        out_shape=[jax.ShapeDtypeStruct((t, Z_COLS), BF16)]
                  + [jax.ShapeDtypeStruct(w.shape, BF16) for w in later_weights],
        grid=(steps,),
        in_specs=[pl.BlockSpec((PROJ_TM, D_MODEL), lambda i: (i, 0)),
                  _resident((D_MODEL, Z_COLS))] + [slab(w) for w in later_weights],
        out_specs=[pl.BlockSpec((PROJ_TM, Z_COLS), lambda i: (i, 0))] + [slab(w) for w in later_weights],
        compiler_params=pltpu.CompilerParams(
            dimension_semantics=("arbitrary",), vmem_limit_bytes=VMEM_LIMIT_BYTES),
        name="proj_in",
    )(x2d, w_in_b, *later_weights)
    return outs[0], outs[1:]


HG_TB = 512


def _hgrn_kernel(lbl_ref, gain_ref, tril_ref, zq_ref, zf_ref, zi_ref, zg_ref, o_ref, st_ref, stb_ref):
    @pl.when(pl.program_id(1) == 0)
    def _():
        st_ref[...] = jnp.zeros_like(st_ref)
        stb_ref[...] = jnp.zeros_like(stb_ref)

    l0 = lbl_ref[0:1, :]
    l1 = lbl_ref[1:2, :]
    lmax = jnp.maximum(l0, l1)
    e0 = jnp.exp(l0 - lmax)
    e1 = jnp.exp(l1 - lmax)
    lb_all = e0 / (e0 + e1)
    f_mid = 0.5 * (1.0 + lb_all)
    f_amp = 0.5 * (1.0 - lb_all)
    gain_s = gain_ref[...] * (HG_DV ** 0.5)
    tril2 = tril_ref[...]
    causal = tril2[:, :HG_CHUNK] > 0
    heads = range(HG_HEADS)
    hcols = [slice(h * HG_DK, (h + 1) * HG_DK) for h in heads]

    def chunk(c, carry):
        r0 = pl.multiple_of(c * HG_CHUNK, HG_CHUNK)
        rows = pl.ds(r0, HG_CHUNK)

        ks, bs = [], []
        for h in heads:
            f = f_mid[:, hcols[h]] + f_amp[:, hcols[h]] * jnp.tanh(0.5 * zf_ref[rows, hcols[h]].astype(F32))
            logf = jnp.log(f)
            ks.append(1.0 - f)
            hi = logf.astype(BF16)
            lo = (logf - hi.astype(F32)).astype(BF16)
            bs.append(_dot(tril2, jnp.concatenate([hi, lo], axis=0)))

        attns, upds, qins, ebls = [], [], [], []
        for h in heads:
            b = bs[h]
            ebl = jnp.exp(b[HG_CHUNK - 1:HG_CHUNK, :])
            eb = jnp.exp2(b * LOG2E)
            q_in = (zq_ref[rows, hcols[h]].astype(F32) * eb).astype(BF16)
            kin = ks[h] * (1.0 / eb)
            k_in = kin.astype(BF16)
            k_out = (kin * ebl).astype(BF16)
            attns.append(lax.dot_general(q_in, k_in, _NT, preferred_element_type=F32))
            upds.append(lax.dot_general(zi_ref[rows, hcols[h]], k_out, _TN, preferred_element_type=F32))
            qins.append(q_in)
            ebls.append(ebl)

        outs = []
        for h in heads:
            attn = jnp.where(causal, attns[h], 0.0).astype(BF16)
            lhs = jnp.concatenate([qins[h], attn], axis=1)
            rhs = jnp.concatenate([stb_ref[h], zi_ref[rows, hcols[h]]], axis=0)
            outs.append(_dot(lhs, rhs))
            st = st_ref[h] * ebls[h] + upds[h]
            st_ref[h] = st
            stb_ref[h] = st.astype(BF16).T

        for h in heads:
            o = outs[h]
            hg = 0.5 * zg_ref[rows, hcols[h]].astype(F32)
            ss = jnp.sum(o * o, axis=-1, keepdims=True)
            gate = gain_s[:, hcols[h]] * (hg + hg * jnp.tanh(hg))
            o_ref[rows, hcols[h]] = (o * lax.rsqrt(ss + HG_DV * RMS_EPS) * gate).astype(BF16)
        return carry

    lax.fori_loop(0, HG_TB // HG_CHUNK, chunk, 0, unroll=4)


def _hgrn(z, lb_logits, gain, batch, seq):
    nt = seq // HG_TB
    tril = np.tril(np.ones((HG_CHUNK, HG_CHUNK), np.float32))
    tril = jnp.asarray(np.concatenate([tril, tril], axis=1), BF16)

    def zspec(col):
        return pl.BlockSpec((HG_TB, D_MODEL), lambda b, t, col=col: (b * nt + t, col // D_MODEL))

    return pl.pallas_call(
        _hgrn_kernel,
        out_shape=jax.ShapeDtypeStruct((batch * seq, D_MODEL), BF16),
        grid=(batch, nt),
        in_specs=[_resident(lb_logits.shape), _resident(gain.shape), _resident(tril.shape),
                  zspec(Z_HQ), zspec(Z_HF), zspec(Z_HI), zspec(Z_HG)],
        out_specs=pl.BlockSpec((HG_TB, D_MODEL), lambda b, t: (b * nt + t, 0)),
        scratch_shapes=[pltpu.VMEM((HG_HEADS, HG_DV, HG_DK), F32),
                        pltpu.VMEM((HG_HEADS, HG_DV, HG_DK), BF16)],
        compiler_params=pltpu.CompilerParams(
            dimension_semantics=("arbitrary", "arbitrary"), vmem_limit_bytes=VMEM_LIMIT_BYTES),
        name="hgrn",
    )(lb_logits, gain, tril, z, z, z, z)


def _swa_bucket_table():
    qi = np.arange(SWA_BLOCK)[:, None]
    kj = np.arange(SWA_BLOCK)[None, :]
    n = np.where(kj <= qi, qi - kj, qi + SWA_BLOCK - kj).astype(np.int32)
    max_exact = NUM_BUCKETS // 2
    nf = np.maximum(n, 1).astype(np.float32)
    large = max_exact + (np.log(nf / np.float32(max_exact)) / np.float32(math.log(MAX_DISTANCE / max_exact))
                         * np.float32(NUM_BUCKETS - max_exact)).astype(np.int32)
    large = np.minimum(large, NUM_BUCKETS - 1)
    return np.where(n < max_exact, n, large).astype(np.int32)


SWA_TQ = 512
SWA_NQ = SWA_TQ // SWA_BLOCK
SWA_MASKED = -1e30


def _swa_kernel(relb_ref, sink_ref, bucket_ref, q_ref, kvc_ref, kvp_ref, o_ref, bias_ref, kz_ref, vz_ref):
    t = pl.program_id(1)
    qi = lax.broadcasted_iota(jnp.int32, (SWA_BLOCK, SWA_BLOCK), 0)
    kj = lax.broadcasted_iota(jnp.int32, (SWA_BLOCK, SWA_BLOCK), 1)
    tri = kj <= qi

    @pl.when((pl.program_id(0) == 0) & (t == 0))
    def _():
        bucket = bucket_ref[...]
        for h in range(SWA_HEADS):
            acc = jnp.zeros((SWA_BLOCK, SWA_BLOCK), F32)
            for bk in range(NUM_BUCKETS):
                acc = jnp.where(bucket == bk, relb_ref[bk, h], acc)
            bias_ref[h] = acc
            bias_ref[SWA_HEADS + h] = jnp.where(tri, acc, SWA_MASKED)
        vz_ref[:, :, 128:256] = jnp.ones((2 * SWA_KV_HEADS, SWA_TQ + SWA_BLOCK, 128), BF16)

    lane = lax.broadcasted_iota(jnp.int32, (SWA_BLOCK, 2 * SWA_HEAD_DIM), 1)
    low = lane < SWA_HEAD_DIM
    scale = SWA_HEAD_DIM ** -0.5
    for blk in range(SWA_NQ + 1):
        rows = slice(blk * SWA_BLOCK, (blk + 1) * SWA_BLOCK)
        if blk == 0:
            kk = kvp_ref[:, 0:128].astype(F32) * scale
            vv = kvp_ref[:, 128:256].astype(F32)
        else:
            src = slice((blk - 1) * SWA_BLOCK, blk * SWA_BLOCK)
            kk = kvc_ref[src, 0:128].astype(F32) * scale
            vv = kvc_ref[src, 128:256].astype(F32)
        kk_sw = pltpu.roll(kk, SWA_HEAD_DIM, 1)
        vv_sw = pltpu.roll(vv, SWA_HEAD_DIM, 1)
        for g in range(SWA_KV_HEADS):
            k_lo, k_hi = (kk, kk_sw) if g == 0 else (kk_sw, kk)
            v_lo, v_hi = (vv, vv_sw) if g == 0 else (vv_sw, vv)
            kz_ref[2 * g + 0, rows, :] = jnp.where(low, k_lo, 0.0).astype(BF16)
            kz_ref[2 * g + 1, rows, :] = jnp.where(low, 0.0, k_hi).astype(BF16)
            vz_ref[2 * g + 0, rows, 0:128] = jnp.where(low, v_lo, 0.0).astype(BF16)
            vz_ref[2 * g + 1, rows, 0:128] = jnp.where(low, 0.0, v_hi).astype(BF16)

    def sub_block(i, carry):
        r0 = pl.multiple_of(i * SWA_BLOCK, SWA_BLOCK)
        qrows = pl.ds(r0, SWA_BLOCK)
        krows = pl.ds(r0, 2 * SWA_BLOCK)
        bias_off = jnp.where((t == 0) & (i == 0), SWA_HEADS, 0)
        for pair in range(SWA_HEADS // 2):
            g = (2 * pair) // SWA_GROUP
            qp = q_ref[qrows, pair * 128:(pair + 1) * 128]
            o_pair = None
            for half in range(2):
                h = 2 * pair + half
                slot = 2 * g + half
                s2 = lax.dot_general(qp, kz_ref[slot, krows, :], _NT, preferred_element_type=F32)
                s = jnp.where(tri, s2[:, SWA_BLOCK:], s2[:, :SWA_BLOCK]) + bias_ref[bias_off + h]
                sink = sink_ref[h]
                m = jnp.maximum(jnp.max(s, axis=-1, keepdims=True), sink)
                p = jnp.exp(s - m)
                p2 = jnp.concatenate([jnp.where(tri, 0.0, p), jnp.where(tri, p, 0.0)], axis=1).astype(BF16)
                od = _dot(p2, vz_ref[slot, krows, :])
                den = od[:, 128:] + jnp.exp(sink - m)
                o_h = od[:, :128] * (1.0 / den)
                o_pair = o_h if o_pair is None else o_pair + o_h
            o_ref[qrows, pair * 128:(pair + 1) * 128] = o_pair.astype(BF16)
        return carry

    lax.fori_loop(0, SWA_NQ, sub_block, 0, unroll=2)


def _swa(z, rel_bias, sinks, batch, seq):
    nt = seq // SWA_TQ
    bucket = jnp.asarray(_swa_bucket_table())
    kv_col = Z_KV // 256
    smem = pl.BlockSpec(memory_space=pltpu.SMEM)
    return pl.pallas_call(
        _swa_kernel,
        out_shape=jax.ShapeDtypeStruct((batch * seq, D_MODEL), BF16),
        grid=(batch, nt),
        in_specs=[smem, smem, _resident(bucket.shape),
                  pl.BlockSpec((SWA_TQ, D_MODEL), lambda b, t: (b * nt + t, Z_SQ // D_MODEL)),
                  pl.BlockSpec((SWA_TQ, 256), lambda b, t: (b * nt + t, kv_col)),
                  pl.BlockSpec((SWA_BLOCK, 256),
                               lambda b, t: ((b * nt + t) * SWA_NQ - jnp.minimum(t, 1), kv_col))],
        out_specs=pl.BlockSpec((SWA_TQ, D_MODEL), lambda b, t: (b * nt + t, 0)),
        scratch_shapes=[pltpu.VMEM((2 * SWA_HEADS, SWA_BLOCK, SWA_BLOCK), F32),
                        pltpu.VMEM((2 * SWA_KV_HEADS, SWA_TQ + SWA_BLOCK, 128), BF16),
                        pltpu.VMEM((2 * SWA_KV_HEADS, SWA_TQ + SWA_BLOCK, 256), BF16)],
        compiler_params=pltpu.CompilerParams(
            dimension_semantics=("arbitrary", "arbitrary"), vmem_limit_bytes=VMEM_LIMIT_BYTES),
        name="swa",
    )(rel_bias, sinks, bucket, z, z, z)


def _mem_kv_kernel(mem_ref, w_ref, kv_ref):
    mem_w = MEM_HEADS * MEM_HEAD_DIM
    mb = mem_ref[...].astype(BF16)
    kv_ref[:, :mem_w] = (_dot(mb, w_ref[:, :mem_w].astype(BF16)) * MEM_HEAD_DIM ** -0.5).astype(BF16)
    kv_ref[:, mem_w:] = _dot(mb, w_ref[:, mem_w:].astype(BF16)).astype(BF16)


def _mem_kv(mem2d, w_b):
    return pl.pallas_call(
        _mem_kv_kernel,
        out_shape=jax.ShapeDtypeStruct((mem2d.shape[0], w_b.shape[1]), BF16),
        compiler_params=pltpu.CompilerParams(vmem_limit_bytes=VMEM_LIMIT_BYTES),
        name="mem_kv",
    )(mem2d, w_b)


TAIL_TM = 512
TAIL_SUB = 256
TAIL_TF = 1024


def _layer_norm(y, g, b):
    mu = jnp.mean(y, axis=-1, keepdims=True)
    yc = y - mu
    var = jnp.mean(yc * yc, axis=-1, keepdims=True)
    return yc * lax.rsqrt(var + LN_EPS) * g + b


def _tail_kernel(alpha, x_ref, oa_ref, ob_ref, mq_ref, gl_ref, kv_ref, wa_ref, wb_ref, wc_ref, wo_ref,
                 g1_ref, b1_ref, wu_ref, wd_ref, g2_ref, b2_ref, o_ref):
    subs = [slice(r, r + TAIL_SUB) for r in range(0, TAIL_TM, TAIL_SUB)]
    mem_w = MEM_HEADS * MEM_HEAD_DIM
    hcols = [slice(h * MEM_HEAD_DIM, (h + 1) * MEM_HEAD_DIM) for h in range(MEM_HEADS)]

    def gate(rows, i):
        return jax.nn.sigmoid(gl_ref[rows, i * D_MODEL:(i + 1) * D_MODEL].astype(F32))

    scores = [[lax.dot_general(mq_ref[rows, c], kv_ref[:, c], _NT, preferred_element_type=F32) for c in hcols]
              for rows in subs]
    merged = [gate(rows, 0) * _dot(oa_ref[rows, :], wa_ref[...]) for rows in subs]
    probs = []
    for s_heads in scores:
        ps = []
        for s in s_heads:
            p = jnp.exp(s - jnp.max(s, axis=-1, keepdims=True))
            ps.append((p * (1.0 / jnp.sum(p, axis=-1, keepdims=True))).astype(BF16))
        probs.append(ps)
    o_c = [jnp.concatenate([_dot(p, kv_ref[:, mem_w + h * MEM_HEAD_DIM:mem_w + (h + 1) * MEM_HEAD_DIM]).astype(BF16)
                            for h, p in enumerate(ps)], axis=1) for ps in probs]
    merged = [m + gate(rows, 1) * _dot(ob_ref[rows, :], wb_ref[...]) for m, rows in zip(merged, subs)]
    merged = [m + gate(rows, 2) * _dot(oc, wc_ref[...]) for m, rows, oc in zip(merged, subs, o_c)]

    h1 = []
    for m, rows in zip(merged, subs):
        mix = _dot(m.astype(BF16), wo_ref[...])
        h1.append(_layer_norm(alpha * x_ref[rows, :] + mix, g1_ref[...], b1_ref[...]))

    hb = [h.astype(BF16) for h in h1]
    ff = [None] * len(subs)
    for f0 in range(0, D_FF, TAIL_TF):
        us = [jnp.maximum(_dot(hb[i], wu_ref[:, f0:f0 + TAIL_TF]), 0.0) for i in range(len(subs))]
        for i, u in enumerate(us):
            d = _dot((u * u).astype(BF16), wd_ref[f0:f0 + TAIL_TF, :])
            ff[i] = d if ff[i] is None else ff[i] + d
    for i, rows in enumerate(subs):
        o_ref[rows, :] = _layer_norm(alpha * h1[i] + ff[i], g2_ref[...], b2_ref[...])


def _tail(alpha, x2d, o_a, o_b, z, kv, wa, wb, wc, wo, ln1_g, ln1_b, wu, wd, ln2_g, ln2_b, batch, seq):
    t = x2d.shape[0]
    per_batch = seq // TAIL_TM
    row = lambda i: (i, 0)
    weights = (wa, wb, wc, wo, ln1_g, ln1_b, wu, wd, ln2_g, ln2_b)
    return pl.pallas_call(
        functools.partial(_tail_kernel, alpha),
        out_shape=jax.ShapeDtypeStruct((t, D_MODEL), F32),
        grid=(t // TAIL_TM,),
        in_specs=[pl.BlockSpec((TAIL_TM, D_MODEL), row),
                  pl.BlockSpec((TAIL_TM, D_MODEL), row),
                  pl.BlockSpec((TAIL_TM, D_MODEL), row),
                  pl.BlockSpec((TAIL_TM, D_MODEL), lambda i: (i, Z_MQ // D_MODEL)),
                  pl.BlockSpec((TAIL_TM, 3 * D_MODEL), lambda i: (i, Z_GL // (3 * D_MODEL))),
                  pl.BlockSpec((kv.shape[0] // batch, kv.shape[1]), lambda i: (i // per_batch, 0))]
                 + [_resident(w.shape) for w in weights],
        out_specs=pl.BlockSpec((TAIL_TM, D_MODEL), row),
        compiler_params=pltpu.CompilerParams(
            dimension_semantics=("arbitrary",), vmem_limit_bytes=VMEM_LIMIT_BYTES),
        name="tail",
    )(x2d, o_a, o_b, z, z, kv, *weights)


def kernel(x, mem, w_in, lb_logits, hg_norm_gain, swa_sinks, rel_bias, w_mem_kv, w_branch_hg, w_branch_swa,
           w_branch_mem, w_out, ln1_g, ln1_b, w_up, w_down, ln2_g, ln2_b):
    batch, seq, _ = x.shape
    depth = w_in.shape[0]
    assert depth == 1 and lb_logits.shape[0] == depth + 1
    alpha = (2.0 * depth) ** 0.25
    out_dtype = x.dtype

    w_in_b = w_in[0].astype(BF16)
    x2d = x.reshape(batch * seq, D_MODEL).astype(F32)
    later = [w[0].astype(F32) for w in (w_branch_hg, w_branch_swa, w_branch_mem, w_out, w_up, w_down)]
    z, (wa, wb, wc, wo, wu, wd) = _proj_in(x2d, w_in_b, later)
    o_a = _hgrn(z, lb_logits.astype(F32), hg_norm_gain.astype(F32), batch, seq)
    o_b = _swa(z, rel_bias.astype(F32), swa_sinks[0].astype(F32), batch, seq)
    kv = _mem_kv(mem.reshape(batch * mem.shape[1], D_MODEL).astype(F32), w_mem_kv[0].astype(F32))
    h2 = _tail(alpha, x2d, o_a, o_b, z, kv, wa, wb, wc, wo, ln1_g.astype(F32), ln1_b.astype(F32),
               wu, wd, ln2_g.astype(F32), ln2_b.astype(F32), batch, seq)
    return h2.reshape(batch, seq, D_MODEL).astype(out_dtype)
```

```python
import functools
import math

import numpy as np
import jax
import jax.numpy as jnp
from jax import lax
from jax.experimental import pallas as pl
from jax.experimental.pallas import tpu as pltpu

F32 = jnp.float32
BF16 = jnp.bfloat16

D_MODEL = 1024
HG_HEADS = 8
HG_DK = 128
HG_DV = 128
HG_CHUNK = 64
SWA_HEADS = 16
SWA_KV_HEADS = 2
SWA_GROUP = SWA_HEADS // SWA_KV_HEADS
SWA_HEAD_DIM = 64
SWA_BLOCK = 128
MEM_HEADS = 4
MEM_HEAD_DIM = 256
NUM_BUCKETS = 32
MAX_DISTANCE = 128
D_FF = 4 * D_MODEL
LN_EPS = 1e-5
RMS_EPS = 1e-6
LOG2E = 1.4426950408889634

Z_HQ, Z_HF, Z_HI, Z_HG, Z_SQ, Z_KV = (i * D_MODEL for i in range(6))
ZF_COLS = Z_KV + 2 * SWA_KV_HEADS * SWA_HEAD_DIM
ZB_GL, ZB_MQ = 0, 3 * D_MODEL
ZB_COLS = 4 * D_MODEL

VMEM_LIMIT_BYTES = 56 * 1024 * 1024

_NT = (((1,), (1,)), ((), ()))
_TN = (((0,), (0,)), ((), ()))


def _dot(a, b):
    return jnp.dot(a, b, preferred_element_type=F32)


def _resident(shape):
    nd = len(shape)
    return pl.BlockSpec(shape, lambda *_: (0,) * nd, pipeline_mode=pl.Buffered(1))


PROJ_TM = 512
PROJ_TN = 1024


def _proj_in_kernel(n_cast, x_ref, w_ref, *refs):
    cast_in, z_ref, cast_out = refs[:n_cast], refs[n_cast], refs[n_cast + 1:]
    xb = x_ref[...].astype(BF16)
    for n0 in range(0, ZF_COLS, PROJ_TN):
        n1 = min(n0 + PROJ_TN, ZF_COLS)
        z_ref[:, n0:n1] = _dot(xb, w_ref[:, n0:n1]).astype(BF16)
    for src_ref, dst_ref in zip(cast_in, cast_out):
        dst_ref[...] = src_ref[...].astype(BF16)


def _proj_in(x2d, w_front, later_weights):
    t = x2d.shape[0]
    steps = t // PROJ_TM
    slab = lambda w: pl.BlockSpec((w.shape[0] // steps, w.shape[1]), lambda i: (i, 0))
    outs = pl.pallas_call(
        functools.partial(_proj_in_kernel, len(later_weights)),
        out_shape=[jax.ShapeDtypeStruct((t, ZF_COLS), BF16)]
                  + [jax.ShapeDtypeStruct(w.shape, BF16) for w in later_weights],
        grid=(steps,),
        in_specs=[pl.BlockSpec((PROJ_TM, D_MODEL), lambda i: (i, 0)),
                  _resident((D_MODEL, ZF_COLS))] + [slab(w) for w in later_weights],
        out_specs=[pl.BlockSpec((PROJ_TM, ZF_COLS), lambda i: (i, 0))] + [slab(w) for w in later_weights],
        compiler_params=pltpu.CompilerParams(
            dimension_semantics=("arbitrary",), vmem_limit_bytes=VMEM_LIMIT_BYTES),
        name="proj_in",
    )(x2d, w_front, *later_weights)
    return outs[0], outs[1:]


HG_TB = 512


HG_NCHUNK = HG_TB // HG_CHUNK
HG_BACK_TN = ZB_COLS // HG_NCHUNK


def _hgrn_kernel(lbl_ref, gain_ref, tril_ref, zq_ref, zf_ref, zi_ref, zg_ref, x_ref, wb_ref,
                 o_ref, zb_ref, st_ref, stb_ref):
    @pl.when(pl.program_id(1) == 0)
    def _():
        st_ref[...] = jnp.zeros_like(st_ref)
        stb_ref[...] = jnp.zeros_like(stb_ref)

    l0 = lbl_ref[0:1, :]
    l1 = lbl_ref[1:2, :]
    lmax = jnp.maximum(l0, l1)
    e0 = jnp.exp(l0 - lmax)
    e1 = jnp.exp(l1 - lmax)
    lb_all = e0 / (e0 + e1)
    f_mid = 0.5 * (1.0 + lb_all)
    f_amp = 0.5 * (1.0 - lb_all)
    gain_s = gain_ref[...] * (HG_DV ** 0.5)
    tril2 = tril_ref[...]
    causal = tril2[:, :HG_CHUNK] > 0
    heads = range(HG_HEADS)
    hcols = [slice(h * HG_DK, (h + 1) * HG_DK) for h in heads]

    xb = x_ref[...].astype(BF16)

    def back_projection(c):
        src = c * HG_BACK_TN
        dst = src + ZB_MQ if src < D_MODEL else src - D_MODEL
        zb_ref[:, dst:dst + HG_BACK_TN] = _dot(xb, wb_ref[:, src:src + HG_BACK_TN]).astype(BF16)

    def chunk(c):
        rows = slice(c * HG_CHUNK, (c + 1) * HG_CHUNK)

        ks, bs = [], []
        for h in heads:
            f = f_mid[:, hcols[h]] + f_amp[:, hcols[h]] * jnp.tanh(0.5 * zf_ref[rows, hcols[h]].astype(F32))
            logf = jnp.log(f)
            ks.append(1.0 - f)
            hi = logf.astype(BF16)
            lo = (logf - hi.astype(F32)).astype(BF16)
            bs.append(_dot(tril2, jnp.concatenate([hi, lo], axis=0)))

        attns, upds, qins, ebls = [], [], [], []
        for h in heads:
            b = bs[h]
            ebl = jnp.exp(b[HG_CHUNK - 1:HG_CHUNK, :])
            eb = jnp.exp2(b * LOG2E)
            q_in = (zq_ref[rows, hcols[h]].astype(F32) * eb).astype(BF16)
            kin = ks[h] * (1.0 / eb)
            k_in = kin.astype(BF16)
            k_out = (kin * ebl).astype(BF16)
            attns.append(lax.dot_general(q_in, k_in, _NT, preferred_element_type=F32))
            upds.append(lax.dot_general(zi_ref[rows, hcols[h]], k_out, _TN, preferred_element_type=F32))
            qins.append(q_in)
            ebls.append(ebl)

        outs = []
        for h in heads:
            attn = jnp.where(causal, attns[h], 0.0).astype(BF16)
            lhs = jnp.concatenate([qins[h], attn], axis=1)
            rhs = jnp.concatenate([stb_ref[h], zi_ref[rows, hcols[h]]], axis=0)
            outs.append(_dot(lhs, rhs))
            st = st_ref[h] * ebls[h] + upds[h]
            st_ref[h] = st
            stb_ref[h] = st.astype(BF16).T

        for h in heads:
            o = outs[h]
            hg = 0.5 * zg_ref[rows, hcols[h]].astype(F32)
            ss = jnp.sum(o * o, axis=-1, keepdims=True)
            gate = gain_s[:, hcols[h]] * (hg + hg * jnp.tanh(hg))
            o_ref[rows, hcols[h]] = (o * lax.rsqrt(ss + HG_DV * RMS_EPS) * gate).astype(BF16)

    for c in range(HG_NCHUNK):
        chunk(c)
        back_projection(c)


def _hgrn(z, x2d, w_back, lb_logits, gain, batch, seq):
    nt = seq // HG_TB
    tril = np.tril(np.ones((HG_CHUNK, HG_CHUNK), np.float32))
    tril = jnp.asarray(np.concatenate([tril, tril], axis=1), BF16)

    def zspec(col):
        return pl.BlockSpec((HG_TB, D_MODEL), lambda b, t, col=col: (b * nt + t, col // D_MODEL))

    tile = lambda b, t: (b * nt + t, 0)
    return pl.pallas_call(
        _hgrn_kernel,
        out_shape=[jax.ShapeDtypeStruct((batch * seq, D_MODEL), BF16),
                   jax.ShapeDtypeStruct((batch * seq, ZB_COLS), BF16)],
        grid=(batch, nt),
        in_specs=[_resident(lb_logits.shape), _resident(gain.shape), _resident(tril.shape),
                  zspec(Z_HQ), zspec(Z_HF), zspec(Z_HI), zspec(Z_HG),
                  pl.BlockSpec((HG_TB, D_MODEL), tile), _resident(w_back.shape)],
        out_specs=[pl.BlockSpec((HG_TB, D_MODEL), tile), pl.BlockSpec((HG_TB, ZB_COLS), tile)],
        scratch_shapes=[pltpu.VMEM((HG_HEADS, HG_DV, HG_DK), F32),
                        pltpu.VMEM((HG_HEADS, HG_DV, HG_DK), BF16)],
        compiler_params=pltpu.CompilerParams(
            dimension_semantics=("arbitrary", "arbitrary"), vmem_limit_bytes=VMEM_LIMIT_BYTES),
        name="hgrn",
    )(lb_logits, gain, tril, z, z, z, z, x2d, w_back)


def _swa_bucket_table():
    qi = np.arange(SWA_BLOCK)[:, None]
    kj = np.arange(SWA_BLOCK)[None, :]
    n = np.where(kj <= qi, qi - kj, qi + SWA_BLOCK - kj).astype(np.int32)
    max_exact = NUM_BUCKETS // 2
    nf = np.maximum(n, 1).astype(np.float32)
    large = max_exact + (np.log(nf / np.float32(max_exact)) / np.float32(math.log(MAX_DISTANCE / max_exact))
                         * np.float32(NUM_BUCKETS - max_exact)).astype(np.int32)
    large = np.minimum(large, NUM_BUCKETS - 1)
    return np.where(n < max_exact, n, large).astype(np.int32)


SWA_TQ = 512
SWA_NQ = SWA_TQ // SWA_BLOCK
SWA_MASKED = -1e30


def _swa_kernel(relb_ref, sink_ref, bucket_ref, q_ref, kvc_ref, kvp_ref, o_ref, bias_ref, kz_ref, vz_ref):
    t = pl.program_id(1)
    qi = lax.broadcasted_iota(jnp.int32, (SWA_BLOCK, SWA_BLOCK), 0)
    kj = lax.broadcasted_iota(jnp.int32, (SWA_BLOCK, SWA_BLOCK), 1)
    tri = kj <= qi

    @pl.when((pl.program_id(0) == 0) & (t == 0))
    def _():
        bucket = bucket_ref[...]
        for h in range(SWA_HEADS):
            acc = jnp.zeros((SWA_BLOCK, SWA_BLOCK), F32)
            for bk in range(NUM_BUCKETS):
                acc = jnp.where(bucket == bk, relb_ref[bk, h], acc)
            bias_ref[h] = acc
            bias_ref[SWA_HEADS + h] = jnp.where(tri, acc, SWA_MASKED)
        vz_ref[:, :, 128:256] = jnp.ones((2 * SWA_KV_HEADS, SWA_TQ + SWA_BLOCK, 128), BF16)

    lane = lax.broadcasted_iota(jnp.int32, (SWA_BLOCK, 2 * SWA_HEAD_DIM), 1)
    low = lane < SWA_HEAD_DIM
    scale = SWA_HEAD_DIM ** -0.5
    for blk in range(SWA_NQ + 1):
        rows = slice(blk * SWA_BLOCK, (blk + 1) * SWA_BLOCK)
        if blk == 0:
            kk = kvp_ref[:, 0:128].astype(F32) * scale
            vv = kvp_ref[:, 128:256].astype(F32)
        else:
            src = slice((blk - 1) * SWA_BLOCK, blk * SWA_BLOCK)
            kk = kvc_ref[src, 0:128].astype(F32) * scale
            vv = kvc_ref[src, 128:256].astype(F32)
        kk_sw = pltpu.roll(kk, SWA_HEAD_DIM, 1)
        vv_sw = pltpu.roll(vv, SWA_HEAD_DIM, 1)
        for g in range(SWA_KV_HEADS):
            k_lo, k_hi = (kk, kk_sw) if g == 0 else (kk_sw, kk)
            v_lo, v_hi = (vv, vv_sw) if g == 0 else (vv_sw, vv)
            kz_ref[2 * g + 0, rows, :] = jnp.where(low, k_lo, 0.0).astype(BF16)
            kz_ref[2 * g + 1, rows, :] = jnp.where(low, 0.0, k_hi).astype(BF16)
            vz_ref[2 * g + 0, rows, 0:128] = jnp.where(low, v_lo, 0.0).astype(BF16)
            vz_ref[2 * g + 1, rows, 0:128] = jnp.where(low, 0.0, v_hi).astype(BF16)

    def sub_block(i, carry):
        r0 = pl.multiple_of(i * SWA_BLOCK, SWA_BLOCK)
        qrows = pl.ds(r0, SWA_BLOCK)
        krows = pl.ds(r0, 2 * SWA_BLOCK)
        bias_off = jnp.where((t == 0) & (i == 0), SWA_HEADS, 0)
        for pair in range(SWA_HEADS // 2):
            g = (2 * pair) // SWA_GROUP
            qp = q_ref[qrows, pair * 128:(pair + 1) * 128]
            o_pair = None
            for half in range(2):
                h = 2 * pair + half
                slot = 2 * g + half
                s2 = lax.dot_general(qp, kz_ref[slot, krows, :], _NT, preferred_element_type=F32)
                s = jnp.where(tri, s2[:, SWA_BLOCK:], s2[:, :SWA_BLOCK]) + bias_ref[bias_off + h]
                sink = sink_ref[h]
                m = jnp.maximum(jnp.max(s, axis=-1, keepdims=True), sink)
                p = jnp.exp(s - m)
                p2 = jnp.concatenate([jnp.where(tri, 0.0, p), jnp.where(tri, p, 0.0)], axis=1).astype(BF16)
                od = _dot(p2, vz_ref[slot, krows, :])
                den = od[:, 128:] + jnp.exp(sink - m)
                o_h = od[:, :128] * (1.0 / den)
                o_pair = o_h if o_pair is None else o_pair + o_h
            o_ref[qrows, pair * 128:(pair + 1) * 128] = o_pair.astype(BF16)
        return carry

    lax.fori_loop(0, SWA_NQ, sub_block, 0, unroll=2)


def _swa(z, rel_bias, sinks, batch, seq):
    nt = seq // SWA_TQ
    bucket = jnp.asarray(_swa_bucket_table())
    kv_col = Z_KV // 256
    smem = pl.BlockSpec(memory_space=pltpu.SMEM)
    return pl.pallas_call(
        _swa_kernel,
        out_shape=jax.ShapeDtypeStruct((batch * seq, D_MODEL), BF16),
        grid=(batch, nt),
        in_specs=[smem, smem, _resident(bucket.shape),
                  pl.BlockSpec((SWA_TQ, D_MODEL), lambda b, t: (b * nt + t, Z_SQ // D_MODEL)),
                  pl.BlockSpec((SWA_TQ, 256), lambda b, t: (b * nt + t, kv_col)),
                  pl.BlockSpec((SWA_BLOCK, 256),
                               lambda b, t: ((b * nt + t) * SWA_NQ - jnp.minimum(t, 1), kv_col))],
        out_specs=pl.BlockSpec((SWA_TQ, D_MODEL), lambda b, t: (b * nt + t, 0)),
        scratch_shapes=[pltpu.VMEM((2 * SWA_HEADS, SWA_BLOCK, SWA_BLOCK), F32),
                        pltpu.VMEM((2 * SWA_KV_HEADS, SWA_TQ + SWA_BLOCK, 128), BF16),
                        pltpu.VMEM((2 * SWA_KV_HEADS, SWA_TQ + SWA_BLOCK, 256), BF16)],
        compiler_params=pltpu.CompilerParams(
            dimension_semantics=("arbitrary", "arbitrary"), vmem_limit_bytes=VMEM_LIMIT_BYTES),
        name="swa",
    )(rel_bias, sinks, bucket, z, z, z)


def _mem_kv_kernel(mem_ref, w_ref, kv_ref):
    mem_w = MEM_HEADS * MEM_HEAD_DIM
    mb = mem_ref[...].astype(BF16)
    kv_ref[:, :mem_w] = (_dot(mb, w_ref[:, :mem_w].astype(BF16)) * MEM_HEAD_DIM ** -0.5).astype(BF16)
    kv_ref[:, mem_w:] = _dot(mb, w_ref[:, mem_w:].astype(BF16)).astype(BF16)


def _mem_kv(mem2d, w_b):
    return pl.pallas_call(
        _mem_kv_kernel,
        out_shape=jax.ShapeDtypeStruct((mem2d.shape[0], w_b.shape[1]), BF16),
        compiler_params=pltpu.CompilerParams(vmem_limit_bytes=VMEM_LIMIT_BYTES),
        name="mem_kv",
    )(mem2d, w_b)


TAIL_TM = 512
TAIL_SUB = 256
TAIL_TF = 1024


def _layer_norm(y, g, b):
    mu = jnp.mean(y, axis=-1, keepdims=True)
    yc = y - mu
    var = jnp.mean(yc * yc, axis=-1, keepdims=True)
    return yc * lax.rsqrt(var + LN_EPS) * g + b


def _tail_kernel(alpha, x_ref, oa_ref, ob_ref, mq_ref, gl_ref, kv_ref, wa_ref, wb_ref, wc_ref, wo_ref,
                 g1_ref, b1_ref, wu_ref, wd_ref, g2_ref, b2_ref, o_ref):
    subs = [slice(r, r + TAIL_SUB) for r in range(0, TAIL_TM, TAIL_SUB)]
    mem_w = MEM_HEADS * MEM_HEAD_DIM
    hcols = [slice(h * MEM_HEAD_DIM, (h + 1) * MEM_HEAD_DIM) for h in range(MEM_HEADS)]

    def gate(rows, i):
        return jax.nn.sigmoid(gl_ref[rows, i * D_MODEL:(i + 1) * D_MODEL].astype(F32))

    scores = [[lax.dot_general(mq_ref[rows, c], kv_ref[:, c], _NT, preferred_element_type=F32) for c in hcols]
              for rows in subs]
    merged = [gate(rows, 0) * _dot(oa_ref[rows, :], wa_ref[...]) for rows in subs]
    probs = []
    for s_heads in scores:
        ps = []
        for s in s_heads:
            p = jnp.exp(s - jnp.max(s, axis=-1, keepdims=True))
            ps.append((p * (1.0 / jnp.sum(p, axis=-1, keepdims=True))).astype(BF16))
        probs.append(ps)
    o_c = [jnp.concatenate([_dot(p, kv_ref[:, mem_w + h * MEM_HEAD_DIM:mem_w + (h + 1) * MEM_HEAD_DIM]).astype(BF16)
                            for h, p in enumerate(ps)], axis=1) for ps in probs]
    merged = [m + gate(rows, 1) * _dot(ob_ref[rows, :], wb_ref[...]) for m, rows in zip(merged, subs)]
    merged = [m + gate(rows, 2) * _dot(oc, wc_ref[...]) for m, rows, oc in zip(merged, subs, o_c)]

    h1 = []
    for m, rows in zip(merged, subs):
        mix = _dot(m.astype(BF16), wo_ref[...])
        h1.append(_layer_norm(alpha * x_ref[rows, :] + mix, g1_ref[...], b1_ref[...]))

    hb = [h.astype(BF16) for h in h1]
    ff = [None] * len(subs)
    for f0 in range(0, D_FF, TAIL_TF):
        us = [jnp.maximum(_dot(hb[i], wu_ref[:, f0:f0 + TAIL_TF]), 0.0) for i in range(len(subs))]
        for i, u in enumerate(us):
            d = _dot((u * u).astype(BF16), wd_ref[f0:f0 + TAIL_TF, :])
            ff[i] = d if ff[i] is None else ff[i] + d
    for i, rows in enumerate(subs):
        o_ref[rows, :] = _layer_norm(alpha * h1[i] + ff[i], g2_ref[...], b2_ref[...])


def _tail(alpha, x2d, o_a, o_b, z, kv, wa, wb, wc, wo, ln1_g, ln1_b, wu, wd, ln2_g, ln2_b, batch, seq):
    t = x2d.shape[0]
    per_batch = seq // TAIL_TM
    row = lambda i: (i, 0)
    weights = (wa, wb, wc, wo, ln1_g, ln1_b, wu, wd, ln2_g, ln2_b)
    return pl.pallas_call(
        functools.partial(_tail_kernel, alpha),
        out_shape=jax.ShapeDtypeStruct((t, D_MODEL), F32),
        grid=(t // TAIL_TM,),
        in_specs=[pl.BlockSpec((TAIL_TM, D_MODEL), row),
                  pl.BlockSpec((TAIL_TM, D_MODEL), row),
                  pl.BlockSpec((TAIL_TM, D_MODEL), row),
                  pl.BlockSpec((TAIL_TM, D_MODEL), lambda i: (i, ZB_MQ // D_MODEL)),
                  pl.BlockSpec((TAIL_TM, 3 * D_MODEL), lambda i: (i, ZB_GL // (3 * D_MODEL))),
                  pl.BlockSpec((kv.shape[0] // batch, kv.shape[1]), lambda i: (i // per_batch, 0))]
                 + [_resident(w.shape) for w in weights],
        out_specs=pl.BlockSpec((TAIL_TM, D_MODEL), row),
        compiler_params=pltpu.CompilerParams(
            dimension_semantics=("arbitrary",), vmem_limit_bytes=VMEM_LIMIT_BYTES),
        name="tail",
    )(x2d, o_a, o_b, z, z, kv, *weights)


def kernel(x, mem, w_in, lb_logits, hg_norm_gain, swa_sinks, rel_bias, w_mem_kv, w_branch_hg, w_branch_swa,
           w_branch_mem, w_out, ln1_g, ln1_b, w_up, w_down, ln2_g, ln2_b):
    batch, seq, _ = x.shape
    depth = w_in.shape[0]
    assert depth == 1 and lb_logits.shape[0] == depth + 1
    alpha = (2.0 * depth) ** 0.25
    out_dtype = x.dtype

    w_front = w_in[0, :, :ZF_COLS].astype(BF16)
    w_back = w_in[0, :, ZF_COLS:].astype(BF16)
    x2d = x.reshape(batch * seq, D_MODEL).astype(F32)
    later = [w[0].astype(F32) for w in (w_branch_hg, w_branch_swa, w_branch_mem, w_out, w_up, w_down)]
    z, (wa, wb, wc, wo, wu, wd) = _proj_in(x2d, w_front, later)
    o_a, zb = _hgrn(z, x2d, w_back, lb_logits.astype(F32), hg_norm_gain.astype(F32), batch, seq)
    o_b = _swa(z, rel_bias.astype(F32), swa_sinks[0].astype(F32), batch, seq)
    kv = _mem_kv(mem.reshape(batch * mem.shape[1], D_MODEL).astype(F32), w_mem_kv[0].astype(F32))
    h2 = _tail(alpha, x2d, o_a, o_b, zb, kv, wa, wb, wc, wo, ln1_g.astype(F32), ln1_b.astype(F32),
               wu, wd, ln2_g.astype(F32), ln2_b.astype(F32), batch, seq)
    return h2.reshape(batch, seq, D_MODEL).astype(out_dtype)
```

```python
import functools
import math

import numpy as np
import jax
import jax.numpy as jnp
from jax import lax
from jax.experimental import pallas as pl
from jax.experimental.pallas import tpu as pltpu

F32 = jnp.float32
BF16 = jnp.bfloat16

D_MODEL = 1024
HG_HEADS = 8
HG_DK = 128
HG_DV = 128
HG_CHUNK = 64
SWA_HEADS = 16
SWA_KV_HEADS = 2
SWA_GROUP = SWA_HEADS // SWA_KV_HEADS
SWA_HEAD_DIM = 64
SWA_BLOCK = 128
MEM_HEADS = 4
MEM_HEAD_DIM = 256
NUM_BUCKETS = 32
MAX_DISTANCE = 128
D_FF = 4 * D_MODEL
LN_EPS = 1e-5
RMS_EPS = 1e-6
LOG2E = 1.4426950408889634

Z_HQ, Z_HF, Z_HI, Z_HG, Z_SQ, Z_MQ = (i * D_MODEL for i in range(6))
Z_GL = 6 * D_MODEL
Z_KV = 9 * D_MODEL
Z_COLS = Z_KV + 2 * SWA_KV_HEADS * SWA_HEAD_DIM

VMEM_LIMIT_BYTES = 56 * 1024 * 1024

_NT = (((1,), (1,)), ((), ()))
_TN = (((0,), (0,)), ((), ()))


def _dot(a, b):
    return jnp.dot(a, b, preferred_element_type=F32)


def _resident(shape):
    nd = len(shape)
    return pl.BlockSpec(shape, lambda *_: (0,) * nd, pipeline_mode=pl.Buffered(1))


PROJ_TM = 512
PROJ_TN = 1024


def _proj_col_chunks():
    kv_w = 2 * SWA_KV_HEADS * SWA_HEAD_DIM
    src_kv = Z_MQ
    segments = ((0, 0, Z_MQ), (src_kv + kv_w, Z_MQ, Z_KV - Z_MQ), (src_kv, Z_KV, kv_w))
    chunks = []
    for src, dst, width in segments:
        for off in range(0, width, PROJ_TN):
            chunks.append((src + off, dst + off, min(PROJ_TN, width - off)))
    return chunks


def _proj_in_kernel(n_cast, x_ref, w_ref, lbl_ref, gain_ref, *refs):
    cast_in, (z_ref, lo_ref, k_ref), cast_out = refs[:n_cast], refs[n_cast:n_cast + 3], refs[n_cast + 3:]
    l0 = lbl_ref[0:1, :]
    l1 = lbl_ref[1:2, :]
    lmax = jnp.maximum(l0, l1)
    e0 = jnp.exp(l0 - lmax)
    e1 = jnp.exp(l1 - lmax)
    lb = e0 / (e0 + e1)
    f_mid = 0.5 * (1.0 + lb)
    f_amp = 0.5 * (1.0 - lb)
    gain_s = gain_ref[...] * (HG_DV ** 0.5)

    xb = x_ref[...].astype(BF16)
    for src, dst, width in _proj_col_chunks():
        res = _dot(xb, w_ref[:, src:src + width])
        if dst == Z_HF:
            f = f_mid + f_amp * jnp.tanh(0.5 * res)
            logf = jnp.log(f)
            hi = logf.astype(BF16)
            z_ref[:, dst:dst + width] = hi
            lo_ref[...] = (logf - hi.astype(F32)).astype(BF16)
            k_ref[...] = 1.0 - f
        elif dst == Z_HG:
            hg = 0.5 * res
            z_ref[:, dst:dst + width] = (gain_s * (hg + hg * jnp.tanh(hg))).astype(BF16)
        else:
            z_ref[:, dst:dst + width] = res.astype(BF16)
    for src_ref, dst_ref in zip(cast_in, cast_out):
        dst_ref[...] = src_ref[...].astype(BF16)


def _proj_in(x2d, w_in_b, lb_logits, gain, later_weights):
    t = x2d.shape[0]
    steps = t // PROJ_TM
    slab = lambda w: pl.BlockSpec((w.shape[0] // steps, w.shape[1]), lambda i: (i, 0))
    row = lambda width: pl.BlockSpec((PROJ_TM, width), lambda i: (i, 0))
    outs = pl.pallas_call(
        functools.partial(_proj_in_kernel, len(later_weights)),
        out_shape=[jax.ShapeDtypeStruct((t, Z_COLS), BF16), jax.ShapeDtypeStruct((t, D_MODEL), BF16),
                   jax.ShapeDtypeStruct((t, D_MODEL), F32)]
                  + [jax.ShapeDtypeStruct(w.shape, BF16) for w in later_weights],
        grid=(steps,),
        in_specs=[row(D_MODEL), _resident((D_MODEL, Z_COLS)), _resident(lb_logits.shape), _resident(gain.shape)]
                 + [slab(w) for w in later_weights],
        out_specs=[row(Z_COLS), row(D_MODEL), row(D_MODEL)] + [slab(w) for w in later_weights],
        compiler_params=pltpu.CompilerParams(
            dimension_semantics=("arbitrary",), vmem_limit_bytes=VMEM_LIMIT_BYTES),
        name="proj_in",
    )(x2d, w_in_b, lb_logits, gain, *later_weights)
    return outs[0], outs[1], outs[2], outs[3:]


HG_TB = 512


def _hgrn_kernel(tril_ref, zq_ref, zhi_ref, zlo_ref, zk_ref, zi_ref, zg_ref, o_ref, st_ref, stb_ref):
    @pl.when(pl.program_id(1) == 0)
    def _():
        st_ref[...] = jnp.zeros_like(st_ref)
        stb_ref[...] = jnp.zeros_like(stb_ref)

    tril2 = tril_ref[...]
    causal = tril2[:, :HG_CHUNK] > 0
    heads = range(HG_HEADS)
    hcols = [slice(h * HG_DK, (h + 1) * HG_DK) for h in heads]

    def chunk(c, carry):
        r0 = pl.multiple_of(c * HG_CHUNK, HG_CHUNK)
        rows = pl.ds(r0, HG_CHUNK)

        bs = [_dot(tril2, jnp.concatenate([zhi_ref[rows, hcols[h]], zlo_ref[rows, hcols[h]]], axis=0))
              for h in heads]

        attns, upds, qins, ebls = [], [], [], []
        for h in heads:
            b = bs[h]
            ebl = jnp.exp(b[HG_CHUNK - 1:HG_CHUNK, :])
            eb = jnp.exp2(b * LOG2E)
            q_in = (zq_ref[rows, hcols[h]].astype(F32) * eb).astype(BF16)
            kin = zk_ref[rows, hcols[h]] * (1.0 / eb)
            k_in = kin.astype(BF16)
            k_out = (kin * ebl).astype(BF16)
            attns.append(lax.dot_general(q_in, k_in, _NT, preferred_element_type=F32))
            upds.append(lax.dot_general(zi_ref[rows, hcols[h]], k_out, _TN, preferred_element_type=F32))
            qins.append(q_in)
            ebls.append(ebl)

        outs = []
        for h in heads:
            attn = jnp.where(causal, attns[h], 0.0).astype(BF16)
            lhs = jnp.concatenate([qins[h], attn], axis=1)
            rhs = jnp.concatenate([stb_ref[h], zi_ref[rows, hcols[h]]], axis=0)
            outs.append(_dot(lhs, rhs))
            st = st_ref[h] * ebls[h] + upds[h]
            st_ref[h] = st
            stb_ref[h] = st.astype(BF16).T

        for h in heads:
            o = outs[h]
            ss = jnp.sum(o * o, axis=-1, keepdims=True)
            gate = zg_ref[rows, hcols[h]].astype(F32)
            o_ref[rows, hcols[h]] = (o * lax.rsqrt(ss + HG_DV * RMS_EPS) * gate).astype(BF16)
        return carry

    lax.fori_loop(0, HG_TB // HG_CHUNK, chunk, 0, unroll=4)


def _hgrn(z, z_lo, z_k, batch, seq):
    nt = seq // HG_TB
    tril = np.tril(np.ones((HG_CHUNK, HG_CHUNK), np.float32))
    tril = jnp.asarray(np.concatenate([tril, tril], axis=1), BF16)
    tile = lambda b, t: (b * nt + t, 0)

    def zspec(col):
        return pl.BlockSpec((HG_TB, D_MODEL), lambda b, t, col=col: (b * nt + t, col // D_MODEL))

    return pl.pallas_call(
        _hgrn_kernel,
        out_shape=jax.ShapeDtypeStruct((batch * seq, D_MODEL), BF16),
        grid=(batch, nt),
        in_specs=[_resident(tril.shape), zspec(Z_HQ), zspec(Z_HF),
                  pl.BlockSpec((HG_TB, D_MODEL), tile), pl.BlockSpec((HG_TB, D_MODEL), tile),
                  zspec(Z_HI), zspec(Z_HG)],
        out_specs=pl.BlockSpec((HG_TB, D_MODEL), tile),
        scratch_shapes=[pltpu.VMEM((HG_HEADS, HG_DV, HG_DK), F32),
                        pltpu.VMEM((HG_HEADS, HG_DV, HG_DK), BF16)],
        compiler_params=pltpu.CompilerParams(
            dimension_semantics=("arbitrary", "arbitrary"), vmem_limit_bytes=VMEM_LIMIT_BYTES),
        name="hgrn",
    )(tril, z, z, z_lo, z_k, z, z)


def _swa_bucket_table():
    qi = np.arange(SWA_BLOCK)[:, None]
    kj = np.arange(SWA_BLOCK)[None, :]
    n = np.where(kj <= qi, qi - kj, qi + SWA_BLOCK - kj).astype(np.int32)
    max_exact = NUM_BUCKETS // 2
    nf = np.maximum(n, 1).astype(np.float32)
    large = max_exact + (np.log(nf / np.float32(max_exact)) / np.float32(math.log(MAX_DISTANCE / max_exact))
                         * np.float32(NUM_BUCKETS - max_exact)).astype(np.int32)
    large = np.minimum(large, NUM_BUCKETS - 1)
    return np.where(n < max_exact, n, large).astype(np.int32)


SWA_TQ = 512
SWA_NQ = SWA_TQ // SWA_BLOCK
SWA_MASKED = -1e30


def _swa_kernel(relb_ref, sink_ref, bucket_ref, q_ref, kvc_ref, kvp_ref, o_ref, bias_ref, kz_ref, vz_ref):
    t = pl.program_id(1)
    qi = lax.broadcasted_iota(jnp.int32, (SWA_BLOCK, SWA_BLOCK), 0)
    kj = lax.broadcasted_iota(jnp.int32, (SWA_BLOCK, SWA_BLOCK), 1)
    tri = kj <= qi

    @pl.when((pl.program_id(0) == 0) & (t == 0))
    def _():
        bucket = bucket_ref[...]
        for h in range(SWA_HEADS):
            acc = jnp.zeros((SWA_BLOCK, SWA_BLOCK), F32)
            for bk in range(NUM_BUCKETS):
                acc = jnp.where(bucket == bk, relb_ref[bk, h], acc)
            bias_ref[h] = acc
            bias_ref[SWA_HEADS + h] = jnp.where(tri, acc, SWA_MASKED)
        vz_ref[:, :, 128:256] = jnp.ones((2 * SWA_KV_HEADS, SWA_TQ + SWA_BLOCK, 128), BF16)

    lane = lax.broadcasted_iota(jnp.int32, (SWA_BLOCK, 2 * SWA_HEAD_DIM), 1)
    low = lane < SWA_HEAD_DIM
    scale = SWA_HEAD_DIM ** -0.5
    for blk in range(SWA_NQ + 1):
        rows = slice(blk * SWA_BLOCK, (blk + 1) * SWA_BLOCK)
        if blk == 0:
            kk = kvp_ref[:, 0:128].astype(F32) * scale
            vv = kvp_ref[:, 128:256].astype(F32)
        else:
            src = slice((blk - 1) * SWA_BLOCK, blk * SWA_BLOCK)
            kk = kvc_ref[src, 0:128].astype(F32) * scale
            vv = kvc_ref[src, 128:256].astype(F32)
        kk_sw = pltpu.roll(kk, SWA_HEAD_DIM, 1)
        vv_sw = pltpu.roll(vv, SWA_HEAD_DIM, 1)
        for g in range(SWA_KV_HEADS):
            k_lo, k_hi = (kk, kk_sw) if g == 0 else (kk_sw, kk)
            v_lo, v_hi = (vv, vv_sw) if g == 0 else (vv_sw, vv)
            kz_ref[2 * g + 0, rows, :] = jnp.where(low, k_lo, 0.0).astype(BF16)
            kz_ref[2 * g + 1, rows, :] = jnp.where(low, 0.0, k_hi).astype(BF16)
            vz_ref[2 * g + 0, rows, 0:128] = jnp.where(low, v_lo, 0.0).astype(BF16)
            vz_ref[2 * g + 1, rows, 0:128] = jnp.where(low, 0.0, v_hi).astype(BF16)

    def sub_block(i, carry):
        r0 = pl.multiple_of(i * SWA_BLOCK, SWA_BLOCK)
        qrows = pl.ds(r0, SWA_BLOCK)
        krows = pl.ds(r0, 2 * SWA_BLOCK)
        bias_off = jnp.where((t == 0) & (i == 0), SWA_HEADS, 0)
        for pair in range(SWA_HEADS // 2):
            g = (2 * pair) // SWA_GROUP
            qp = q_ref[qrows, pair * 128:(pair + 1) * 128]
            o_pair = None
            for half in range(2):
                h = 2 * pair + half
                slot = 2 * g + half
                s2 = lax.dot_general(qp, kz_ref[slot, krows, :], _NT, preferred_element_type=F32)
                s = jnp.where(tri, s2[:, SWA_BLOCK:], s2[:, :SWA_BLOCK]) + bias_ref[bias_off + h]
                sink = sink_ref[h]
                m = jnp.maximum(jnp.max(s, axis=-1, keepdims=True), sink)
                p = jnp.exp(s - m)
                p2 = jnp.concatenate([jnp.where(tri, 0.0, p), jnp.where(tri, p, 0.0)], axis=1).astype(BF16)
                od = _dot(p2, vz_ref[slot, krows, :])
                den = od[:, 128:] + jnp.exp(sink - m)
                o_h = od[:, :128] * (1.0 / den)
                o_pair = o_h if o_pair is None else o_pair + o_h
            o_ref[qrows, pair * 128:(pair + 1) * 128] = o_pair.astype(BF16)
        return carry

    lax.fori_loop(0, SWA_NQ, sub_block, 0, unroll=2)


def _swa(z, rel_bias, sinks, batch, seq):
    nt = seq // SWA_TQ
    bucket = jnp.asarray(_swa_bucket_table())
    kv_col = Z_KV // 256
    smem = pl.BlockSpec(memory_space=pltpu.SMEM)
    return pl.pallas_call(
        _swa_kernel,
        out_shape=jax.ShapeDtypeStruct((batch * seq, D_MODEL), BF16),
        grid=(batch, nt),
        in_specs=[smem, smem, _resident(bucket.shape),
                  pl.BlockSpec((SWA_TQ, D_MODEL), lambda b, t: (b * nt + t, Z_SQ // D_MODEL)),
                  pl.BlockSpec((SWA_TQ, 256), lambda b, t: (b * nt + t, kv_col)),
                  pl.BlockSpec((SWA_BLOCK, 256),
                               lambda b, t: ((b * nt + t) * SWA_NQ - jnp.minimum(t, 1), kv_col))],
        out_specs=pl.BlockSpec((SWA_TQ, D_MODEL), lambda b, t: (b * nt + t, 0)),
        scratch_shapes=[pltpu.VMEM((2 * SWA_HEADS, SWA_BLOCK, SWA_BLOCK), F32),
                        pltpu.VMEM((2 * SWA_KV_HEADS, SWA_TQ + SWA_BLOCK, 128), BF16),
                        pltpu.VMEM((2 * SWA_KV_HEADS, SWA_TQ + SWA_BLOCK, 256), BF16)],
        compiler_params=pltpu.CompilerParams(
            dimension_semantics=("arbitrary", "arbitrary"), vmem_limit_bytes=VMEM_LIMIT_BYTES),
        name="swa",
    )(rel_bias, sinks, bucket, z, z, z)


def _mem_kv_kernel(mem_ref, w_ref, kv_ref):
    mem_w = MEM_HEADS * MEM_HEAD_DIM
    mb = mem_ref[...].astype(BF16)
    kv_ref[:, :mem_w] = (_dot(mb, w_ref[:, :mem_w].astype(BF16)) * MEM_HEAD_DIM ** -0.5).astype(BF16)
    kv_ref[:, mem_w:] = _dot(mb, w_ref[:, mem_w:].astype(BF16)).astype(BF16)


def _mem_kv(mem2d, w_b):
    return pl.pallas_call(
        _mem_kv_kernel,
        out_shape=jax.ShapeDtypeStruct((mem2d.shape[0], w_b.shape[1]), BF16),
        compiler_params=pltpu.CompilerParams(vmem_limit_bytes=VMEM_LIMIT_BYTES),
        name="mem_kv",
    )(mem2d, w_b)


TAIL_TM = 512
TAIL_SUB = 256
TAIL_TF = 1024


def _layer_norm(y, g, b):
    mu = jnp.mean(y, axis=-1, keepdims=True)
    yc = y - mu
    var = jnp.mean(yc * yc, axis=-1, keepdims=True)
    return yc * lax.rsqrt(var + LN_EPS) * g + b


def _tail_kernel(alpha, x_ref, oa_ref, ob_ref, mq_ref, gl_ref, kv_ref, wa_ref, wb_ref, wc_ref, wo_ref,
                 g1_ref, b1_ref, wu_ref, wd_ref, g2_ref, b2_ref, o_ref):
    subs = [slice(r, r + TAIL_SUB) for r in range(0, TAIL_TM, TAIL_SUB)]
    mem_w = MEM_HEADS * MEM_HEAD_DIM
    hcols = [slice(h * MEM_HEAD_DIM, (h + 1) * MEM_HEAD_DIM) for h in range(MEM_HEADS)]

    def gate(rows, i):
        return jax.nn.sigmoid(gl_ref[rows, i * D_MODEL:(i + 1) * D_MODEL].astype(F32))

    scores = [[lax.dot_general(mq_ref[rows, c], kv_ref[:, c], _NT, preferred_element_type=F32) for c in hcols]
              for rows in subs]
    merged = [gate(rows, 0) * _dot(oa_ref[rows, :], wa_ref[...]) for rows in subs]
    probs = []
    for s_heads in scores:
        ps = []
        for s in s_heads:
            p = jnp.exp(s - jnp.max(s, axis=-1, keepdims=True))
            ps.append((p * (1.0 / jnp.sum(p, axis=-1, keepdims=True))).astype(BF16))
        probs.append(ps)
    o_c = [jnp.concatenate([_dot(p, kv_ref[:, mem_w + h * MEM_HEAD_DIM:mem_w + (h + 1) * MEM_HEAD_DIM]).astype(BF16)
                            for h, p in enumerate(ps)], axis=1) for ps in probs]
    merged = [m + gate(rows, 1) * _dot(ob_ref[rows, :], wb_ref[...]) for m, rows in zip(merged, subs)]
    merged = [m + gate(rows, 2) * _dot(oc, wc_ref[...]) for m, rows, oc in zip(merged, subs, o_c)]

    h1 = []
    for m, rows in zip(merged, subs):
        mix = _dot(m.astype(BF16), wo_ref[...])
        h1.append(_layer_norm(alpha * x_ref[rows, :] + mix, g1_ref[...], b1_ref[...]))

    hb = [h.astype(BF16) for h in h1]
    ff = [None] * len(subs)
    for f0 in range(0, D_FF, TAIL_TF):
        us = [jnp.maximum(_dot(hb[i], wu_ref[:, f0:f0 + TAIL_TF]), 0.0) for i in range(len(subs))]
        for i, u in enumerate(us):
            d = _dot((u * u).astype(BF16), wd_ref[f0:f0 + TAIL_TF, :])
            ff[i] = d if ff[i] is None else ff[i] + d
    for i, rows in enumerate(subs):
        o_ref[rows, :] = _layer_norm(alpha * h1[i] + ff[i], g2_ref[...], b2_ref[...])


def _tail(alpha, x2d, o_a, o_b, z, kv, wa, wb, wc, wo, ln1_g, ln1_b, wu, wd, ln2_g, ln2_b, batch, seq):
    t = x2d.shape[0]
    per_batch = seq // TAIL_TM
    row = lambda i: (i, 0)
    weights = (wa, wb, wc, wo, ln1_g, ln1_b, wu, wd, ln2_g, ln2_b)
    return pl.pallas_call(
        functools.partial(_tail_kernel, alpha),
        out_shape=jax.ShapeDtypeStruct((t, D_MODEL), F32),
        grid=(t // TAIL_TM,),
        in_specs=[pl.BlockSpec((TAIL_TM, D_MODEL), row),
                  pl.BlockSpec((TAIL_TM, D_MODEL), row),
                  pl.BlockSpec((TAIL_TM, D_MODEL), row),
                  pl.BlockSpec((TAIL_TM, D_MODEL), lambda i: (i, Z_MQ // D_MODEL)),
                  pl.BlockSpec((TAIL_TM, 3 * D_MODEL), lambda i: (i, Z_GL // (3 * D_MODEL))),
                  pl.BlockSpec((kv.shape[0] // batch, kv.shape[1]), lambda i: (i // per_batch, 0))]
                 + [_resident(w.shape) for w in weights],
        out_specs=pl.BlockSpec((TAIL_TM, D_MODEL), row),
        compiler_params=pltpu.CompilerParams(
            dimension_semantics=("arbitrary",), vmem_limit_bytes=VMEM_LIMIT_BYTES),
        name="tail",
    )(x2d, o_a, o_b, z, z, kv, *weights)


def kernel(x, mem, w_in, lb_logits, hg_norm_gain, swa_sinks, rel_bias, w_mem_kv, w_branch_hg, w_branch_swa,
           w_branch_mem, w_out, ln1_g, ln1_b, w_up, w_down, ln2_g, ln2_b):
    batch, seq, _ = x.shape
    depth = w_in.shape[0]
    assert depth == 1 and lb_logits.shape[0] == depth + 1
    alpha = (2.0 * depth) ** 0.25
    out_dtype = x.dtype

    w_in_b = w_in[0].astype(BF16)
    x2d = x.reshape(batch * seq, D_MODEL).astype(F32)
    later = [w[0].astype(F32) for w in (w_branch_hg, w_branch_swa, w_branch_mem, w_out, w_up, w_down)]
    z, z_lo, z_k, (wa, wb, wc, wo, wu, wd) = _proj_in(
        x2d, w_in_b, lb_logits.astype(F32), hg_norm_gain.astype(F32), later)
    o_a = _hgrn(z, z_lo, z_k, batch, seq)
    o_b = _swa(z, rel_bias.astype(F32), swa_sinks[0].astype(F32), batch, seq)
    kv = _mem_kv(mem.reshape(batch * mem.shape[1], D_MODEL).astype(F32), w_mem_kv[0].astype(F32))
    h2 = _tail(alpha, x2d, o_a, o_b, z, kv, wa, wb, wc, wo, ln1_g.astype(F32), ln1_b.astype(F32),
               wu, wd, ln2_g.astype(F32), ln2_b.astype(F32), batch, seq)
    return h2.reshape(batch, seq, D_MODEL).astype(out_dtype)
```

```python
import functools
import math

import numpy as np
import jax
import jax.numpy as jnp
from jax import lax
from jax.experimental import pallas as pl
from jax.experimental.pallas import tpu as pltpu

F32 = jnp.float32
BF16 = jnp.bfloat16

D_MODEL = 1024
HG_HEADS = 8
HG_DK = 128
HG_DV = 128
HG_CHUNK = 64
SWA_HEADS = 16
SWA_KV_HEADS = 2
SWA_GROUP = SWA_HEADS // SWA_KV_HEADS
SWA_HEAD_DIM = 64
SWA_BLOCK = 128
MEM_HEADS = 4
MEM_HEAD_DIM = 256
NUM_BUCKETS = 32
MAX_DISTANCE = 128
D_FF = 4 * D_MODEL
LN_EPS = 1e-5
RMS_EPS = 1e-6
LOG2E = 1.4426950408889634

Z_HQ, Z_HF, Z_HI, Z_HG, Z_SQ, Z_MQ = (i * D_MODEL for i in range(6))
Z_GL = 6 * D_MODEL
Z_KV = 9 * D_MODEL
Z_COLS = Z_KV + 2 * SWA_KV_HEADS * SWA_HEAD_DIM

VMEM_LIMIT_BYTES = 56 * 1024 * 1024

_NT = (((1,), (1,)), ((), ()))
_TN = (((0,), (0,)), ((), ()))


def _dot(a, b):
    return jnp.dot(a, b, preferred_element_type=F32)


def _resident(shape):
    nd = len(shape)
    return pl.BlockSpec(shape, lambda *_: (0,) * nd, pipeline_mode=pl.Buffered(1))


PROJ_TM = 512
PROJ_TN = 1024


def _proj_col_chunks():
    kv_w = 2 * SWA_KV_HEADS * SWA_HEAD_DIM
    src_kv = Z_MQ
    segments = ((0, 0, Z_MQ), (src_kv + kv_w, Z_MQ, Z_KV - Z_MQ), (src_kv, Z_KV, kv_w))
    chunks = []
    for src, dst, width in segments:
        for off in range(0, width, PROJ_TN):
            chunks.append((src + off, dst + off, min(PROJ_TN, width - off)))
    return chunks


def _proj_in_kernel(n_cast, x_ref, w_ref, lbl_ref, gain_ref, *refs):
    cast_in, (z_ref, lo_ref, k_ref), cast_out = refs[:n_cast], refs[n_cast:n_cast + 3], refs[n_cast + 3:]
    l0 = lbl_ref[0:1, :]
    l1 = lbl_ref[1:2, :]
    lmax = jnp.maximum(l0, l1)
    e0 = jnp.exp(l0 - lmax)
    e1 = jnp.exp(l1 - lmax)
    lb = e0 / (e0 + e1)
    f_mid = 0.5 * (1.0 + lb)
    f_amp = 0.5 * (1.0 - lb)
    gain_s = gain_ref[...] * (HG_DV ** 0.5)

    xb = x_ref[...].astype(BF16)
    for src, dst, width in _proj_col_chunks():
        res = _dot(xb, w_ref[:, src:src + width])
        if dst == Z_HF:
            f = f_mid + f_amp * jnp.tanh(0.5 * res)
            logf = jnp.log(f)
            hi = logf.astype(BF16)
            z_ref[:, dst:dst + width] = hi
            lo_ref[...] = (logf - hi.astype(F32)).astype(BF16)
            k_ref[...] = 1.0 - f
        elif dst == Z_HG:
            hg = 0.5 * res
            z_ref[:, dst:dst + width] = (gain_s * (hg + hg * jnp.tanh(hg))).astype(BF16)
        else:
            z_ref[:, dst:dst + width] = res.astype(BF16)
    for src_ref, dst_ref in zip(cast_in, cast_out):
        dst_ref[...] = src_ref[...].astype(BF16)


def _proj_in(x2d, w_in_b, lb_logits, gain, later_weights):
    t = x2d.shape[0]
    steps = t // PROJ_TM
    slab = lambda w: pl.BlockSpec((w.shape[0] // steps, w.shape[1]), lambda i: (i, 0))
    row = lambda width: pl.BlockSpec((PROJ_TM, width), lambda i: (i, 0))
    outs = pl.pallas_call(
        functools.partial(_proj_in_kernel, len(later_weights)),
        out_shape=[jax.ShapeDtypeStruct((t, Z_COLS), BF16), jax.ShapeDtypeStruct((t, D_MODEL), BF16),
                   jax.ShapeDtypeStruct((t, D_MODEL), F32)]
                  + [jax.ShapeDtypeStruct(w.shape, BF16) for w in later_weights],
        grid=(steps,),
        in_specs=[row(D_MODEL), _resident((D_MODEL, Z_COLS)), _resident(lb_logits.shape), _resident(gain.shape)]
                 + [slab(w) for w in later_weights],
        out_specs=[row(Z_COLS), row(D_MODEL), row(D_MODEL)] + [slab(w) for w in later_weights],
        compiler_params=pltpu.CompilerParams(
            dimension_semantics=("arbitrary",), vmem_limit_bytes=VMEM_LIMIT_BYTES),
        name="proj_in",
    )(x2d, w_in_b, lb_logits, gain, *later_weights)
    return outs[0], outs[1], outs[2], outs[3:]


HG_TB = 512
HG_GROUP = 2


def _hgrn_kernel(tril_ref, zq_ref, zhi_ref, zlo_ref, zk_ref, zi_ref, zg_ref, o_ref, st_ref, stb_ref):
    @pl.when(pl.program_id(1) == 0)
    def _():
        st_ref[...] = jnp.zeros_like(st_ref)
        stb_ref[...] = jnp.zeros_like(stb_ref)

    tril2 = tril_ref[...]
    causal = tril2[:, :HG_CHUNK] > 0
    heads = range(HG_HEADS)
    hcols = [slice(h * HG_DK, (h + 1) * HG_DK) for h in heads]

    def chunk_group(g, carry):
        rows_of = [pl.ds(pl.multiple_of((g * HG_GROUP + j) * HG_CHUNK, HG_CHUNK), HG_CHUNK)
                   for j in range(HG_GROUP)]
        units = [(rows, h) for rows in rows_of for h in heads]

        bs = [_dot(tril2, jnp.concatenate([zhi_ref[rows, hcols[h]], zlo_ref[rows, hcols[h]]], axis=0))
              for rows, h in units]

        attns, upds, qins, ebls = [], [], [], []
        for (rows, h), b in zip(units, bs):
            ebl = jnp.exp(b[HG_CHUNK - 1:HG_CHUNK, :])
            eb = jnp.exp2(b * LOG2E)
            q_in = (zq_ref[rows, hcols[h]].astype(F32) * eb).astype(BF16)
            kin = zk_ref[rows, hcols[h]] * (1.0 / eb)
            k_in = kin.astype(BF16)
            k_out = (kin * ebl).astype(BF16)
            attns.append(lax.dot_general(q_in, k_in, _NT, preferred_element_type=F32))
            upds.append(lax.dot_general(zi_ref[rows, hcols[h]], k_out, _TN, preferred_element_type=F32))
            qins.append(q_in)
            ebls.append(ebl)

        outs = []
        for u, (rows, h) in enumerate(units):
            attn = jnp.where(causal, attns[u], 0.0).astype(BF16)
            lhs = jnp.concatenate([qins[u], attn], axis=1)
            rhs = jnp.concatenate([stb_ref[h], zi_ref[rows, hcols[h]]], axis=0)
            outs.append(_dot(lhs, rhs))
            st = st_ref[h] * ebls[u] + upds[u]
            st_ref[h] = st
            stb_ref[h] = st.astype(BF16).T

        for (rows, h), o in zip(units, outs):
            ss = jnp.sum(o * o, axis=-1, keepdims=True)
            gate = zg_ref[rows, hcols[h]].astype(F32)
            o_ref[rows, hcols[h]] = (o * lax.rsqrt(ss + HG_DV * RMS_EPS) * gate).astype(BF16)
        return carry

    lax.fori_loop(0, HG_TB // (HG_CHUNK * HG_GROUP), chunk_group, 0, unroll=2)


def _hgrn(z, z_lo, z_k, batch, seq):
    nt = seq // HG_TB
    tril = np.tril(np.ones((HG_CHUNK, HG_CHUNK), np.float32))
    tril = jnp.asarray(np.concatenate([tril, tril], axis=1), BF16)
    tile = lambda b, t: (b * nt + t, 0)

    def zspec(col):
        return pl.BlockSpec((HG_TB, D_MODEL), lambda b, t, col=col: (b * nt + t, col // D_MODEL))

    return pl.pallas_call(
        _hgrn_kernel,
        out_shape=jax.ShapeDtypeStruct((batch * seq, D_MODEL), BF16),
        grid=(batch, nt),
        in_specs=[_resident(tril.shape), zspec(Z_HQ), zspec(Z_HF),
                  pl.BlockSpec((HG_TB, D_MODEL), tile), pl.BlockSpec((HG_TB, D_MODEL), tile),
                  zspec(Z_HI), zspec(Z_HG)],
        out_specs=pl.BlockSpec((HG_TB, D_MODEL), tile),
        scratch_shapes=[pltpu.VMEM((HG_HEADS, HG_DV, HG_DK), F32),
                        pltpu.VMEM((HG_HEADS, HG_DV, HG_DK), BF16)],
        compiler_params=pltpu.CompilerParams(
            dimension_semantics=("arbitrary", "arbitrary"), vmem_limit_bytes=VMEM_LIMIT_BYTES),
        name="hgrn",
    )(tril, z, z, z_lo, z_k, z, z)


def _swa_bucket_table():
    qi = np.arange(SWA_BLOCK)[:, None]
    kj = np.arange(SWA_BLOCK)[None, :]
    n = np.where(kj <= qi, qi - kj, qi + SWA_BLOCK - kj).astype(np.int32)
    max_exact = NUM_BUCKETS // 2
    nf = np.maximum(n, 1).astype(np.float32)
    large = max_exact + (np.log(nf / np.float32(max_exact)) / np.float32(math.log(MAX_DISTANCE / max_exact))
                         * np.float32(NUM_BUCKETS - max_exact)).astype(np.int32)
    large = np.minimum(large, NUM_BUCKETS - 1)
    return np.where(n < max_exact, n, large).astype(np.int32)


SWA_TQ = 512
SWA_NQ = SWA_TQ // SWA_BLOCK
SWA_MASKED = -1e30


def _swa_kernel(relb_ref, sink_ref, bucket_ref, q_ref, kvc_ref, kvp_ref, o_ref, bias_ref, kz_ref, vz_ref):
    t = pl.program_id(1)
    qi = lax.broadcasted_iota(jnp.int32, (SWA_BLOCK, SWA_BLOCK), 0)
    kj = lax.broadcasted_iota(jnp.int32, (SWA_BLOCK, SWA_BLOCK), 1)
    tri = kj <= qi

    @pl.when((pl.program_id(0) == 0) & (t == 0))
    def _():
        bucket = bucket_ref[...]
        for h in range(SWA_HEADS):
            acc = jnp.zeros((SWA_BLOCK, SWA_BLOCK), F32)
            for bk in range(NUM_BUCKETS):
                acc = jnp.where(bucket == bk, relb_ref[bk, h], acc)
            bias_ref[h] = acc
            bias_ref[SWA_HEADS + h] = jnp.where(tri, acc, SWA_MASKED)
        vz_ref[:, :, 128:256] = jnp.ones((2 * SWA_KV_HEADS, SWA_TQ + SWA_BLOCK, 128), BF16)

    lane = lax.broadcasted_iota(jnp.int32, (SWA_BLOCK, 2 * SWA_HEAD_DIM), 1)
    low = lane < SWA_HEAD_DIM
    scale = SWA_HEAD_DIM ** -0.5
    for blk in range(SWA_NQ + 1):
        rows = slice(blk * SWA_BLOCK, (blk + 1) * SWA_BLOCK)
        if blk == 0:
            kk = kvp_ref[:, 0:128].astype(F32) * scale
            vv = kvp_ref[:, 128:256].astype(F32)
        else:
            src = slice((blk - 1) * SWA_BLOCK, blk * SWA_BLOCK)
            kk = kvc_ref[src, 0:128].astype(F32) * scale
            vv = kvc_ref[src, 128:256].astype(F32)
        kk_sw = pltpu.roll(kk, SWA_HEAD_DIM, 1)
        vv_sw = pltpu.roll(vv, SWA_HEAD_DIM, 1)
        for g in range(SWA_KV_HEADS):
            k_lo, k_hi = (kk, kk_sw) if g == 0 else (kk_sw, kk)
            v_lo, v_hi = (vv, vv_sw) if g == 0 else (vv_sw, vv)
            kz_ref[2 * g + 0, rows, :] = jnp.where(low, k_lo, 0.0).astype(BF16)
            kz_ref[2 * g + 1, rows, :] = jnp.where(low, 0.0, k_hi).astype(BF16)
            vz_ref[2 * g + 0, rows, 0:128] = jnp.where(low, v_lo, 0.0).astype(BF16)
            vz_ref[2 * g + 1, rows, 0:128] = jnp.where(low, 0.0, v_hi).astype(BF16)

    def sub_block(i, carry):
        r0 = pl.multiple_of(i * SWA_BLOCK, SWA_BLOCK)
        qrows = pl.ds(r0, SWA_BLOCK)
        krows = pl.ds(r0, 2 * SWA_BLOCK)
        bias_off = jnp.where((t == 0) & (i == 0), SWA_HEADS, 0)
        for pair in range(SWA_HEADS // 2):
            g = (2 * pair) // SWA_GROUP
            qp = q_ref[qrows, pair * 128:(pair + 1) * 128]
            o_pair = None
            for half in range(2):
                h = 2 * pair + half
                slot = 2 * g + half
                s2 = lax.dot_general(qp, kz_ref[slot, krows, :], _NT, preferred_element_type=F32)
                s = jnp.where(tri, s2[:, SWA_BLOCK:], s2[:, :SWA_BLOCK]) + bias_ref[bias_off + h]
                sink = sink_ref[h]
                m = jnp.maximum(jnp.max(s, axis=-1, keepdims=True), sink)
                p = jnp.exp(s - m)
                p2 = jnp.concatenate([jnp.where(tri, 0.0, p), jnp.where(tri, p, 0.0)], axis=1).astype(BF16)
                od = _dot(p2, vz_ref[slot, krows, :])
                den = od[:, 128:] + jnp.exp(sink - m)
                o_h = od[:, :128] * (1.0 / den)
                o_pair = o_h if o_pair is None else o_pair + o_h
            o_ref[qrows, pair * 128:(pair + 1) * 128] = o_pair.astype(BF16)
        return carry

    lax.fori_loop(0, SWA_NQ, sub_block, 0, unroll=2)


def _swa(z, rel_bias, sinks, batch, seq):
    nt = seq // SWA_TQ
    bucket = jnp.asarray(_swa_bucket_table())
    kv_col = Z_KV // 256
    smem = pl.BlockSpec(memory_space=pltpu.SMEM)
    return pl.pallas_call(
        _swa_kernel,
        out_shape=jax.ShapeDtypeStruct((batch * seq, D_MODEL), BF16),
        grid=(batch, nt),
        in_specs=[smem, smem, _resident(bucket.shape),
                  pl.BlockSpec((SWA_TQ, D_MODEL), lambda b, t: (b * nt + t, Z_SQ // D_MODEL)),
                  pl.BlockSpec((SWA_TQ, 256), lambda b, t: (b * nt + t, kv_col)),
                  pl.BlockSpec((SWA_BLOCK, 256),
                               lambda b, t: ((b * nt + t) * SWA_NQ - jnp.minimum(t, 1), kv_col))],
        out_specs=pl.BlockSpec((SWA_TQ, D_MODEL), lambda b, t: (b * nt + t, 0)),
        scratch_shapes=[pltpu.VMEM((2 * SWA_HEADS, SWA_BLOCK, SWA_BLOCK), F32),
                        pltpu.VMEM((2 * SWA_KV_HEADS, SWA_TQ + SWA_BLOCK, 128), BF16),
                        pltpu.VMEM((2 * SWA_KV_HEADS, SWA_TQ + SWA_BLOCK, 256), BF16)],
        compiler_params=pltpu.CompilerParams(
            dimension_semantics=("arbitrary", "arbitrary"), vmem_limit_bytes=VMEM_LIMIT_BYTES),
        name="swa",
    )(rel_bias, sinks, bucket, z, z, z)


def _mem_kv_kernel(mem_ref, w_ref, kv_ref):
    mem_w = MEM_HEADS * MEM_HEAD_DIM
    mb = mem_ref[...].astype(BF16)
    kv_ref[:, :mem_w] = (_dot(mb, w_ref[:, :mem_w].astype(BF16)) * MEM_HEAD_DIM ** -0.5).astype(BF16)
    kv_ref[:, mem_w:] = _dot(mb, w_ref[:, mem_w:].astype(BF16)).astype(BF16)


def _mem_kv(mem2d, w_b):
    return pl.pallas_call(
        _mem_kv_kernel,
        out_shape=jax.ShapeDtypeStruct((mem2d.shape[0], w_b.shape[1]), BF16),
        compiler_params=pltpu.CompilerParams(vmem_limit_bytes=VMEM_LIMIT_BYTES),
        name="mem_kv",
    )(mem2d, w_b)


TAIL_TM = 512
TAIL_SUB = 256
TAIL_TF = 1024


def _layer_norm(y, g, b):
    mu = jnp.mean(y, axis=-1, keepdims=True)
    yc = y - mu
    var = jnp.mean(yc * yc, axis=-1, keepdims=True)
    return yc * lax.rsqrt(var + LN_EPS) * g + b


def _tail_kernel(alpha, x_ref, oa_ref, ob_ref, mq_ref, gl_ref, kv_ref, wa_ref, wb_ref, wc_ref, wo_ref,
                 g1_ref, b1_ref, wu_ref, wd_ref, g2_ref, b2_ref, o_ref):
    subs = [slice(r, r + TAIL_SUB) for r in range(0, TAIL_TM, TAIL_SUB)]
    mem_w = MEM_HEADS * MEM_HEAD_DIM
    hcols = [slice(h * MEM_HEAD_DIM, (h + 1) * MEM_HEAD_DIM) for h in range(MEM_HEADS)]

    def gate(rows, i):
        return jax.nn.sigmoid(gl_ref[rows, i * D_MODEL:(i + 1) * D_MODEL].astype(F32))

    scores = [[lax.dot_general(mq_ref[rows, c], kv_ref[:, c], _NT, preferred_element_type=F32) for c in hcols]
              for rows in subs]
    merged = [gate(rows, 0) * _dot(oa_ref[rows, :], wa_ref[...]) for rows in subs]
    probs = []
    for s_heads in scores:
        ps = []
        for s in s_heads:
            p = jnp.exp(s - jnp.max(s, axis=-1, keepdims=True))
            ps.append((p * (1.0 / jnp.sum(p, axis=-1, keepdims=True))).astype(BF16))
        probs.append(ps)
    o_c = [jnp.concatenate([_dot(p, kv_ref[:, mem_w + h * MEM_HEAD_DIM:mem_w + (h + 1) * MEM_HEAD_DIM]).astype(BF16)
                            for h, p in enumerate(ps)], axis=1) for ps in probs]
    merged = [m + gate(rows, 1) * _dot(ob_ref[rows, :], wb_ref[...]) for m, rows in zip(merged, subs)]
    merged = [m + gate(rows, 2) * _dot(oc, wc_ref[...]) for m, rows, oc in zip(merged, subs, o_c)]

    h1 = []
    for m, rows in zip(merged, subs):
        mix = _dot(m.astype(BF16), wo_ref[...])
        h1.append(_layer_norm(alpha * x_ref[rows, :] + mix, g1_ref[...], b1_ref[...]))

    hb = [h.astype(BF16) for h in h1]
    ff = [None] * len(subs)
    for f0 in range(0, D_FF, TAIL_TF):
        us = [jnp.maximum(_dot(hb[i], wu_ref[:, f0:f0 + TAIL_TF]), 0.0) for i in range(len(subs))]
        for i, u in enumerate(us):
            d = _dot((u * u).astype(BF16), wd_ref[f0:f0 + TAIL_TF, :])
            ff[i] = d if ff[i] is None else ff[i] + d
    for i, rows in enumerate(subs):
        o_ref[rows, :] = _layer_norm(alpha * h1[i] + ff[i], g2_ref[...], b2_ref[...])


def _tail(alpha, x2d, o_a, o_b, z, kv, wa, wb, wc, wo, ln1_g, ln1_b, wu, wd, ln2_g, ln2_b, batch, seq):
    t = x2d.shape[0]
    per_batch = seq // TAIL_TM
    row = lambda i: (i, 0)
    weights = (wa, wb, wc, wo, ln1_g, ln1_b, wu, wd, ln2_g, ln2_b)
    return pl.pallas_call(
        functools.partial(_tail_kernel, alpha),
        out_shape=jax.ShapeDtypeStruct((t, D_MODEL), F32),
        grid=(t // TAIL_TM,),
        in_specs=[pl.BlockSpec((TAIL_TM, D_MODEL), row),
                  pl.BlockSpec((TAIL_TM, D_MODEL), row),
                  pl.BlockSpec((TAIL_TM, D_MODEL), row),
                  pl.BlockSpec((TAIL_TM, D_MODEL), lambda i: (i, Z_MQ // D_MODEL)),
                  pl.BlockSpec((TAIL_TM, 3 * D_MODEL), lambda i: (i, Z_GL // (3 * D_MODEL))),
                  pl.BlockSpec((kv.shape[0] // batch, kv.shape[1]), lambda i: (i // per_batch, 0))]
                 + [_resident(w.shape) for w in weights],
        out_specs=pl.BlockSpec((TAIL_TM, D_MODEL), row),
        compiler_params=pltpu.CompilerParams(
            dimension_semantics=("arbitrary",), vmem_limit_bytes=VMEM_LIMIT_BYTES),
        name="tail",
    )(x2d, o_a, o_b, z, z, kv, *weights)


def kernel(x, mem, w_in, lb_logits, hg_norm_gain, swa_sinks, rel_bias, w_mem_kv, w_branch_hg, w_branch_swa,
           w_branch_mem, w_out, ln1_g, ln1_b, w_up, w_down, ln2_g, ln2_b):
    batch, seq, _ = x.shape
    depth = w_in.shape[0]
    assert depth == 1 and lb_logits.shape[0] == depth + 1
    alpha = (2.0 * depth) ** 0.25
    out_dtype = x.dtype

    w_in_b = w_in[0].astype(BF16)
    x2d = x.reshape(batch * seq, D_MODEL).astype(F32)
    later = [w[0].astype(F32) for w in (w_branch_hg, w_branch_swa, w_branch_mem, w_out, w_up, w_down)]
    z, z_lo, z_k, (wa, wb, wc, wo, wu, wd) = _proj_in(
        x2d, w_in_b, lb_logits.astype(F32), hg_norm_gain.astype(F32), later)
    o_a = _hgrn(z, z_lo, z_k, batch, seq)
    o_b = _swa(z, rel_bias.astype(F32), swa_sinks[0].astype(F32), batch, seq)
    kv = _mem_kv(mem.reshape(batch * mem.shape[1], D_MODEL).astype(F32), w_mem_kv[0].astype(F32))
    h2 = _tail(alpha, x2d, o_a, o_b, z, kv, wa, wb, wc, wo, ln1_g.astype(F32), ln1_b.astype(F32),
               wu, wd, ln2_g.astype(F32), ln2_b.astype(F32), batch, seq)
    return h2.reshape(batch, seq, D_MODEL).astype(out_dtype)
```

```python
import functools
import math

import numpy as np
import jax
import jax.numpy as jnp
from jax import lax
from jax.experimental import pallas as pl
from jax.experimental.pallas import tpu as pltpu

F32 = jnp.float32
BF16 = jnp.bfloat16

D_MODEL = 1024
HG_HEADS = 8
HG_DK = 128
HG_DV = 128
HG_CHUNK = 64
SWA_HEADS = 16
SWA_KV_HEADS = 2
SWA_GROUP = SWA_HEADS // SWA_KV_HEADS
SWA_HEAD_DIM = 64
SWA_BLOCK = 128
MEM_HEADS = 4
MEM_HEAD_DIM = 256
NUM_BUCKETS = 32
MAX_DISTANCE = 128
D_FF = 4 * D_MODEL
LN_EPS = 1e-5
RMS_EPS = 1e-6
LOG2E = 1.4426950408889634

Z_HQ, Z_HF, Z_HI, Z_HG, Z_SQ, Z_MQ = (i * D_MODEL for i in range(6))
Z_GL = 6 * D_MODEL
Z_KV = 9 * D_MODEL
Z_COLS = Z_KV + 2 * SWA_KV_HEADS * SWA_HEAD_DIM

VMEM_LIMIT_BYTES = 56 * 1024 * 1024

_NT = (((1,), (1,)), ((), ()))
_TN = (((0,), (0,)), ((), ()))


def _dot(a, b):
    return jnp.dot(a, b, preferred_element_type=F32)


def _resident(shape):
    nd = len(shape)
    return pl.BlockSpec(shape, lambda *_: (0,) * nd, pipeline_mode=pl.Buffered(1))


PROJ_TM = 512
PROJ_TN = 1024


def _proj_col_chunks():
    kv_w = 2 * SWA_KV_HEADS * SWA_HEAD_DIM
    src_kv = Z_MQ
    segments = ((0, 0, Z_MQ), (src_kv + kv_w, Z_MQ, Z_KV - Z_MQ), (src_kv, Z_KV, kv_w))
    chunks = []
    for src, dst, width in segments:
        for off in range(0, width, PROJ_TN):
            chunks.append((src + off, dst + off, min(PROJ_TN, width - off)))
    return chunks


def _proj_in_kernel(x_ref, w_ref, lbl_ref, gain_ref, z_ref, lo_ref, k_ref):
    l0 = lbl_ref[0:1, :]
    l1 = lbl_ref[1:2, :]
    lmax = jnp.maximum(l0, l1)
    e0 = jnp.exp(l0 - lmax)
    e1 = jnp.exp(l1 - lmax)
    lb = e0 / (e0 + e1)
    f_mid = 0.5 * (1.0 + lb)
    f_amp = 0.5 * (1.0 - lb)
    gain_s = gain_ref[...] * (HG_DV ** 0.5)

    xb = x_ref[...].astype(BF16)
    for src, dst, width in _proj_col_chunks():
        res = _dot(xb, w_ref[:, src:src + width])
        if dst == Z_HF:
            f = f_mid + f_amp * jnp.tanh(0.5 * res)
            logf = jnp.log(f)
            hi = logf.astype(BF16)
            z_ref[:, dst:dst + width] = hi
            lo_ref[...] = (logf - hi.astype(F32)).astype(BF16)
            k_ref[...] = 1.0 - f
        elif dst == Z_HG:
            hg = 0.5 * res
            z_ref[:, dst:dst + width] = (gain_s * (hg + hg * jnp.tanh(hg))).astype(BF16)
        else:
            z_ref[:, dst:dst + width] = res.astype(BF16)


def _proj_in(x2d, w_in_b, lb_logits, gain):
    t = x2d.shape[0]
    row = lambda width: pl.BlockSpec((PROJ_TM, width), lambda i: (i, 0))
    return pl.pallas_call(
        _proj_in_kernel,
        out_shape=[jax.ShapeDtypeStruct((t, Z_COLS), BF16), jax.ShapeDtypeStruct((t, D_MODEL), BF16),
                   jax.ShapeDtypeStruct((t, D_MODEL), F32)],
        grid=(t // PROJ_TM,),
        in_specs=[row(D_MODEL), _resident((D_MODEL, Z_COLS)), _resident(lb_logits.shape), _resident(gain.shape)],
        out_specs=[row(Z_COLS), row(D_MODEL), row(D_MODEL)],
        compiler_params=pltpu.CompilerParams(
            dimension_semantics=("arbitrary",), vmem_limit_bytes=VMEM_LIMIT_BYTES),
        name="proj_in",
    )(x2d, w_in_b, lb_logits, gain)


HG_TB = 1024
HG_GROUP = 2


def _hgrn_kernel(tril_ref, zq_ref, zhi_ref, zlo_ref, zk_ref, zi_ref, zg_ref, o_ref, st_ref, stb_ref):
    @pl.when(pl.program_id(1) == 0)
    def _():
        st_ref[...] = jnp.zeros_like(st_ref)
        stb_ref[...] = jnp.zeros_like(stb_ref)

    tril2 = tril_ref[...]
    causal = tril2[:, :HG_CHUNK] > 0
    heads = range(HG_HEADS)
    hcols = [slice(h * HG_DK, (h + 1) * HG_DK) for h in heads]

    def chunk_group(g, carry):
        rows_of = [pl.ds(pl.multiple_of((g * HG_GROUP + j) * HG_CHUNK, HG_CHUNK), HG_CHUNK)
                   for j in range(HG_GROUP)]
        units = [(rows, h) for rows in rows_of for h in heads]

        bs = [_dot(tril2, jnp.concatenate([zhi_ref[rows, hcols[h]], zlo_ref[rows, hcols[h]]], axis=0))
              for rows, h in units]

        attns, upds, qins, ebls = [], [], [], []
        for (rows, h), b in zip(units, bs):
            ebl = jnp.exp(b[HG_CHUNK - 1:HG_CHUNK, :])
            eb = jnp.exp2(b * LOG2E)
            q_in = (zq_ref[rows, hcols[h]].astype(F32) * eb).astype(BF16)
            kin = zk_ref[rows, hcols[h]] * (1.0 / eb)
            k_in = kin.astype(BF16)
            k_out = (kin * ebl).astype(BF16)
            attns.append(lax.dot_general(q_in, k_in, _NT, preferred_element_type=F32))
            upds.append(lax.dot_general(zi_ref[rows, hcols[h]], k_out, _TN, preferred_element_type=F32))
            qins.append(q_in)
            ebls.append(ebl)

        outs = []
        for u, (rows, h) in enumerate(units):
            attn = jnp.where(causal, attns[u], 0.0).astype(BF16)
            lhs = jnp.concatenate([qins[u], attn], axis=1)
            rhs = jnp.concatenate([stb_ref[h], zi_ref[rows, hcols[h]]], axis=0)
            outs.append(_dot(lhs, rhs))
            st = st_ref[h] * ebls[u] + upds[u]
            st_ref[h] = st
            stb_ref[h] = st.astype(BF16).T

        for (rows, h), o in zip(units, outs):
            ss = jnp.sum(o * o, axis=-1, keepdims=True)
            gate = zg_ref[rows, hcols[h]].astype(F32)
            o_ref[rows, hcols[h]] = (o * lax.rsqrt(ss + HG_DV * RMS_EPS) * gate).astype(BF16)
        return carry

    lax.fori_loop(0, HG_TB // (HG_CHUNK * HG_GROUP), chunk_group, 0, unroll=2)


def _hgrn(z, z_lo, z_k, batch, seq):
    nt = seq // HG_TB
    tril = np.tril(np.ones((HG_CHUNK, HG_CHUNK), np.float32))
    tril = jnp.asarray(np.concatenate([tril, tril], axis=1), BF16)
    tile = lambda b, t: (b * nt + t, 0)

    def zspec(col):
        return pl.BlockSpec((HG_TB, D_MODEL), lambda b, t, col=col: (b * nt + t, col // D_MODEL))

    return pl.pallas_call(
        _hgrn_kernel,
        out_shape=jax.ShapeDtypeStruct((batch * seq, D_MODEL), BF16),
        grid=(batch, nt),
        in_specs=[_resident(tril.shape), zspec(Z_HQ), zspec(Z_HF),
                  pl.BlockSpec((HG_TB, D_MODEL), tile), pl.BlockSpec((HG_TB, D_MODEL), tile),
                  zspec(Z_HI), zspec(Z_HG)],
        out_specs=pl.BlockSpec((HG_TB, D_MODEL), tile),
        scratch_shapes=[pltpu.VMEM((HG_HEADS, HG_DV, HG_DK), F32),
                        pltpu.VMEM((HG_HEADS, HG_DV, HG_DK), BF16)],
        compiler_params=pltpu.CompilerParams(
            dimension_semantics=("arbitrary", "arbitrary"), vmem_limit_bytes=VMEM_LIMIT_BYTES),
        name="hgrn",
    )(tril, z, z, z_lo, z_k, z, z)


def _swa_bucket_table():
    qi = np.arange(SWA_BLOCK)[:, None]
    kj = np.arange(SWA_BLOCK)[None, :]
    n = np.where(kj <= qi, qi - kj, qi + SWA_BLOCK - kj).astype(np.int32)
    max_exact = NUM_BUCKETS // 2
    nf = np.maximum(n, 1).astype(np.float32)
    large = max_exact + (np.log(nf / np.float32(max_exact)) / np.float32(math.log(MAX_DISTANCE / max_exact))
                         * np.float32(NUM_BUCKETS - max_exact)).astype(np.int32)
    large = np.minimum(large, NUM_BUCKETS - 1)
    return np.where(n < max_exact, n, large).astype(np.int32)


SWA_TQ = 1024
SWA_NQ = SWA_TQ // SWA_BLOCK
SWA_MASKED = -1e30


SWA_CAST_STEPS = 8


def _swa_kernel(n_cast, relb_ref, sink_ref, bucket_ref, q_ref, kvc_ref, kvp_ref, *refs):
    cast_in, o_ref, cast_out = refs[:n_cast], refs[n_cast], refs[n_cast + 1:2 * n_cast + 1]
    bias_ref, kz_ref, vz_ref = refs[2 * n_cast + 1:]
    t = pl.program_id(1)

    @pl.when(pl.program_id(0) * pl.num_programs(1) + t < SWA_CAST_STEPS)
    def _():
        for src_ref, dst_ref in zip(cast_in, cast_out):
            dst_ref[...] = src_ref[...].astype(BF16)

    qi = lax.broadcasted_iota(jnp.int32, (SWA_BLOCK, SWA_BLOCK), 0)
    kj = lax.broadcasted_iota(jnp.int32, (SWA_BLOCK, SWA_BLOCK), 1)
    tri = kj <= qi

    @pl.when((pl.program_id(0) == 0) & (t == 0))
    def _():
        bucket = bucket_ref[...]
        for h in range(SWA_HEADS):
            acc = jnp.zeros((SWA_BLOCK, SWA_BLOCK), F32)
            for bk in range(NUM_BUCKETS):
                acc = jnp.where(bucket == bk, relb_ref[bk, h], acc)
            bias_ref[h] = acc
            bias_ref[SWA_HEADS + h] = jnp.where(tri, acc, SWA_MASKED)
        vz_ref[:, :, 128:256] = jnp.ones((2 * SWA_KV_HEADS, SWA_TQ + SWA_BLOCK, 128), BF16)

    lane = lax.broadcasted_iota(jnp.int32, (SWA_BLOCK, 2 * SWA_HEAD_DIM), 1)
    low = lane < SWA_HEAD_DIM
    scale = SWA_HEAD_DIM ** -0.5
    for blk in range(SWA_NQ + 1):
        rows = slice(blk * SWA_BLOCK, (blk + 1) * SWA_BLOCK)
        if blk == 0:
            kk = kvp_ref[:, 0:128].astype(F32) * scale
            vv = kvp_ref[:, 128:256].astype(F32)
        else:
            src = slice((blk - 1) * SWA_BLOCK, blk * SWA_BLOCK)
            kk = kvc_ref[src, 0:128].astype(F32) * scale
            vv = kvc_ref[src, 128:256].astype(F32)
        kk_sw = pltpu.roll(kk, SWA_HEAD_DIM, 1)
        vv_sw = pltpu.roll(vv, SWA_HEAD_DIM, 1)
        for g in range(SWA_KV_HEADS):
            k_lo, k_hi = (kk, kk_sw) if g == 0 else (kk_sw, kk)
            v_lo, v_hi = (vv, vv_sw) if g == 0 else (vv_sw, vv)
            kz_ref[2 * g + 0, rows, :] = jnp.where(low, k_lo, 0.0).astype(BF16)
            kz_ref[2 * g + 1, rows, :] = jnp.where(low, 0.0, k_hi).astype(BF16)
            vz_ref[2 * g + 0, rows, 0:128] = jnp.where(low, v_lo, 0.0).astype(BF16)
            vz_ref[2 * g + 1, rows, 0:128] = jnp.where(low, 0.0, v_hi).astype(BF16)

    def sub_block(i, carry):
        r0 = pl.multiple_of(i * SWA_BLOCK, SWA_BLOCK)
        qrows = pl.ds(r0, SWA_BLOCK)
        krows = pl.ds(r0, 2 * SWA_BLOCK)
        bias_off = jnp.where((t == 0) & (i == 0), SWA_HEADS, 0)
        for pair in range(SWA_HEADS // 2):
            g = (2 * pair) // SWA_GROUP
            qp = q_ref[qrows, pair * 128:(pair + 1) * 128]
            o_pair = None
            for half in range(2):
                h = 2 * pair + half
                slot = 2 * g + half
                s2 = lax.dot_general(qp, kz_ref[slot, krows, :], _NT, preferred_element_type=F32)
                s = jnp.where(tri, s2[:, SWA_BLOCK:], s2[:, :SWA_BLOCK]) + bias_ref[bias_off + h]
                sink = sink_ref[h]
                m = jnp.maximum(jnp.max(s, axis=-1, keepdims=True), sink)
                p = jnp.exp(s - m)
                p2 = jnp.concatenate([jnp.where(tri, 0.0, p), jnp.where(tri, p, 0.0)], axis=1).astype(BF16)
                od = _dot(p2, vz_ref[slot, krows, :])
                den = od[:, 128:] + jnp.exp(sink - m)
                o_h = od[:, :128] * (1.0 / den)
                o_pair = o_h if o_pair is None else o_pair + o_h
            o_ref[qrows, pair * 128:(pair + 1) * 128] = o_pair.astype(BF16)
        return carry

    lax.fori_loop(0, SWA_NQ, sub_block, 0, unroll=2)


def _swa(z, rel_bias, sinks, later_weights, batch, seq):
    nt = seq // SWA_TQ
    assert batch * nt >= SWA_CAST_STEPS
    bucket = jnp.asarray(_swa_bucket_table())
    kv_col = Z_KV // 256
    smem = pl.BlockSpec(memory_space=pltpu.SMEM)
    slab = lambda w: pl.BlockSpec((w.shape[0] // SWA_CAST_STEPS, w.shape[1]),
                                  lambda b, t: (jnp.minimum(b * nt + t, SWA_CAST_STEPS - 1), 0))
    outs = pl.pallas_call(
        functools.partial(_swa_kernel, len(later_weights)),
        out_shape=[jax.ShapeDtypeStruct((batch * seq, D_MODEL), BF16)]
                  + [jax.ShapeDtypeStruct(w.shape, BF16) for w in later_weights],
        grid=(batch, nt),
        in_specs=[smem, smem, _resident(bucket.shape),
                  pl.BlockSpec((SWA_TQ, D_MODEL), lambda b, t: (b * nt + t, Z_SQ // D_MODEL)),
                  pl.BlockSpec((SWA_TQ, 256), lambda b, t: (b * nt + t, kv_col)),
                  pl.BlockSpec((SWA_BLOCK, 256),
                               lambda b, t: ((b * nt + t) * SWA_NQ - jnp.minimum(t, 1), kv_col))]
                 + [slab(w) for w in later_weights],
        out_specs=[pl.BlockSpec((SWA_TQ, D_MODEL), lambda b, t: (b * nt + t, 0))]
                  + [slab(w) for w in later_weights],
        scratch_shapes=[pltpu.VMEM((2 * SWA_HEADS, SWA_BLOCK, SWA_BLOCK), F32),
                        pltpu.VMEM((2 * SWA_KV_HEADS, SWA_TQ + SWA_BLOCK, 128), BF16),
                        pltpu.VMEM((2 * SWA_KV_HEADS, SWA_TQ + SWA_BLOCK, 256), BF16)],
        compiler_params=pltpu.CompilerParams(
            dimension_semantics=("arbitrary", "arbitrary"), vmem_limit_bytes=VMEM_LIMIT_BYTES),
        name="swa",
    )(rel_bias, sinks, bucket, z, z, z, *later_weights)
    return outs[0], outs[1:]


def _mem_kv_kernel(mem_ref, w_ref, kv_ref):
    mem_w = MEM_HEADS * MEM_HEAD_DIM
    mb = mem_ref[...].astype(BF16)
    kv_ref[:, :mem_w] = (_dot(mb, w_ref[:, :mem_w].astype(BF16)) * MEM_HEAD_DIM ** -0.5).astype(BF16)
    kv_ref[:, mem_w:] = _dot(mb, w_ref[:, mem_w:].astype(BF16)).astype(BF16)


def _mem_kv(mem2d, w_b):
    return pl.pallas_call(
        _mem_kv_kernel,
        out_shape=jax.ShapeDtypeStruct((mem2d.shape[0], w_b.shape[1]), BF16),
        compiler_params=pltpu.CompilerParams(vmem_limit_bytes=VMEM_LIMIT_BYTES),
        name="mem_kv",
    )(mem2d, w_b)


TAIL_TM = 512
TAIL_SUB = 256
TAIL_TF = 1024


def _layer_norm(y, g, b):
    mu = jnp.mean(y, axis=-1, keepdims=True)
    yc = y - mu
    var = jnp.mean(yc * yc, axis=-1, keepdims=True)
    return yc * lax.rsqrt(var + LN_EPS) * g + b


def _tail_kernel(alpha, x_ref, oa_ref, ob_ref, mq_ref, gl_ref, kv_ref, wa_ref, wb_ref, wc_ref, wo_ref,
                 g1_ref, b1_ref, wu_ref, wd_ref, g2_ref, b2_ref, o_ref):
    subs = [slice(r, r + TAIL_SUB) for r in range(0, TAIL_TM, TAIL_SUB)]
    mem_w = MEM_HEADS * MEM_HEAD_DIM
    hcols = [slice(h * MEM_HEAD_DIM, (h + 1) * MEM_HEAD_DIM) for h in range(MEM_HEADS)]

    def gate(rows, i):
        return jax.nn.sigmoid(gl_ref[rows, i * D_MODEL:(i + 1) * D_MODEL].astype(F32))

    scores = [[lax.dot_general(mq_ref[rows, c], kv_ref[:, c], _NT, preferred_element_type=F32) for c in hcols]
              for rows in subs]
    merged = [gate(rows, 0) * _dot(oa_ref[rows, :], wa_ref[...]) for rows in subs]
    probs = []
    for s_heads in scores:
        ps = []
        for s in s_heads:
            p = jnp.exp(s - jnp.max(s, axis=-1, keepdims=True))
            ps.append((p * (1.0 / jnp.sum(p, axis=-1, keepdims=True))).astype(BF16))
        probs.append(ps)
    o_c = [jnp.concatenate([_dot(p, kv_ref[:, mem_w + h * MEM_HEAD_DIM:mem_w + (h + 1) * MEM_HEAD_DIM]).astype(BF16)
                            for h, p in enumerate(ps)], axis=1) for ps in probs]
    merged = [m + gate(rows, 1) * _dot(ob_ref[rows, :], wb_ref[...]) for m, rows in zip(merged, subs)]
    merged = [m + gate(rows, 2) * _dot(oc, wc_ref[...]) for m, rows, oc in zip(merged, subs, o_c)]

    h1 = []
    for m, rows in zip(merged, subs):
        mix = _dot(m.astype(BF16), wo_ref[...])
        h1.append(_layer_norm(alpha * x_ref[rows, :] + mix, g1_ref[...], b1_ref[...]))

    hb = [h.astype(BF16) for h in h1]
    ff = [None] * len(subs)
    for f0 in range(0, D_FF, TAIL_TF):
        us = [jnp.maximum(_dot(hb[i], wu_ref[:, f0:f0 + TAIL_TF]), 0.0) for i in range(len(subs))]
        for i, u in enumerate(us):
            d = _dot((u * u).astype(BF16), wd_ref[f0:f0 + TAIL_TF, :])
            ff[i] = d if ff[i] is None else ff[i] + d
    for i, rows in enumerate(subs):
        o_ref[rows, :] = _layer_norm(alpha * h1[i] + ff[i], g2_ref[...], b2_ref[...])


def _tail(alpha, x2d, o_a, o_b, z, kv, wa, wb, wc, wo, ln1_g, ln1_b, wu, wd, ln2_g, ln2_b, batch, seq):
    t = x2d.shape[0]
    per_batch = seq // TAIL_TM
    row = lambda i: (i, 0)
    weights = (wa, wb, wc, wo, ln1_g, ln1_b, wu, wd, ln2_g, ln2_b)
    return pl.pallas_call(
        functools.partial(_tail_kernel, alpha),
        out_shape=jax.ShapeDtypeStruct((t, D_MODEL), F32),
        grid=(t // TAIL_TM,),
        in_specs=[pl.BlockSpec((TAIL_TM, D_MODEL), row),
                  pl.BlockSpec((TAIL_TM, D_MODEL), row),
                  pl.BlockSpec((TAIL_TM, D_MODEL), row),
                  pl.BlockSpec((TAIL_TM, D_MODEL), lambda i: (i, Z_MQ // D_MODEL)),
                  pl.BlockSpec((TAIL_TM, 3 * D_MODEL), lambda i: (i, Z_GL // (3 * D_MODEL))),
                  pl.BlockSpec((kv.shape[0] // batch, kv.shape[1]), lambda i: (i // per_batch, 0))]
                 + [_resident(w.shape) for w in weights],
        out_specs=pl.BlockSpec((TAIL_TM, D_MODEL), row),
        compiler_params=pltpu.CompilerParams(
            dimension_semantics=("arbitrary",), vmem_limit_bytes=VMEM_LIMIT_BYTES),
        name="tail",
    )(x2d, o_a, o_b, z, z, kv, *weights)


def kernel(x, mem, w_in, lb_logits, hg_norm_gain, swa_sinks, rel_bias, w_mem_kv, w_branch_hg, w_branch_swa,
           w_branch_mem, w_out, ln1_g, ln1_b, w_up, w_down, ln2_g, ln2_b):
    batch, seq, _ = x.shape
    depth = w_in.shape[0]
    assert depth == 1 and lb_logits.shape[0] == depth + 1
    alpha = (2.0 * depth) ** 0.25
    out_dtype = x.dtype

    w_in_b = w_in[0].astype(BF16)
    x2d = x.reshape(batch * seq, D_MODEL).astype(F32)
    later = [w[0].astype(F32) for w in (w_branch_hg, w_branch_swa, w_branch_mem, w_out, w_up, w_down)]
    z, z_lo, z_k = _proj_in(x2d, w_in_b, lb_logits.astype(F32), hg_norm_gain.astype(F32))
    o_a = _hgrn(z, z_lo, z_k, batch, seq)
    o_b, (wa, wb, wc, wo, wu, wd) = _swa(z, rel_bias.astype(F32), swa_sinks[0].astype(F32), later, batch, seq)
    kv = _mem_kv(mem.reshape(batch * mem.shape[1], D_MODEL).astype(F32), w_mem_kv[0].astype(F32))
    h2 = _tail(alpha, x2d, o_a, o_b, z, kv, wa, wb, wc, wo, ln1_g.astype(F32), ln1_b.astype(F32),
               wu, wd, ln2_g.astype(F32), ln2_b.astype(F32), batch, seq)
    return h2.reshape(batch, seq, D_MODEL).astype(out_dtype)
```

```python
import functools
import math

import numpy as np
import jax
import jax.numpy as jnp
from jax import lax
from jax.experimental import pallas as pl
from jax.experimental.pallas import tpu as pltpu

F32 = jnp.float32
BF16 = jnp.bfloat16

D_MODEL = 1024
HG_HEADS = 8
HG_DK = 128
HG_DV = 128
HG_CHUNK = 64
SWA_HEADS = 16
SWA_KV_HEADS = 2
SWA_GROUP = SWA_HEADS // SWA_KV_HEADS
SWA_HEAD_DIM = 64
SWA_BLOCK = 128
MEM_HEADS = 4
MEM_HEAD_DIM = 256
NUM_BUCKETS = 32
MAX_DISTANCE = 128
D_FF = 4 * D_MODEL
LN_EPS = 1e-5
RMS_EPS = 1e-6
LOG2E = 1.4426950408889634

Z_HQ, Z_HF, Z_HI, Z_HG, Z_SQ, Z_MQ = (i * D_MODEL for i in range(6))
Z_GL = 6 * D_MODEL
Z_KV = 9 * D_MODEL
Z_COLS = Z_KV + 2 * SWA_KV_HEADS * SWA_HEAD_DIM

VMEM_LIMIT_BYTES = 56 * 1024 * 1024

_NT = (((1,), (1,)), ((), ()))
_TN = (((0,), (0,)), ((), ()))


def _dot(a, b):
    return jnp.dot(a, b, preferred_element_type=F32)


def _resident(shape):
    nd = len(shape)
    return pl.BlockSpec(shape, lambda *_: (0,) * nd, pipeline_mode=pl.Buffered(1))


PROJ_TM = 512
PROJ_TN = 1024


def _proj_col_chunks():
    kv_w = 2 * SWA_KV_HEADS * SWA_HEAD_DIM
    src_kv = Z_MQ
    segments = ((0, 0, Z_MQ), (src_kv + kv_w, Z_MQ, Z_KV - Z_MQ), (src_kv, Z_KV, kv_w))
    chunks = []
    for src, dst, width in segments:
        for off in range(0, width, PROJ_TN):
            chunks.append((src + off, dst + off, min(PROJ_TN, width - off)))
    return chunks


PROJ_W_CHUNK = 256
PROJ_W_NCHUNK = Z_COLS // PROJ_W_CHUNK


def _proj_in_kernel(x_ref, w_hbm, lbl_ref, gain_ref, z_ref, lo_ref, k_ref, w_ref, stage_ref, sem_ref):
    def w_copy(c):
        cols = slice(c * PROJ_W_CHUNK, (c + 1) * PROJ_W_CHUNK)
        return pltpu.make_async_copy(w_hbm.at[:, cols], stage_ref.at[c % 2], sem_ref.at[c % 2])

    @pl.when(pl.program_id(0) == 0)
    def _():
        w_copy(0).start()
        for c in range(PROJ_W_NCHUNK):
            if c + 1 < PROJ_W_NCHUNK:
                w_copy(c + 1).start()
            w_copy(c).wait()
            w_ref[:, c * PROJ_W_CHUNK:(c + 1) * PROJ_W_CHUNK] = stage_ref[c % 2].astype(BF16)

    l0 = lbl_ref[0:1, :]
    l1 = lbl_ref[1:2, :]
    lmax = jnp.maximum(l0, l1)
    e0 = jnp.exp(l0 - lmax)
    e1 = jnp.exp(l1 - lmax)
    lb = e0 / (e0 + e1)
    f_mid = 0.5 * (1.0 + lb)
    f_amp = 0.5 * (1.0 - lb)
    gain_s = gain_ref[...] * (HG_DV ** 0.5)

    xb = x_ref[...].astype(BF16)
    for src, dst, width in _proj_col_chunks():
        res = _dot(xb, w_ref[:, src:src + width])
        if dst == Z_HF:
            f = f_mid + f_amp * jnp.tanh(0.5 * res)
            logf = jnp.log(f)
            hi = logf.astype(BF16)
            z_ref[:, dst:dst + width] = hi
            lo_ref[...] = (logf - hi.astype(F32)).astype(BF16)
            k_ref[...] = 1.0 - f
        elif dst == Z_HG:
            hg = 0.5 * res
            z_ref[:, dst:dst + width] = (gain_s * (hg + hg * jnp.tanh(hg))).astype(BF16)
        else:
            z_ref[:, dst:dst + width] = res.astype(BF16)


def _proj_in(x2d, w_in_f32, lb_logits, gain):
    t = x2d.shape[0]
    row = lambda width: pl.BlockSpec((PROJ_TM, width), lambda i: (i, 0))
    return pl.pallas_call(
        _proj_in_kernel,
        out_shape=[jax.ShapeDtypeStruct((t, Z_COLS), BF16), jax.ShapeDtypeStruct((t, D_MODEL), BF16),
                   jax.ShapeDtypeStruct((t, D_MODEL), F32)],
        grid=(t // PROJ_TM,),
        in_specs=[row(D_MODEL), pl.BlockSpec(memory_space=pl.ANY),
                  _resident(lb_logits.shape), _resident(gain.shape)],
        out_specs=[row(Z_COLS), row(D_MODEL), row(D_MODEL)],
        scratch_shapes=[pltpu.VMEM((D_MODEL, Z_COLS), BF16),
                        pltpu.VMEM((2, D_MODEL, PROJ_W_CHUNK), F32),
                        pltpu.SemaphoreType.DMA((2,))],
        compiler_params=pltpu.CompilerParams(
            dimension_semantics=("arbitrary",), vmem_limit_bytes=VMEM_LIMIT_BYTES),
        name="proj_in",
    )(x2d, w_in_f32, lb_logits, gain)


HG_TB = 1024
HG_GROUP = 2


def _hgrn_kernel(tril_ref, zq_ref, zhi_ref, zlo_ref, zk_ref, zi_ref, zg_ref, o_ref, st_ref, stb_ref):
    @pl.when(pl.program_id(1) == 0)
    def _():
        st_ref[...] = jnp.zeros_like(st_ref)
        stb_ref[...] = jnp.zeros_like(stb_ref)

    tril2 = tril_ref[...]
    causal = tril2[:, :HG_CHUNK] > 0
    heads = range(HG_HEADS)
    hcols = [slice(h * HG_DK, (h + 1) * HG_DK) for h in heads]

    def chunk_group(g, carry):
        rows_of = [pl.ds(pl.multiple_of((g * HG_GROUP + j) * HG_CHUNK, HG_CHUNK), HG_CHUNK)
                   for j in range(HG_GROUP)]
        units = [(rows, h) for rows in rows_of for h in heads]

        bs = [_dot(tril2, jnp.concatenate([zhi_ref[rows, hcols[h]], zlo_ref[rows, hcols[h]]], axis=0))
              for rows, h in units]

        attns, upds, qins, ebls = [], [], [], []
        for (rows, h), b in zip(units, bs):
            ebl = jnp.exp(b[HG_CHUNK - 1:HG_CHUNK, :])
            eb = jnp.exp2(b * LOG2E)
            q_in = (zq_ref[rows, hcols[h]].astype(F32) * eb).astype(BF16)
            kin = zk_ref[rows, hcols[h]] * (1.0 / eb)
            k_in = kin.astype(BF16)
            k_out = (kin * ebl).astype(BF16)
            attns.append(lax.dot_general(q_in, k_in, _NT, preferred_element_type=F32))
            upds.append(lax.dot_general(zi_ref[rows, hcols[h]], k_out, _TN, preferred_element_type=F32))
            qins.append(q_in)
            ebls.append(ebl)

        outs = []
        for u, (rows, h) in enumerate(units):
            attn = jnp.where(causal, attns[u], 0.0).astype(BF16)
            lhs = jnp.concatenate([qins[u], attn], axis=1)
            rhs = jnp.concatenate([stb_ref[h], zi_ref[rows, hcols[h]]], axis=0)
            outs.append(_dot(lhs, rhs))
            st = st_ref[h] * ebls[u] + upds[u]
            st_ref[h] = st
            stb_ref[h] = st.astype(BF16).T

        for (rows, h), o in zip(units, outs):
            ss = jnp.sum(o * o, axis=-1, keepdims=True)
            gate = zg_ref[rows, hcols[h]].astype(F32)
            o_ref[rows, hcols[h]] = (o * lax.rsqrt(ss + HG_DV * RMS_EPS) * gate).astype(BF16)
        return carry

    lax.fori_loop(0, HG_TB // (HG_CHUNK * HG_GROUP), chunk_group, 0, unroll=2)


def _hgrn(z, z_lo, z_k, batch, seq):
    nt = seq // HG_TB
    tril = np.tril(np.ones((HG_CHUNK, HG_CHUNK), np.float32))
    tril = jnp.asarray(np.concatenate([tril, tril], axis=1), BF16)
    tile = lambda b, t: (b * nt + t, 0)

    def zspec(col):
        return pl.BlockSpec((HG_TB, D_MODEL), lambda b, t, col=col: (b * nt + t, col // D_MODEL))

    return pl.pallas_call(
        _hgrn_kernel,
        out_shape=jax.ShapeDtypeStruct((batch * seq, D_MODEL), BF16),
        grid=(batch, nt),
        in_specs=[_resident(tril.shape), zspec(Z_HQ), zspec(Z_HF),
                  pl.BlockSpec((HG_TB, D_MODEL), tile), pl.BlockSpec((HG_TB, D_MODEL), tile),
                  zspec(Z_HI), zspec(Z_HG)],
        out_specs=pl.BlockSpec((HG_TB, D_MODEL), tile),
        scratch_shapes=[pltpu.VMEM((HG_HEADS, HG_DV, HG_DK), F32),
                        pltpu.VMEM((HG_HEADS, HG_DV, HG_DK), BF16)],
        compiler_params=pltpu.CompilerParams(
            dimension_semantics=("arbitrary", "arbitrary"), vmem_limit_bytes=VMEM_LIMIT_BYTES),
        name="hgrn",
    )(tril, z, z, z_lo, z_k, z, z)


def _swa_bucket_table():
    qi = np.arange(SWA_BLOCK)[:, None]
    kj = np.arange(SWA_BLOCK)[None, :]
    n = np.where(kj <= qi, qi - kj, qi + SWA_BLOCK - kj).astype(np.int32)
    max_exact = NUM_BUCKETS // 2
    nf = np.maximum(n, 1).astype(np.float32)
    large = max_exact + (np.log(nf / np.float32(max_exact)) / np.float32(math.log(MAX_DISTANCE / max_exact))
                         * np.float32(NUM_BUCKETS - max_exact)).astype(np.int32)
    large = np.minimum(large, NUM_BUCKETS - 1)
    return np.where(n < max_exact, n, large).astype(np.int32)


SWA_TQ = 1024
SWA_NQ = SWA_TQ // SWA_BLOCK
SWA_MASKED = -1e30


SWA_CAST_STEPS = 8


def _swa_kernel(n_cast, relb_ref, sink_ref, bucket_ref, q_ref, kvc_ref, kvp_ref, *refs):
    cast_in, o_ref, cast_out = refs[:n_cast], refs[n_cast], refs[n_cast + 1:2 * n_cast + 1]
    bias_ref, kz_ref, vz_ref = refs[2 * n_cast + 1:]
    t = pl.program_id(1)

    @pl.when(pl.program_id(0) * pl.num_programs(1) + t < SWA_CAST_STEPS)
    def _():
        for src_ref, dst_ref in zip(cast_in, cast_out):
            dst_ref[...] = src_ref[...].astype(BF16)

    qi = lax.broadcasted_iota(jnp.int32, (SWA_BLOCK, SWA_BLOCK), 0)
    kj = lax.broadcasted_iota(jnp.int32, (SWA_BLOCK, SWA_BLOCK), 1)
    tri = kj <= qi

    @pl.when((pl.program_id(0) == 0) & (t == 0))
    def _():
        bucket = bucket_ref[...]
        for h in range(SWA_HEADS):
            acc = jnp.zeros((SWA_BLOCK, SWA_BLOCK), F32)
            for bk in range(NUM_BUCKETS):
                acc = jnp.where(bucket == bk, relb_ref[bk, h], acc)
            bias_ref[h] = acc
            bias_ref[SWA_HEADS + h] = jnp.where(tri, acc, SWA_MASKED)
        vz_ref[:, :, 128:256] = jnp.ones((2 * SWA_KV_HEADS, SWA_TQ + SWA_BLOCK, 128), BF16)

    lane = lax.broadcasted_iota(jnp.int32, (SWA_BLOCK, 2 * SWA_HEAD_DIM), 1)
    low = lane < SWA_HEAD_DIM
    scale = SWA_HEAD_DIM ** -0.5
    for blk in range(SWA_NQ + 1):
        rows = slice(blk * SWA_BLOCK, (blk + 1) * SWA_BLOCK)
        if blk == 0:
            kk = kvp_ref[:, 0:128].astype(F32) * scale
            vv = kvp_ref[:, 128:256].astype(F32)
        else:
            src = slice((blk - 1) * SWA_BLOCK, blk * SWA_BLOCK)
            kk = kvc_ref[src, 0:128].astype(F32) * scale
            vv = kvc_ref[src, 128:256].astype(F32)
        kk_sw = pltpu.roll(kk, SWA_HEAD_DIM, 1)
        vv_sw = pltpu.roll(vv, SWA_HEAD_DIM, 1)
        for g in range(SWA_KV_HEADS):
            k_lo, k_hi = (kk, kk_sw) if g == 0 else (kk_sw, kk)
            v_lo, v_hi = (vv, vv_sw) if g == 0 else (vv_sw, vv)
            kz_ref[2 * g + 0, rows, :] = jnp.where(low, k_lo, 0.0).astype(BF16)
            kz_ref[2 * g + 1, rows, :] = jnp.where(low, 0.0, k_hi).astype(BF16)
            vz_ref[2 * g + 0, rows, 0:128] = jnp.where(low, v_lo, 0.0).astype(BF16)
            vz_ref[2 * g + 1, rows, 0:128] = jnp.where(low, 0.0, v_hi).astype(BF16)

    def sub_block(i, carry):
        r0 = pl.multiple_of(i * SWA_BLOCK, SWA_BLOCK)
        qrows = pl.ds(r0, SWA_BLOCK)
        krows = pl.ds(r0, 2 * SWA_BLOCK)
        bias_off = jnp.where((t == 0) & (i == 0), SWA_HEADS, 0)
        for pair in range(SWA_HEADS // 2):
            g = (2 * pair) // SWA_GROUP
            qp = q_ref[qrows, pair * 128:(pair + 1) * 128]
            o_pair = None
            for half in range(2):
                h = 2 * pair + half
                slot = 2 * g + half
                s2 = lax.dot_general(qp, kz_ref[slot, krows, :], _NT, preferred_element_type=F32)
                s = jnp.where(tri, s2[:, SWA_BLOCK:], s2[:, :SWA_BLOCK]) + bias_ref[bias_off + h]
                sink = sink_ref[h]
                m = jnp.maximum(jnp.max(s, axis=-1, keepdims=True), sink)
                p = jnp.exp(s - m)
                p2 = jnp.concatenate([jnp.where(tri, 0.0, p), jnp.where(tri, p, 0.0)], axis=1).astype(BF16)
                od = _dot(p2, vz_ref[slot, krows, :])
                den = od[:, 128:] + jnp.exp(sink - m)
                o_h = od[:, :128] * (1.0 / den)
                o_pair = o_h if o_pair is None else o_pair + o_h
            o_ref[qrows, pair * 128:(pair + 1) * 128] = o_pair.astype(BF16)
        return carry

    lax.fori_loop(0, SWA_NQ, sub_block, 0, unroll=2)


def _swa(z, rel_bias, sinks, later_weights, batch, seq):
    nt = seq // SWA_TQ
    assert batch * nt >= SWA_CAST_STEPS
    bucket = jnp.asarray(_swa_bucket_table())
    kv_col = Z_KV // 256
    smem = pl.BlockSpec(memory_space=pltpu.SMEM)
    slab = lambda w: pl.BlockSpec((w.shape[0] // SWA_CAST_STEPS, w.shape[1]),
                                  lambda b, t: (jnp.minimum(b * nt + t, SWA_CAST_STEPS - 1), 0))
    outs = pl.pallas_call(
        functools.partial(_swa_kernel, len(later_weights)),
        out_shape=[jax.ShapeDtypeStruct((batch * seq, D_MODEL), BF16)]
                  + [jax.ShapeDtypeStruct(w.shape, BF16) for w in later_weights],
        grid=(batch, nt),
        in_specs=[smem, smem, _resident(bucket.shape),
                  pl.BlockSpec((SWA_TQ, D_MODEL), lambda b, t: (b * nt + t, Z_SQ // D_MODEL)),
                  pl.BlockSpec((SWA_TQ, 256), lambda b, t: (b * nt + t, kv_col)),
                  pl.BlockSpec((SWA_BLOCK, 256),
                               lambda b, t: ((b * nt + t) * SWA_NQ - jnp.minimum(t, 1), kv_col))]
                 + [slab(w) for w in later_weights],
        out_specs=[pl.BlockSpec((SWA_TQ, D_MODEL), lambda b, t: (b * nt + t, 0))]
                  + [slab(w) for w in later_weights],
        scratch_shapes=[pltpu.VMEM((2 * SWA_HEADS, SWA_BLOCK, SWA_BLOCK), F32),
                        pltpu.VMEM((2 * SWA_KV_HEADS, SWA_TQ + SWA_BLOCK, 128), BF16),
                        pltpu.VMEM((2 * SWA_KV_HEADS, SWA_TQ + SWA_BLOCK, 256), BF16)],
        compiler_params=pltpu.CompilerParams(
            dimension_semantics=("arbitrary", "arbitrary"), vmem_limit_bytes=VMEM_LIMIT_BYTES),
        name="swa",
    )(rel_bias, sinks, bucket, z, z, z, *later_weights)
    return outs[0], outs[1:]


def _mem_kv_kernel(mem_ref, w_ref, kv_ref):
    mem_w = MEM_HEADS * MEM_HEAD_DIM
    mb = mem_ref[...].astype(BF16)
    kv_ref[:, :mem_w] = (_dot(mb, w_ref[:, :mem_w].astype(BF16)) * MEM_HEAD_DIM ** -0.5).astype(BF16)
    kv_ref[:, mem_w:] = _dot(mb, w_ref[:, mem_w:].astype(BF16)).astype(BF16)


def _mem_kv(mem2d, w_b):
    return pl.pallas_call(
        _mem_kv_kernel,
        out_shape=jax.ShapeDtypeStruct((mem2d.shape[0], w_b.shape[1]), BF16),
        compiler_params=pltpu.CompilerParams(vmem_limit_bytes=VMEM_LIMIT_BYTES),
        name="mem_kv",
    )(mem2d, w_b)


TAIL_TM = 512
TAIL_SUB = 256
TAIL_TF = 1024


def _layer_norm(y, g, b):
    mu = jnp.mean(y, axis=-1, keepdims=True)
    yc = y - mu
    var = jnp.mean(yc * yc, axis=-1, keepdims=True)
    return yc * lax.rsqrt(var + LN_EPS) * g + b


def _tail_kernel(alpha, x_ref, oa_ref, ob_ref, mq_ref, gl_ref, kv_ref, wa_ref, wb_ref, wc_ref, wo_ref,
                 g1_ref, b1_ref, wu_ref, wd_ref, g2_ref, b2_ref, o_ref):
    subs = [slice(r, r + TAIL_SUB) for r in range(0, TAIL_TM, TAIL_SUB)]
    mem_w = MEM_HEADS * MEM_HEAD_DIM
    hcols = [slice(h * MEM_HEAD_DIM, (h + 1) * MEM_HEAD_DIM) for h in range(MEM_HEADS)]

    def gate(rows, i):
        return jax.nn.sigmoid(gl_ref[rows, i * D_MODEL:(i + 1) * D_MODEL].astype(F32))

    scores = [[lax.dot_general(mq_ref[rows, c], kv_ref[:, c], _NT, preferred_element_type=F32) for c in hcols]
              for rows in subs]
    merged = [gate(rows, 0) * _dot(oa_ref[rows, :], wa_ref[...]) for rows in subs]
    probs = []
    for s_heads in scores:
        ps = []
        for s in s_heads:
            p = jnp.exp(s - jnp.max(s, axis=-1, keepdims=True))
            ps.append((p * (1.0 / jnp.sum(p, axis=-1, keepdims=True))).astype(BF16))
        probs.append(ps)
    o_c = [jnp.concatenate([_dot(p, kv_ref[:, mem_w + h * MEM_HEAD_DIM:mem_w + (h + 1) * MEM_HEAD_DIM]).astype(BF16)
                            for h, p in enumerate(ps)], axis=1) for ps in probs]
    merged = [m + gate(rows, 1) * _dot(ob_ref[rows, :], wb_ref[...]) for m, rows in zip(merged, subs)]
    merged = [m + gate(rows, 2) * _dot(oc, wc_ref[...]) for m, rows, oc in zip(merged, subs, o_c)]

    h1 = []
    for m, rows in zip(merged, subs):
        mix = _dot(m.astype(BF16), wo_ref[...])
        h1.append(_layer_norm(alpha * x_ref[rows, :] + mix, g1_ref[...], b1_ref[...]))

    hb = [h.astype(BF16) for h in h1]
    ff = [None] * len(subs)
    for f0 in range(0, D_FF, TAIL_TF):
        us = [jnp.maximum(_dot(hb[i], wu_ref[:, f0:f0 + TAIL_TF]), 0.0) for i in range(len(subs))]
        for i, u in enumerate(us):
            d = _dot((u * u).astype(BF16), wd_ref[f0:f0 + TAIL_TF, :])
            ff[i] = d if ff[i] is None else ff[i] + d
    for i, rows in enumerate(subs):
        o_ref[rows, :] = _layer_norm(alpha * h1[i] + ff[i], g2_ref[...], b2_ref[...])


def _tail(alpha, x2d, o_a, o_b, z, kv, wa, wb, wc, wo, ln1_g, ln1_b, wu, wd, ln2_g, ln2_b, batch, seq):
    t = x2d.shape[0]
    per_batch = seq // TAIL_TM
    row = lambda i: (i, 0)
    weights = (wa, wb, wc, wo, ln1_g, ln1_b, wu, wd, ln2_g, ln2_b)
    return pl.pallas_call(
        functools.partial(_tail_kernel, alpha),
        out_shape=jax.ShapeDtypeStruct((t, D_MODEL), F32),
        grid=(t // TAIL_TM,),
        in_specs=[pl.BlockSpec((TAIL_TM, D_MODEL), row),
                  pl.BlockSpec((TAIL_TM, D_MODEL), row),
                  pl.BlockSpec((TAIL_TM, D_MODEL), row),
                  pl.BlockSpec((TAIL_TM, D_MODEL), lambda i: (i, Z_MQ // D_MODEL)),
                  pl.BlockSpec((TAIL_TM, 3 * D_MODEL), lambda i: (i, Z_GL // (3 * D_MODEL))),
                  pl.BlockSpec((kv.shape[0] // batch, kv.shape[1]), lambda i: (i // per_batch, 0))]
                 + [_resident(w.shape) for w in weights],
        out_specs=pl.BlockSpec((TAIL_TM, D_MODEL), row),
        compiler_params=pltpu.CompilerParams(
            dimension_semantics=("arbitrary",), vmem_limit_bytes=VMEM_LIMIT_BYTES),
        name="tail",
    )(x2d, o_a, o_b, z, z, kv, *weights)


def kernel(x, mem, w_in, lb_logits, hg_norm_gain, swa_sinks, rel_bias, w_mem_kv, w_branch_hg, w_branch_swa,
           w_branch_mem, w_out, ln1_g, ln1_b, w_up, w_down, ln2_g, ln2_b):
    batch, seq, _ = x.shape
    depth = w_in.shape[0]
    assert depth == 1 and lb_logits.shape[0] == depth + 1
    alpha = (2.0 * depth) ** 0.25
    out_dtype = x.dtype

    x2d = x.reshape(batch * seq, D_MODEL).astype(F32)
    later = [w[0].astype(F32) for w in (w_branch_hg, w_branch_swa, w_branch_mem, w_out, w_up, w_down)]
    z, z_lo, z_k = _proj_in(x2d, w_in[0].astype(F32), lb_logits.astype(F32), hg_norm_gain.astype(F32))
    o_a = _hgrn(z, z_lo, z_k, batch, seq)
    o_b, (wa, wb, wc, wo, wu, wd) = _swa(z, rel_bias.astype(F32), swa_sinks[0].astype(F32), later, batch, seq)
    kv = _mem_kv(mem.reshape(batch * mem.shape[1], D_MODEL).astype(F32), w_mem_kv[0].astype(F32))
    h2 = _tail(alpha, x2d, o_a, o_b, z, kv, wa, wb, wc, wo, ln1_g.astype(F32), ln1_b.astype(F32),
               wu, wd, ln2_g.astype(F32), ln2_b.astype(F32), batch, seq)
    return h2.reshape(batch, seq, D_MODEL).astype(out_dtype)
```

```python
import functools
import math

import numpy as np
import jax
import jax.numpy as jnp
from jax import lax
from jax.experimental import pallas as pl
from jax.experimental.pallas import tpu as pltpu

F32 = jnp.float32
BF16 = jnp.bfloat16

D_MODEL = 1024
HG_HEADS = 8
HG_DK = 128
HG_DV = 128
HG_CHUNK = 64
SWA_HEADS = 16
SWA_KV_HEADS = 2
SWA_GROUP = SWA_HEADS // SWA_KV_HEADS
SWA_HEAD_DIM = 64
SWA_BLOCK = 128
MEM_HEADS = 4
MEM_HEAD_DIM = 256
NUM_BUCKETS = 32
MAX_DISTANCE = 128
D_FF = 4 * D_MODEL
LN_EPS = 1e-5
RMS_EPS = 1e-6
LOG2E = 1.4426950408889634

Z_HQ, Z_HF, Z_HI, Z_HG, Z_SQ, Z_MQ = (i * D_MODEL for i in range(6))
Z_GL = 6 * D_MODEL
Z_KV = 9 * D_MODEL
Z_COLS = Z_KV + 2 * SWA_KV_HEADS * SWA_HEAD_DIM

VMEM_LIMIT_BYTES = 56 * 1024 * 1024

_NT = (((1,), (1,)), ((), ()))
_TN = (((0,), (0,)), ((), ()))


def _dot(a, b):
    return jnp.dot(a, b, preferred_element_type=F32)


def _resident(shape):
    nd = len(shape)
    return pl.BlockSpec(shape, lambda *_: (0,) * nd, pipeline_mode=pl.Buffered(1))


PROJ_TM = 512
PROJ_TN = 1024


def _proj_col_chunks():
    kv_w = 2 * SWA_KV_HEADS * SWA_HEAD_DIM
    src_kv = Z_MQ
    segments = ((0, 0, Z_MQ), (src_kv + kv_w, Z_MQ, Z_KV - Z_MQ), (src_kv, Z_KV, kv_w))
    chunks = []
    for src, dst, width in segments:
        for off in range(0, width, PROJ_TN):
            chunks.append((src + off, dst + off, min(PROJ_TN, width - off)))
    return chunks


def _proj_in_kernel(n_cast, x_ref, w_ref, lbl_ref, gain_ref, *refs):
    cast_in, (z_ref, lo_ref, k_ref), cast_out = refs[:n_cast], refs[n_cast:n_cast + 3], refs[n_cast + 3:]
    l0 = lbl_ref[0:1, :]
    l1 = lbl_ref[1:2, :]
    lmax = jnp.maximum(l0, l1)
    e0 = jnp.exp(l0 - lmax)
    e1 = jnp.exp(l1 - lmax)
    lb = e0 / (e0 + e1)
    f_mid = 0.5 * (1.0 + lb)
    f_amp = 0.5 * (1.0 - lb)
    gain_s = gain_ref[...] * (HG_DV ** 0.5)

    xb = x_ref[...].astype(BF16)
    for src, dst, width in _proj_col_chunks():
        res = _dot(xb, w_ref[:, src:src + width])
        if dst == Z_HF:
            f = f_mid + f_amp * jnp.tanh(0.5 * res)
            logf = jnp.log(f)
            hi = logf.astype(BF16)
            z_ref[:, dst:dst + width] = hi
            lo_ref[...] = (logf - hi.astype(F32)).astype(BF16)
            k_ref[...] = 1.0 - f
        elif dst == Z_HG:
            hg = 0.5 * res
            z_ref[:, dst:dst + width] = (gain_s * (hg + hg * jnp.tanh(hg))).astype(BF16)
        else:
            z_ref[:, dst:dst + width] = res.astype(BF16)
    for src_ref, dst_ref in zip(cast_in, cast_out):
        dst_ref[...] = src_ref[...].astype(BF16)


def _proj_in(x2d, w_in_b, lb_logits, gain, later_weights):
    t = x2d.shape[0]
    steps = t // PROJ_TM
    slab = lambda w: pl.BlockSpec((w.shape[0] // steps, w.shape[1]), lambda i: (i, 0))
    row = lambda width: pl.BlockSpec((PROJ_TM, width), lambda i: (i, 0))
    outs = pl.pallas_call(
        functools.partial(_proj_in_kernel, len(later_weights)),
        out_shape=[jax.ShapeDtypeStruct((t, Z_COLS), BF16), jax.ShapeDtypeStruct((t, D_MODEL), BF16),
                   jax.ShapeDtypeStruct((t, D_MODEL), F32)]
                  + [jax.ShapeDtypeStruct(w.shape, BF16) for w in later_weights],
        grid=(steps,),
        in_specs=[row(D_MODEL), _resident((D_MODEL, Z_COLS)), _resident(lb_logits.shape), _resident(gain.shape)]
                 + [slab(w) for w in later_weights],
        out_specs=[row(Z_COLS), row(D_MODEL), row(D_MODEL)] + [slab(w) for w in later_weights],
        compiler_params=pltpu.CompilerParams(
            dimension_semantics=("arbitrary",), vmem_limit_bytes=VMEM_LIMIT_BYTES),
        name="proj_in",
    )(x2d, w_in_b, lb_logits, gain, *later_weights)
    return outs[0], outs[1], outs[2], outs[3:]


HG_TB = 512
HG_GROUP = 2


def _hgrn_kernel(tril_ref, zq_ref, zhi_ref, zlo_ref, zk_ref, zi_ref, zg_ref, o_ref, st_ref, stb_ref):
    @pl.when(pl.program_id(1) == 0)
    def _():
        st_ref[...] = jnp.zeros_like(st_ref)
        stb_ref[...] = jnp.zeros_like(stb_ref)

    tril2 = tril_ref[...]
    causal = tril2[:, :HG_CHUNK] > 0
    heads = range(HG_HEADS)
    hcols = [slice(h * HG_DK, (h + 1) * HG_DK) for h in heads]

    def chunk_group(g, carry):
        rows_of = [pl.ds(pl.multiple_of((g * HG_GROUP + j) * HG_CHUNK, HG_CHUNK), HG_CHUNK)
                   for j in range(HG_GROUP)]
        units = [(rows, h) for rows in rows_of for h in heads]

        bs = [_dot(tril2, jnp.concatenate([zhi_ref[rows, hcols[h]], zlo_ref[rows, hcols[h]]], axis=0))
              for rows, h in units]

        attns, upds, qins, ebls = [], [], [], []
        for (rows, h), b in zip(units, bs):
            ebl = jnp.exp(b[HG_CHUNK - 1:HG_CHUNK, :])
            eb = jnp.exp2(b * LOG2E)
            q_in = (zq_ref[rows, hcols[h]].astype(F32) * eb).astype(BF16)
            kin = zk_ref[rows, hcols[h]] * (1.0 / eb)
            k_in = kin.astype(BF16)
            k_out = (kin * ebl).astype(BF16)
            attns.append(lax.dot_general(q_in, k_in, _NT, preferred_element_type=F32))
            upds.append(lax.dot_general(zi_ref[rows, hcols[h]], k_out, _TN, preferred_element_type=F32))
            qins.append(q_in)
            ebls.append(ebl)

        outs = []
        for u, (rows, h) in enumerate(units):
            attn = jnp.where(causal, attns[u], 0.0).astype(BF16)
            lhs = jnp.concatenate([qins[u], attn], axis=1)
            rhs = jnp.concatenate([stb_ref[h], zi_ref[rows, hcols[h]]], axis=0)
            outs.append(_dot(lhs, rhs))
            st = st_ref[h] * ebls[u] + upds[u]
            st_ref[h] = st
            stb_ref[h] = st.astype(BF16).T

        for (rows, h), o in zip(units, outs):
            ss = jnp.sum(o * o, axis=-1, keepdims=True)
            gate = zg_ref[rows, hcols[h]].astype(F32)
            o_ref[rows, hcols[h]] = (o * lax.rsqrt(ss + HG_DV * RMS_EPS) * gate).astype(BF16)
        return carry

    lax.fori_loop(0, HG_TB // (HG_CHUNK * HG_GROUP), chunk_group, 0, unroll=2)


def _hgrn(z, z_lo, z_k, batch, seq):
    nt = seq // HG_TB
    tril = np.tril(np.ones((HG_CHUNK, HG_CHUNK), np.float32))
    tril = jnp.asarray(np.concatenate([tril, tril], axis=1), BF16)
    tile = lambda b, t: (b * nt + t, 0)

    def zspec(col):
        return pl.BlockSpec((HG_TB, D_MODEL), lambda b, t, col=col: (b * nt + t, col // D_MODEL))

    return pl.pallas_call(
        _hgrn_kernel,
        out_shape=jax.ShapeDtypeStruct((batch * seq, D_MODEL), BF16),
        grid=(batch, nt),
        in_specs=[_resident(tril.shape), zspec(Z_HQ), zspec(Z_HF),
                  pl.BlockSpec((HG_TB, D_MODEL), tile), pl.BlockSpec((HG_TB, D_MODEL), tile),
                  zspec(Z_HI), zspec(Z_HG)],
        out_specs=pl.BlockSpec((HG_TB, D_MODEL), tile),
        scratch_shapes=[pltpu.VMEM((HG_HEADS, HG_DV, HG_DK), F32),
                        pltpu.VMEM((HG_HEADS, HG_DV, HG_DK), BF16)],
        compiler_params=pltpu.CompilerParams(
            dimension_semantics=("arbitrary", "arbitrary"), vmem_limit_bytes=VMEM_LIMIT_BYTES),
        name="hgrn",
    )(tril, z, z, z_lo, z_k, z, z)


def _swa_bucket_table():
    qi = np.arange(SWA_BLOCK)[:, None]
    kj = np.arange(SWA_BLOCK)[None, :]
    n = np.where(kj <= qi, qi - kj, qi + SWA_BLOCK - kj).astype(np.int32)
    max_exact = NUM_BUCKETS // 2
    nf = np.maximum(n, 1).astype(np.float32)
    large = max_exact + (np.log(nf / np.float32(max_exact)) / np.float32(math.log(MAX_DISTANCE / max_exact))
                         * np.float32(NUM_BUCKETS - max_exact)).astype(np.int32)
    large = np.minimum(large, NUM_BUCKETS - 1)
    return np.where(n < max_exact, n, large).astype(np.int32)


SWA_TQ = 512
SWA_NQ = SWA_TQ // SWA_BLOCK
SWA_MASKED = -1e30


def _swa_kernel(relb_ref, sink_ref, bucket_ref, q_ref, kvc_ref, kvp_ref, mem_ref, wkv_ref,
                o_ref, mkv_ref, bias_ref, kz_ref, vz_ref):
    t = pl.program_id(1)
    qi = lax.broadcasted_iota(jnp.int32, (SWA_BLOCK, SWA_BLOCK), 0)
    kj = lax.broadcasted_iota(jnp.int32, (SWA_BLOCK, SWA_BLOCK), 1)
    tri = kj <= qi

    @pl.when((pl.program_id(0) == 0) & (t == 0))
    def _():
        mem_w = MEM_HEADS * MEM_HEAD_DIM
        mb = mem_ref[...].astype(BF16)
        mkv_ref[:, :mem_w] = (_dot(mb, wkv_ref[:, :mem_w].astype(BF16)) * MEM_HEAD_DIM ** -0.5).astype(BF16)
        mkv_ref[:, mem_w:] = _dot(mb, wkv_ref[:, mem_w:].astype(BF16)).astype(BF16)

        bucket = bucket_ref[...]
        for h in range(SWA_HEADS):
            acc = jnp.zeros((SWA_BLOCK, SWA_BLOCK), F32)
            for bk in range(NUM_BUCKETS):
                acc = jnp.where(bucket == bk, relb_ref[bk, h], acc)
            bias_ref[h] = acc
            bias_ref[SWA_HEADS + h] = jnp.where(tri, acc, SWA_MASKED)
        vz_ref[:, :, 128:256] = jnp.ones((2 * SWA_KV_HEADS, SWA_TQ + SWA_BLOCK, 128), BF16)

    lane = lax.broadcasted_iota(jnp.int32, (SWA_BLOCK, 2 * SWA_HEAD_DIM), 1)
    low = lane < SWA_HEAD_DIM
    scale = SWA_HEAD_DIM ** -0.5
    for blk in range(SWA_NQ + 1):
        rows = slice(blk * SWA_BLOCK, (blk + 1) * SWA_BLOCK)
        if blk == 0:
            kk = kvp_ref[:, 0:128].astype(F32) * scale
            vv = kvp_ref[:, 128:256].astype(F32)
        else:
            src = slice((blk - 1) * SWA_BLOCK, blk * SWA_BLOCK)
            kk = kvc_ref[src, 0:128].astype(F32) * scale
            vv = kvc_ref[src, 128:256].astype(F32)
        kk_sw = pltpu.roll(kk, SWA_HEAD_DIM, 1)
        vv_sw = pltpu.roll(vv, SWA_HEAD_DIM, 1)
        for g in range(SWA_KV_HEADS):
            k_lo, k_hi = (kk, kk_sw) if g == 0 else (kk_sw, kk)
            v_lo, v_hi = (vv, vv_sw) if g == 0 else (vv_sw, vv)
            kz_ref[2 * g + 0, rows, :] = jnp.where(low, k_lo, 0.0).astype(BF16)
            kz_ref[2 * g + 1, rows, :] = jnp.where(low, 0.0, k_hi).astype(BF16)
            vz_ref[2 * g + 0, rows, 0:128] = jnp.where(low, v_lo, 0.0).astype(BF16)
            vz_ref[2 * g + 1, rows, 0:128] = jnp.where(low, 0.0, v_hi).astype(BF16)

    def sub_block(i, carry):
        r0 = pl.multiple_of(i * SWA_BLOCK, SWA_BLOCK)
        qrows = pl.ds(r0, SWA_BLOCK)
        krows = pl.ds(r0, 2 * SWA_BLOCK)
        bias_off = jnp.where((t == 0) & (i == 0), SWA_HEADS, 0)
        for pair in range(SWA_HEADS // 2):
            g = (2 * pair) // SWA_GROUP
            qp = q_ref[qrows, pair * 128:(pair + 1) * 128]
            o_pair = None
            for half in range(2):
                h = 2 * pair + half
                slot = 2 * g + half
                s2 = lax.dot_general(qp, kz_ref[slot, krows, :], _NT, preferred_element_type=F32)
                s = jnp.where(tri, s2[:, SWA_BLOCK:], s2[:, :SWA_BLOCK]) + bias_ref[bias_off + h]
                sink = sink_ref[h]
                m = jnp.maximum(jnp.max(s, axis=-1, keepdims=True), sink)
                p = jnp.exp(s - m)
                p2 = jnp.concatenate([jnp.where(tri, 0.0, p), jnp.where(tri, p, 0.0)], axis=1).astype(BF16)
                od = _dot(p2, vz_ref[slot, krows, :])
                den = od[:, 128:] + jnp.exp(sink - m)
                o_h = od[:, :128] * (1.0 / den)
                o_pair = o_h if o_pair is None else o_pair + o_h
            o_ref[qrows, pair * 128:(pair + 1) * 128] = o_pair.astype(BF16)
        return carry

    lax.fori_loop(0, SWA_NQ, sub_block, 0, unroll=2)


def _swa(z, rel_bias, sinks, mem2d, w_mem_kv, batch, seq):
    nt = seq // SWA_TQ
    bucket = jnp.asarray(_swa_bucket_table())
    kv_col = Z_KV // 256
    smem = pl.BlockSpec(memory_space=pltpu.SMEM)
    mkv_shape = (mem2d.shape[0], w_mem_kv.shape[1])
    return pl.pallas_call(
        _swa_kernel,
        out_shape=[jax.ShapeDtypeStruct((batch * seq, D_MODEL), BF16), jax.ShapeDtypeStruct(mkv_shape, BF16)],
        grid=(batch, nt),
        in_specs=[smem, smem, _resident(bucket.shape),
                  pl.BlockSpec((SWA_TQ, D_MODEL), lambda b, t: (b * nt + t, Z_SQ // D_MODEL)),
                  pl.BlockSpec((SWA_TQ, 256), lambda b, t: (b * nt + t, kv_col)),
                  pl.BlockSpec((SWA_BLOCK, 256),
                               lambda b, t: ((b * nt + t) * SWA_NQ - jnp.minimum(t, 1), kv_col)),
                  _resident(mem2d.shape), _resident(w_mem_kv.shape)],
        out_specs=[pl.BlockSpec((SWA_TQ, D_MODEL), lambda b, t: (b * nt + t, 0)),
                   pl.BlockSpec(mkv_shape, lambda b, t: (0, 0))],
        scratch_shapes=[pltpu.VMEM((2 * SWA_HEADS, SWA_BLOCK, SWA_BLOCK), F32),
                        pltpu.VMEM((2 * SWA_KV_HEADS, SWA_TQ + SWA_BLOCK, 128), BF16),
                        pltpu.VMEM((2 * SWA_KV_HEADS, SWA_TQ + SWA_BLOCK, 256), BF16)],
        compiler_params=pltpu.CompilerParams(
            dimension_semantics=("arbitrary", "arbitrary"), vmem_limit_bytes=VMEM_LIMIT_BYTES),
        name="swa",
    )(rel_bias, sinks, bucket, z, z, z, mem2d, w_mem_kv)


TAIL_TM = 512
TAIL_SUB = 256
TAIL_TF = 1024


def _layer_norm(y, g, b):
    mu = jnp.mean(y, axis=-1, keepdims=True)
    yc = y - mu
    var = jnp.mean(yc * yc, axis=-1, keepdims=True)
    return yc * lax.rsqrt(var + LN_EPS) * g + b


def _tail_kernel(alpha, x_ref, oa_ref, ob_ref, mq_ref, gl_ref, kv_ref, wa_ref, wb_ref, wc_ref, wo_ref,
                 g1_ref, b1_ref, wu_ref, wd_ref, g2_ref, b2_ref, o_ref):
    subs = [slice(r, r + TAIL_SUB) for r in range(0, TAIL_TM, TAIL_SUB)]
    mem_w = MEM_HEADS * MEM_HEAD_DIM
    hcols = [slice(h * MEM_HEAD_DIM, (h + 1) * MEM_HEAD_DIM) for h in range(MEM_HEADS)]

    def gate(rows, i):
        return jax.nn.sigmoid(gl_ref[rows, i * D_MODEL:(i + 1) * D_MODEL].astype(F32))

    scores = [[lax.dot_general(mq_ref[rows, c], kv_ref[:, c], _NT, preferred_element_type=F32) for c in hcols]
              for rows in subs]
    merged = [gate(rows, 0) * _dot(oa_ref[rows, :], wa_ref[...]) for rows in subs]
    probs = []
    for s_heads in scores:
        ps = []
        for s in s_heads:
            p = jnp.exp(s - jnp.max(s, axis=-1, keepdims=True))
            ps.append((p * (1.0 / jnp.sum(p, axis=-1, keepdims=True))).astype(BF16))
        probs.append(ps)
    o_c = [jnp.concatenate([_dot(p, kv_ref[:, mem_w + h * MEM_HEAD_DIM:mem_w + (h + 1) * MEM_HEAD_DIM]).astype(BF16)
                            for h, p in enumerate(ps)], axis=1) for ps in probs]
    merged = [m + gate(rows, 1) * _dot(ob_ref[rows, :], wb_ref[...]) for m, rows in zip(merged, subs)]
    merged = [m + gate(rows, 2) * _dot(oc, wc_ref[...]) for m, rows, oc in zip(merged, subs, o_c)]

    h1 = []
    for m, rows in zip(merged, subs):
        mix = _dot(m.astype(BF16), wo_ref[...])
        h1.append(_layer_norm(alpha * x_ref[rows, :] + mix, g1_ref[...], b1_ref[...]))

    hb = [h.astype(BF16) for h in h1]
    ff = [None] * len(subs)
    for f0 in range(0, D_FF, TAIL_TF):
        us = [jnp.maximum(_dot(hb[i], wu_ref[:, f0:f0 + TAIL_TF]), 0.0) for i in range(len(subs))]
        for i, u in enumerate(us):
            d = _dot((u * u).astype(BF16), wd_ref[f0:f0 + TAIL_TF, :])
            ff[i] = d if ff[i] is None else ff[i] + d
    for i, rows in enumerate(subs):
        o_ref[rows, :] = _layer_norm(alpha * h1[i] + ff[i], g2_ref[...], b2_ref[...])


def _tail(alpha, x2d, o_a, o_b, z, kv, wa, wb, wc, wo, ln1_g, ln1_b, wu, wd, ln2_g, ln2_b, batch, seq):
    t = x2d.shape[0]
    per_batch = seq // TAIL_TM
    row = lambda i: (i, 0)
    weights = (wa, wb, wc, wo, ln1_g, ln1_b, wu, wd, ln2_g, ln2_b)
    return pl.pallas_call(
        functools.partial(_tail_kernel, alpha),
        out_shape=jax.ShapeDtypeStruct((t, D_MODEL), F32),
        grid=(t // TAIL_TM,),
        in_specs=[pl.BlockSpec((TAIL_TM, D_MODEL), row),
                  pl.BlockSpec((TAIL_TM, D_MODEL), row),
                  pl.BlockSpec((TAIL_TM, D_MODEL), row),
                  pl.BlockSpec((TAIL_TM, D_MODEL), lambda i: (i, Z_MQ // D_MODEL)),
                  pl.BlockSpec((TAIL_TM, 3 * D_MODEL), lambda i: (i, Z_GL // (3 * D_MODEL))),
                  pl.BlockSpec((kv.shape[0] // batch, kv.shape[1]), lambda i: (i // per_batch, 0))]
                 + [_resident(w.shape) for w in weights],
        out_specs=pl.BlockSpec((TAIL_TM, D_MODEL), row),
        compiler_params=pltpu.CompilerParams(
            dimension_semantics=("arbitrary",), vmem_limit_bytes=VMEM_LIMIT_BYTES),
        name="tail",
    )(x2d, o_a, o_b, z, z, kv, *weights)


def kernel(x, mem, w_in, lb_logits, hg_norm_gain, swa_sinks, rel_bias, w_mem_kv, w_branch_hg, w_branch_swa,
           w_branch_mem, w_out, ln1_g, ln1_b, w_up, w_down, ln2_g, ln2_b):
    batch, seq, _ = x.shape
    depth = w_in.shape[0]
    assert depth == 1 and lb_logits.shape[0] == depth + 1
    alpha = (2.0 * depth) ** 0.25
    out_dtype = x.dtype

    w_in_b = w_in[0].astype(BF16)
    x2d = x.reshape(batch * seq, D_MODEL).astype(F32)
    later = [w[0].astype(F32) for w in (w_branch_hg, w_branch_swa, w_branch_mem, w_out, w_up, w_down)]
    z, z_lo, z_k, (wa, wb, wc, wo, wu, wd) = _proj_in(
        x2d, w_in_b, lb_logits.astype(F32), hg_norm_gain.astype(F32), later)
    o_a = _hgrn(z, z_lo, z_k, batch, seq)
    o_b, kv = _swa(z, rel_bias.astype(F32), swa_sinks[0].astype(F32),
                   mem.reshape(batch * mem.shape[1], D_MODEL).astype(F32), w_mem_kv[0].astype(F32), batch, seq)
    h2 = _tail(alpha, x2d, o_a, o_b, z, kv, wa, wb, wc, wo, ln1_g.astype(F32), ln1_b.astype(F32),
               wu, wd, ln2_g.astype(F32), ln2_b.astype(F32), batch, seq)
    return h2.reshape(batch, seq, D_MODEL).astype(out_dtype)
```

```python
import functools
import math

import numpy as np
import jax
import jax.numpy as jnp
from jax import lax
from jax.experimental import pallas as pl
from jax.experimental.pallas import tpu as pltpu

F32 = jnp.float32
BF16 = jnp.bfloat16

D_MODEL = 1024
HG_HEADS = 8
HG_DK = 128
HG_DV = 128
HG_CHUNK = 64
SWA_HEADS = 16
SWA_KV_HEADS = 2
SWA_GROUP = SWA_HEADS // SWA_KV_HEADS
SWA_HEAD_DIM = 64
SWA_BLOCK = 128
MEM_HEADS = 4
MEM_HEAD_DIM = 256
NUM_BUCKETS = 32
MAX_DISTANCE = 128
D_FF = 4 * D_MODEL
LN_EPS = 1e-5
RMS_EPS = 1e-6
LOG2E = 1.4426950408889634

Z_HQ, Z_HF, Z_HI, Z_HG, Z_SQ, Z_MQ = (i * D_MODEL for i in range(6))
Z_GL = 6 * D_MODEL
Z_KV = 9 * D_MODEL
Z_COLS = Z_KV + 2 * SWA_KV_HEADS * SWA_HEAD_DIM

VMEM_LIMIT_BYTES = 56 * 1024 * 1024

_NT = (((1,), (1,)), ((), ()))
_TN = (((0,), (0,)), ((), ()))


def _dot(a, b):
    return jnp.dot(a, b, preferred_element_type=F32)


def _resident(shape):
    nd = len(shape)
    return pl.BlockSpec(shape, lambda *_: (0,) * nd, pipeline_mode=pl.Buffered(1))


PROJ_TM = 512
PROJ_TN = 1024


def _proj_col_chunks():
    kv_w = 2 * SWA_KV_HEADS * SWA_HEAD_DIM
    src_kv = Z_MQ
    segments = ((0, 0, Z_MQ), (src_kv + kv_w, Z_MQ, Z_KV - Z_MQ), (src_kv, Z_KV, kv_w))
    chunks = []
    for src, dst, width in segments:
        for off in range(0, width, PROJ_TN):
            chunks.append((src + off, dst + off, min(PROJ_TN, width - off)))
    return chunks


def _proj_in_kernel(n_cast, x_ref, w_ref, lbl_ref, gain_ref, *refs):
    cast_in, (z_ref, lo_ref, k_ref), cast_out = refs[:n_cast], refs[n_cast:n_cast + 3], refs[n_cast + 3:]
    l0 = lbl_ref[0:1, :]
    l1 = lbl_ref[1:2, :]
    lmax = jnp.maximum(l0, l1)
    e0 = jnp.exp(l0 - lmax)
    e1 = jnp.exp(l1 - lmax)
    lb = e0 / (e0 + e1)
    f_mid = 0.5 * (1.0 + lb)
    f_amp = 0.5 * (1.0 - lb)
    gain_s = gain_ref[...] * (HG_DV ** 0.5)

    xb = x_ref[...].astype(BF16)
    for src, dst, width in _proj_col_chunks():
        res = _dot(xb, w_ref[:, src:src + width])
        if dst == Z_HF:
            f = f_mid + f_amp * jnp.tanh(0.5 * res)
            logf = jnp.log(f)
            hi = logf.astype(BF16)
            z_ref[:, dst:dst + width] = hi
            lo_ref[...] = (logf - hi.astype(F32)).astype(BF16)
            k_ref[...] = 1.0 - f
        elif dst == Z_HG:
            hg = 0.5 * res
            z_ref[:, dst:dst + width] = (gain_s * (hg + hg * jnp.tanh(hg))).astype(BF16)
        else:
            z_ref[:, dst:dst + width] = res.astype(BF16)
    for src_ref, dst_ref in zip(cast_in, cast_out):
        dst_ref[...] = src_ref[...].astype(BF16)


def _proj_in(x2d, w_in_b, lb_logits, gain, later_weights):
    t = x2d.shape[0]
    steps = t // PROJ_TM
    slab = lambda w: pl.BlockSpec((w.shape[0] // steps, w.shape[1]), lambda i: (i, 0))
    row = lambda width: pl.BlockSpec((PROJ_TM, width), lambda i: (i, 0))
    outs = pl.pallas_call(
        functools.partial(_proj_in_kernel, len(later_weights)),
        out_shape=[jax.ShapeDtypeStruct((t, Z_COLS), BF16), jax.ShapeDtypeStruct((t, D_MODEL), BF16),
                   jax.ShapeDtypeStruct((t, D_MODEL), F32)]
                  + [jax.ShapeDtypeStruct(w.shape, BF16) for w in later_weights],
        grid=(steps,),
        in_specs=[row(D_MODEL), _resident((D_MODEL, Z_COLS)), _resident(lb_logits.shape), _resident(gain.shape)]
                 + [slab(w) for w in later_weights],
        out_specs=[row(Z_COLS), row(D_MODEL), row(D_MODEL)] + [slab(w) for w in later_weights],
        compiler_params=pltpu.CompilerParams(
            dimension_semantics=("arbitrary",), vmem_limit_bytes=VMEM_LIMIT_BYTES),
        name="proj_in",
    )(x2d, w_in_b, lb_logits, gain, *later_weights)
    return outs[0], outs[1], outs[2], outs[3:]


HG_TB = 512
HG_GROUP = 2


def _hgrn_kernel(tril_ref, zq_ref, zhi_ref, zlo_ref, zk_ref, zi_ref, zg_ref, o_ref, st_ref, stb_ref):
    @pl.when(pl.program_id(1) == 0)
    def _():
        st_ref[...] = jnp.zeros_like(st_ref)
        stb_ref[...] = jnp.zeros_like(stb_ref)

    tril2 = tril_ref[...]
    causal = tril2[:, :HG_CHUNK] > 0
    heads = range(HG_HEADS)
    hcols = [slice(h * HG_DK, (h + 1) * HG_DK) for h in heads]

    def chunk_group(g, carry):
        rows_of = [pl.ds(pl.multiple_of((g * HG_GROUP + j) * HG_CHUNK, HG_CHUNK), HG_CHUNK)
                   for j in range(HG_GROUP)]
        units = [(rows, h) for rows in rows_of for h in heads]

        bs = [_dot(tril2, jnp.concatenate([zhi_ref[rows, hcols[h]], zlo_ref[rows, hcols[h]]], axis=0))
              for rows, h in units]

        attns, upds, qins, ebls = [], [], [], []
        for (rows, h), b in zip(units, bs):
            ebl = jnp.exp(b[HG_CHUNK - 1:HG_CHUNK, :])
            eb = jnp.exp2(b * LOG2E)
            q_in = (zq_ref[rows, hcols[h]].astype(F32) * eb).astype(BF16)
            kin = zk_ref[rows, hcols[h]] * (1.0 / eb)
            k_in = kin.astype(BF16)
            k_out = (kin * ebl).astype(BF16)
            attns.append(lax.dot_general(q_in, k_in, _NT, preferred_element_type=F32))
            upds.append(lax.dot_general(zi_ref[rows, hcols[h]], k_out, _TN, preferred_element_type=F32))
            qins.append(q_in)
            ebls.append(ebl)

        outs = []
        for u, (rows, h) in enumerate(units):
            attn = jnp.where(causal, attns[u], 0.0).astype(BF16)
            lhs = jnp.concatenate([qins[u], attn], axis=1)
            rhs = jnp.concatenate([stb_ref[h], zi_ref[rows, hcols[h]]], axis=0)
            outs.append(_dot(lhs, rhs))
            st = st_ref[h] * ebls[u] + upds[u]
            st_ref[h] = st
            stb_ref[h] = st.astype(BF16).T

        for (rows, h), o in zip(units, outs):
            ss = jnp.sum(o * o, axis=-1, keepdims=True)
            gate = zg_ref[rows, hcols[h]].astype(F32)
            o_ref[rows, hcols[h]] = (o * lax.rsqrt(ss + HG_DV * RMS_EPS) * gate).astype(BF16)
        return carry

    lax.fori_loop(0, HG_TB // (HG_CHUNK * HG_GROUP), chunk_group, 0, unroll=4)


def _hgrn(z, z_lo, z_k, batch, seq):
    nt = seq // HG_TB
    tril = np.tril(np.ones((HG_CHUNK, HG_CHUNK), np.float32))
    tril = jnp.asarray(np.concatenate([tril, tril], axis=1), BF16)
    tile = lambda b, t: (b * nt + t, 0)

    def zspec(col):
        return pl.BlockSpec((HG_TB, D_MODEL), lambda b, t, col=col: (b * nt + t, col // D_MODEL))

    return pl.pallas_call(
        _hgrn_kernel,
        out_shape=jax.ShapeDtypeStruct((batch * seq, D_MODEL), BF16),
        grid=(batch, nt),
        in_specs=[_resident(tril.shape), zspec(Z_HQ), zspec(Z_HF),
                  pl.BlockSpec((HG_TB, D_MODEL), tile), pl.BlockSpec((HG_TB, D_MODEL), tile),
                  zspec(Z_HI), zspec(Z_HG)],
        out_specs=pl.BlockSpec((HG_TB, D_MODEL), tile),
        scratch_shapes=[pltpu.VMEM((HG_HEADS, HG_DV, HG_DK), F32),
                        pltpu.VMEM((HG_HEADS, HG_DV, HG_DK), BF16)],
        compiler_params=pltpu.CompilerParams(
            dimension_semantics=("arbitrary", "arbitrary"), vmem_limit_bytes=VMEM_LIMIT_BYTES),
        name="hgrn",
    )(tril, z, z, z_lo, z_k, z, z)


def _swa_bucket_table():
    qi = np.arange(SWA_BLOCK)[:, None]
    kj = np.arange(SWA_BLOCK)[None, :]
    n = np.where(kj <= qi, qi - kj, qi + SWA_BLOCK - kj).astype(np.int32)
    max_exact = NUM_BUCKETS // 2
    nf = np.maximum(n, 1).astype(np.float32)
    large = max_exact + (np.log(nf / np.float32(max_exact)) / np.float32(math.log(MAX_DISTANCE / max_exact))
                         * np.float32(NUM_BUCKETS - max_exact)).astype(np.int32)
    large = np.minimum(large, NUM_BUCKETS - 1)
    return np.where(n < max_exact, n, large).astype(np.int32)


SWA_TQ = 512
SWA_NQ = SWA_TQ // SWA_BLOCK
SWA_MASKED = -1e30


def _swa_kernel(relb_ref, sink_ref, bucket_ref, q_ref, kvc_ref, kvp_ref, mem_ref, wkv_ref,
                o_ref, mkv_ref, bias_ref, kz_ref, vz_ref):
    t = pl.program_id(1)
    qi = lax.broadcasted_iota(jnp.int32, (SWA_BLOCK, SWA_BLOCK), 0)
    kj = lax.broadcasted_iota(jnp.int32, (SWA_BLOCK, SWA_BLOCK), 1)
    tri = kj <= qi

    @pl.when((pl.program_id(0) == 0) & (t == 0))
    def _():
        mem_w = MEM_HEADS * MEM_HEAD_DIM
        mb = mem_ref[...].astype(BF16)
        mkv_ref[:, :mem_w] = (_dot(mb, wkv_ref[:, :mem_w].astype(BF16)) * MEM_HEAD_DIM ** -0.5).astype(BF16)
        mkv_ref[:, mem_w:] = _dot(mb, wkv_ref[:, mem_w:].astype(BF16)).astype(BF16)

        bucket = bucket_ref[...]
        for h in range(SWA_HEADS):
            acc = jnp.zeros((SWA_BLOCK, SWA_BLOCK), F32)
            for bk in range(NUM_BUCKETS):
                acc = jnp.where(bucket == bk, relb_ref[bk, h], acc)
            bias_ref[h] = acc
            bias_ref[SWA_HEADS + h] = jnp.where(tri, acc, SWA_MASKED)
        vz_ref[:, :, 128:256] = jnp.ones((2 * SWA_KV_HEADS, SWA_TQ + SWA_BLOCK, 128), BF16)

    lane = lax.broadcasted_iota(jnp.int32, (SWA_BLOCK, 2 * SWA_HEAD_DIM), 1)
    low = lane < SWA_HEAD_DIM
    scale = SWA_HEAD_DIM ** -0.5
    for blk in range(SWA_NQ + 1):
        rows = slice(blk * SWA_BLOCK, (blk + 1) * SWA_BLOCK)
        if blk == 0:
            kk = kvp_ref[:, 0:128].astype(F32) * scale
            vv = kvp_ref[:, 128:256].astype(F32)
        else:
            src = slice((blk - 1) * SWA_BLOCK, blk * SWA_BLOCK)
            kk = kvc_ref[src, 0:128].astype(F32) * scale
            vv = kvc_ref[src, 128:256].astype(F32)
        kk_sw = pltpu.roll(kk, SWA_HEAD_DIM, 1)
        vv_sw = pltpu.roll(vv, SWA_HEAD_DIM, 1)
        for g in range(SWA_KV_HEADS):
            k_lo, k_hi = (kk, kk_sw) if g == 0 else (kk_sw, kk)
            v_lo, v_hi = (vv, vv_sw) if g == 0 else (vv_sw, vv)
            kz_ref[2 * g + 0, rows, :] = jnp.where(low, k_lo, 0.0).astype(BF16)
            kz_ref[2 * g + 1, rows, :] = jnp.where(low, 0.0, k_hi).astype(BF16)
            vz_ref[2 * g + 0, rows, 0:128] = jnp.where(low, v_lo, 0.0).astype(BF16)
            vz_ref[2 * g + 1, rows, 0:128] = jnp.where(low, 0.0, v_hi).astype(BF16)

    def sub_block(i, carry):
        r0 = pl.multiple_of(i * SWA_BLOCK, SWA_BLOCK)
        qrows = pl.ds(r0, SWA_BLOCK)
        krows = pl.ds(r0, 2 * SWA_BLOCK)
        bias_off = jnp.where((t == 0) & (i == 0), SWA_HEADS, 0)
        for pair in range(SWA_HEADS // 2):
            g = (2 * pair) // SWA_GROUP
            qp = q_ref[qrows, pair * 128:(pair + 1) * 128]
            o_pair = None
            for half in range(2):
                h = 2 * pair + half
                slot = 2 * g + half
                s2 = lax.dot_general(qp, kz_ref[slot, krows, :], _NT, preferred_element_type=F32)
                s = jnp.where(tri, s2[:, SWA_BLOCK:], s2[:, :SWA_BLOCK]) + bias_ref[bias_off + h]
                sink = sink_ref[h]
                m = jnp.maximum(jnp.max(s, axis=-1, keepdims=True), sink)
                p = jnp.exp(s - m)
                p2 = jnp.concatenate([jnp.where(tri, 0.0, p), jnp.where(tri, p, 0.0)], axis=1).astype(BF16)
                od = _dot(p2, vz_ref[slot, krows, :])
                den = od[:, 128:] + jnp.exp(sink - m)
                o_h = od[:, :128] * (1.0 / den)
                o_pair = o_h if o_pair is None else o_pair + o_h
            o_ref[qrows, pair * 128:(pair + 1) * 128] = o_pair.astype(BF16)
        return carry

    lax.fori_loop(0, SWA_NQ, sub_block, 0, unroll=2)


def _swa(z, rel_bias, sinks, mem2d, w_mem_kv, batch, seq):
    nt = seq // SWA_TQ
    bucket = jnp.asarray(_swa_bucket_table())
    kv_col = Z_KV // 256
    smem = pl.BlockSpec(memory_space=pltpu.SMEM)
    mkv_shape = (mem2d.shape[0], w_mem_kv.shape[1])
    return pl.pallas_call(
        _swa_kernel,
        out_shape=[jax.ShapeDtypeStruct((batch * seq, D_MODEL), BF16), jax.ShapeDtypeStruct(mkv_shape, BF16)],
        grid=(batch, nt),
        in_specs=[smem, smem, _resident(bucket.shape),
                  pl.BlockSpec((SWA_TQ, D_MODEL), lambda b, t: (b * nt + t, Z_SQ // D_MODEL)),
                  pl.BlockSpec((SWA_TQ, 256), lambda b, t: (b * nt + t, kv_col)),
                  pl.BlockSpec((SWA_BLOCK, 256),
                               lambda b, t: ((b * nt + t) * SWA_NQ - jnp.minimum(t, 1), kv_col)),
                  _resident(mem2d.shape), _resident(w_mem_kv.shape)],
        out_specs=[pl.BlockSpec((SWA_TQ, D_MODEL), lambda b, t: (b * nt + t, 0)),
                   pl.BlockSpec(mkv_shape, lambda b, t: (0, 0))],
        scratch_shapes=[pltpu.VMEM((2 * SWA_HEADS, SWA_BLOCK, SWA_BLOCK), F32),
                        pltpu.VMEM((2 * SWA_KV_HEADS, SWA_TQ + SWA_BLOCK, 128), BF16),
                        pltpu.VMEM((2 * SWA_KV_HEADS, SWA_TQ + SWA_BLOCK, 256), BF16)],
        compiler_params=pltpu.CompilerParams(
            dimension_semantics=("arbitrary", "arbitrary"), vmem_limit_bytes=VMEM_LIMIT_BYTES),
        name="swa",
    )(rel_bias, sinks, bucket, z, z, z, mem2d, w_mem_kv)


TAIL_TM = 512
TAIL_SUB = 256
TAIL_TF = 1024


def _layer_norm(y, g, b):
    mu = jnp.mean(y, axis=-1, keepdims=True)
    yc = y - mu
    var = jnp.mean(yc * yc, axis=-1, keepdims=True)
    return yc * lax.rsqrt(var + LN_EPS) * g + b


def _tail_kernel(alpha, x_ref, oa_ref, ob_ref, mq_ref, gl_ref, kv_ref, wa_ref, wb_ref, wc_ref, wo_ref,
                 g1_ref, b1_ref, wu_ref, wd_ref, g2_ref, b2_ref, o_ref):
    subs = [slice(r, r + TAIL_SUB) for r in range(0, TAIL_TM, TAIL_SUB)]
    mem_w = MEM_HEADS * MEM_HEAD_DIM
    hcols = [slice(h * MEM_HEAD_DIM, (h + 1) * MEM_HEAD_DIM) for h in range(MEM_HEADS)]

    def gate(rows, i):
        return jax.nn.sigmoid(gl_ref[rows, i * D_MODEL:(i + 1) * D_MODEL].astype(F32))

    scores = [[lax.dot_general(mq_ref[rows, c], kv_ref[:, c], _NT, preferred_element_type=F32) for c in hcols]
              for rows in subs]
    merged = [gate(rows, 0) * _dot(oa_ref[rows, :], wa_ref[...]) for rows in subs]
    probs = []
    for s_heads in scores:
        ps = []
        for s in s_heads:
            p = jnp.exp(s - jnp.max(s, axis=-1, keepdims=True))
            ps.append((p * (1.0 / jnp.sum(p, axis=-1, keepdims=True))).astype(BF16))
        probs.append(ps)
    o_c = [jnp.concatenate([_dot(p, kv_ref[:, mem_w + h * MEM_HEAD_DIM:mem_w + (h + 1) * MEM_HEAD_DIM]).astype(BF16)
                            for h, p in enumerate(ps)], axis=1) for ps in probs]
    merged = [m + gate(rows, 1) * _dot(ob_ref[rows, :], wb_ref[...]) for m, rows in zip(merged, subs)]
    merged = [m + gate(rows, 2) * _dot(oc, wc_ref[...]) for m, rows, oc in zip(merged, subs, o_c)]

    h1 = []
    for m, rows in zip(merged, subs):
        mix = _dot(m.astype(BF16), wo_ref[...])
        h1.append(_layer_norm(alpha * x_ref[rows, :] + mix, g1_ref[...], b1_ref[...]))

    hb = [h.astype(BF16) for h in h1]
    ff = [None] * len(subs)
    for f0 in range(0, D_FF, TAIL_TF):
        us = [jnp.maximum(_dot(hb[i], wu_ref[:, f0:f0 + TAIL_TF]), 0.0) for i in range(len(subs))]
        for i, u in enumerate(us):
            d = _dot((u * u).astype(BF16), wd_ref[f0:f0 + TAIL_TF, :])
            ff[i] = d if ff[i] is None else ff[i] + d
    for i, rows in enumerate(subs):
        o_ref[rows, :] = _layer_norm(alpha * h1[i] + ff[i], g2_ref[...], b2_ref[...])


def _tail(alpha, x2d, o_a, o_b, z, kv, wa, wb, wc, wo, ln1_g, ln1_b, wu, wd, ln2_g, ln2_b, batch, seq):
    t = x2d.shape[0]
    per_batch = seq // TAIL_TM
    row = lambda i: (i, 0)
    weights = (wa, wb, wc, wo, ln1_g, ln1_b, wu, wd, ln2_g, ln2_b)
    return pl.pallas_call(
        functools.partial(_tail_kernel, alpha),
        out_shape=jax.ShapeDtypeStruct((t, D_MODEL), F32),
        grid=(t // TAIL_TM,),
        in_specs=[pl.BlockSpec((TAIL_TM, D_MODEL), row),
                  pl.BlockSpec((TAIL_TM, D_MODEL), row),
                  pl.BlockSpec((TAIL_TM, D_MODEL), row),
                  pl.BlockSpec((TAIL_TM, D_MODEL), lambda i: (i, Z_MQ // D_MODEL)),
                  pl.BlockSpec((TAIL_TM, 3 * D_MODEL), lambda i: (i, Z_GL // (3 * D_MODEL))),
                  pl.BlockSpec((kv.shape[0] // batch, kv.shape[1]), lambda i: (i // per_batch, 0))]
                 + [_resident(w.shape) for w in weights],
        out_specs=pl.BlockSpec((TAIL_TM, D_MODEL), row),
        compiler_params=pltpu.CompilerParams(
            dimension_semantics=("arbitrary",), vmem_limit_bytes=VMEM_LIMIT_BYTES),
        name="tail",
    )(x2d, o_a, o_b, z, z, kv, *weights)


def kernel(x, mem, w_in, lb_logits, hg_norm_gain, swa_sinks, rel_bias, w_mem_kv, w_branch_hg, w_branch_swa,
           w_branch_mem, w_out, ln1_g, ln1_b, w_up, w_down, ln2_g, ln2_b):
    batch, seq, _ = x.shape
    depth = w_in.shape[0]
    assert depth == 1 and lb_logits.shape[0] == depth + 1
    alpha = (2.0 * depth) ** 0.25
    out_dtype = x.dtype

    w_in_b = w_in[0].astype(BF16)
    x2d = x.reshape(batch * seq, D_MODEL).astype(F32)
    later = [w[0].astype(F32) for w in (w_branch_hg, w_branch_swa, w_branch_mem, w_out, w_up, w_down)]
    z, z_lo, z_k, (wa, wb, wc, wo, wu, wd) = _proj_in(
        x2d, w_in_b, lb_logits.astype(F32), hg_norm_gain.astype(F32), later)
    o_a = _hgrn(z, z_lo, z_k, batch, seq)
    o_b, kv = _swa(z, rel_bias.astype(F32), swa_sinks[0].astype(F32),
                   mem.reshape(batch * mem.shape[1], D_MODEL).astype(F32), w_mem_kv[0].astype(F32), batch, seq)
    h2 = _tail(alpha, x2d, o_a, o_b, z, kv, wa, wb, wc, wo, ln1_g.astype(F32), ln1_b.astype(F32),
               wu, wd, ln2_g.astype(F32), ln2_b.astype(F32), batch, seq)
    return h2.reshape(batch, seq, D_MODEL).astype(out_dtype)
```

```python
import functools
import math

import numpy as np
import jax
import jax.numpy as jnp
from jax import lax
from jax.experimental import pallas as pl
from jax.experimental.pallas import tpu as pltpu

F32 = jnp.float32
BF16 = jnp.bfloat16

D_MODEL = 1024
HG_HEADS = 8
HG_DK = 128
HG_DV = 128
HG_CHUNK = 64
SWA_HEADS = 16
SWA_KV_HEADS = 2
SWA_GROUP = SWA_HEADS // SWA_KV_HEADS
SWA_HEAD_DIM = 64
SWA_BLOCK = 128
MEM_HEADS = 4
MEM_HEAD_DIM = 256
NUM_BUCKETS = 32
MAX_DISTANCE = 128
D_FF = 4 * D_MODEL
LN_EPS = 1e-5
RMS_EPS = 1e-6
LOG2E = 1.4426950408889634

Z_HQ, Z_HF, Z_HI, Z_HG, Z_SQ, Z_MQ = (i * D_MODEL for i in range(6))
Z_GL = 6 * D_MODEL
Z_KV = 9 * D_MODEL
Z_COLS = Z_KV + 2 * SWA_KV_HEADS * SWA_HEAD_DIM

VMEM_LIMIT_BYTES = 56 * 1024 * 1024

_NT = (((1,), (1,)), ((), ()))
_TN = (((0,), (0,)), ((), ()))


def _dot(a, b):
    return jnp.dot(a, b, preferred_element_type=F32)


def _resident(shape):
    nd = len(shape)
    return pl.BlockSpec(shape, lambda *_: (0,) * nd, pipeline_mode=pl.Buffered(1))


PROJ_TM = 512
PROJ_TN = 1024


def _proj_col_chunks():
    kv_w = 2 * SWA_KV_HEADS * SWA_HEAD_DIM
    src_kv = Z_MQ
    segments = ((0, 0, Z_MQ), (src_kv + kv_w, Z_MQ, Z_KV - Z_MQ), (src_kv, Z_KV, kv_w))
    chunks = []
    for src, dst, width in segments:
        for off in range(0, width, PROJ_TN):
            chunks.append((src + off, dst + off, min(PROJ_TN, width - off)))
    return chunks


def _proj_in_kernel(n_cast, x_ref, w_ref, lbl_ref, gain_ref, *refs):
    cast_in, (z_ref, lo_ref, k_ref), cast_out = refs[:n_cast], refs[n_cast:n_cast + 3], refs[n_cast + 3:]
    l0 = lbl_ref[0:1, :]
    l1 = lbl_ref[1:2, :]
    lmax = jnp.maximum(l0, l1)
    e0 = jnp.exp(l0 - lmax)
    e1 = jnp.exp(l1 - lmax)
    lb = e0 / (e0 + e1)
    f_mid = 0.5 * (1.0 + lb)
    f_amp = 0.5 * (1.0 - lb)
    gain_s = gain_ref[...] * (HG_DV ** 0.5)

    xb = x_ref[...].astype(BF16)
    for src, dst, width in _proj_col_chunks():
        res = _dot(xb, w_ref[:, src:src + width])
        if dst == Z_HF:
            f = f_mid + f_amp * jnp.tanh(0.5 * res)
            logf = jnp.log(f)
            hi = logf.astype(BF16)
            z_ref[:, dst:dst + width] = hi
            lo_ref[...] = (logf - hi.astype(F32)).astype(BF16)
            k_ref[...] = 1.0 - f
        elif dst == Z_HG:
            hg = 0.5 * res
            z_ref[:, dst:dst + width] = (gain_s * (hg + hg * jnp.tanh(hg))).astype(BF16)
        else:
            z_ref[:, dst:dst + width] = res.astype(BF16)
    for src_ref, dst_ref in zip(cast_in, cast_out):
        dst_ref[...] = src_ref[...].astype(BF16)


def _proj_in(x2d, w_in_b, lb_logits, gain, later_weights):
    t = x2d.shape[0]
    steps = t // PROJ_TM
    slab = lambda w: pl.BlockSpec((w.shape[0] // steps, w.shape[1]), lambda i: (i, 0))
    row = lambda width: pl.BlockSpec((PROJ_TM, width), lambda i: (i, 0))
    outs = pl.pallas_call(
        functools.partial(_proj_in_kernel, len(later_weights)),
        out_shape=[jax.ShapeDtypeStruct((t, Z_COLS), BF16), jax.ShapeDtypeStruct((t, D_MODEL), BF16),
                   jax.ShapeDtypeStruct((t, D_MODEL), F32)]
                  + [jax.ShapeDtypeStruct(w.shape, BF16) for w in later_weights],
        grid=(steps,),
        in_specs=[row(D_MODEL), _resident((D_MODEL, Z_COLS)), _resident(lb_logits.shape), _resident(gain.shape)]
                 + [slab(w) for w in later_weights],
        out_specs=[row(Z_COLS), row(D_MODEL), row(D_MODEL)] + [slab(w) for w in later_weights],
        compiler_params=pltpu.CompilerParams(
            dimension_semantics=("arbitrary",), vmem_limit_bytes=VMEM_LIMIT_BYTES),
        name="proj_in",
    )(x2d, w_in_b, lb_logits, gain, *later_weights)
    return outs[0], outs[1], outs[2], outs[3:]


HG_TB = 512
HG_GROUP = 2


def _hgrn_kernel(tril_ref, zq_ref, zhi_ref, zlo_ref, zk_ref, zi_ref, zg_ref, o_ref, st_ref, stb_ref):
    @pl.when(pl.program_id(1) == 0)
    def _():
        st_ref[...] = jnp.zeros_like(st_ref)
        stb_ref[...] = jnp.zeros_like(stb_ref)

    tril2 = tril_ref[...]
    causal = tril2[:, :HG_CHUNK] > 0
    heads = range(HG_HEADS)
    hcols = [slice(h * HG_DK, (h + 1) * HG_DK) for h in heads]

    def chunk_group(g, carry):
        rows_of = [pl.ds(pl.multiple_of((g * HG_GROUP + j) * HG_CHUNK, HG_CHUNK), HG_CHUNK)
                   for j in range(HG_GROUP)]
        units = [(rows, h) for rows in rows_of for h in heads]

        bs = [_dot(tril2, jnp.concatenate([zhi_ref[rows, hcols[h]], zlo_ref[rows, hcols[h]]], axis=0))
              for rows, h in units]

        attns, upds, qins, ebls = [], [], [], []
        for (rows, h), b in zip(units, bs):
            ebl = jnp.exp(b[HG_CHUNK - 1:HG_CHUNK, :])
            eb = jnp.exp2(b * LOG2E)
            q_in = (zq_ref[rows, hcols[h]].astype(F32) * eb).astype(BF16)
            kin = zk_ref[rows, hcols[h]] * (1.0 / eb)
            k_in = kin.astype(BF16)
            k_out = (kin * ebl).astype(BF16)
            attns.append(lax.dot_general(q_in, k_in, _NT, preferred_element_type=F32))
            upds.append(lax.dot_general(zi_ref[rows, hcols[h]], k_out, _TN, preferred_element_type=F32))
            qins.append(q_in)
            ebls.append(ebl)

        outs = []
        for u, (rows, h) in enumerate(units):
            attn = jnp.where(causal, attns[u], 0.0).astype(BF16)
            lhs = jnp.concatenate([qins[u], attn], axis=1)
            rhs = jnp.concatenate([stb_ref[h], zi_ref[rows, hcols[h]]], axis=0)
            outs.append(_dot(lhs, rhs))
            st = st_ref[h] * ebls[u] + upds[u]
            st_ref[h] = st
            stb_ref[h] = st.astype(BF16).T

        for (rows, h), o in zip(units, outs):
            ss = jnp.sum(o * o, axis=-1, keepdims=True)
            gate = zg_ref[rows, hcols[h]].astype(F32)
            o_ref[rows, hcols[h]] = (o * lax.rsqrt(ss + HG_DV * RMS_EPS) * gate).astype(BF16)
        return carry

    lax.fori_loop(0, HG_TB // (HG_CHUNK * HG_GROUP), chunk_group, 0, unroll=4)


def _hgrn(z, z_lo, z_k, batch, seq):
    nt = seq // HG_TB
    tril = np.tril(np.ones((HG_CHUNK, HG_CHUNK), np.float32))
    tril = jnp.asarray(np.concatenate([tril, tril], axis=1), BF16)
    tile = lambda b, t: (b * nt + t, 0)

    def zspec(col):
        return pl.BlockSpec((HG_TB, D_MODEL), lambda b, t, col=col: (b * nt + t, col // D_MODEL))

    return pl.pallas_call(
        _hgrn_kernel,
        out_shape=jax.ShapeDtypeStruct((batch * seq, D_MODEL), BF16),
        grid=(batch, nt),
        in_specs=[_resident(tril.shape), zspec(Z_HQ), zspec(Z_HF),
                  pl.BlockSpec((HG_TB, D_MODEL), tile), pl.BlockSpec((HG_TB, D_MODEL), tile),
                  zspec(Z_HI), zspec(Z_HG)],
        out_specs=pl.BlockSpec((HG_TB, D_MODEL), tile),
        scratch_shapes=[pltpu.VMEM((HG_HEADS, HG_DV, HG_DK), F32),
                        pltpu.VMEM((HG_HEADS, HG_DV, HG_DK), BF16)],
        compiler_params=pltpu.CompilerParams(
            dimension_semantics=("arbitrary", "arbitrary"), vmem_limit_bytes=VMEM_LIMIT_BYTES),
        name="hgrn",
    )(tril, z, z, z_lo, z_k, z, z)


def _swa_bucket_table():
    qi = np.arange(SWA_BLOCK)[:, None]
    kj = np.arange(SWA_BLOCK)[None, :]
    n = np.where(kj <= qi, qi - kj, qi + SWA_BLOCK - kj).astype(np.int32)
    max_exact = NUM_BUCKETS // 2
    nf = np.maximum(n, 1).astype(np.float32)
    large = max_exact + (np.log(nf / np.float32(max_exact)) / np.float32(math.log(MAX_DISTANCE / max_exact))
                         * np.float32(NUM_BUCKETS - max_exact)).astype(np.int32)
    large = np.minimum(large, NUM_BUCKETS - 1)
    return np.where(n < max_exact, n, large).astype(np.int32)


SWA_TQ = 512
SWA_NQ = SWA_TQ // SWA_BLOCK
SWA_MASKED = -1e30


def _swa_kernel(relb_ref, sink_ref, bucket_ref, q_ref, kvc_ref, kvp_ref, mem_ref, wkv_ref,
                o_ref, mkv_ref, bias_ref, kz_ref, vz_ref):
    t = pl.program_id(1)
    qi = lax.broadcasted_iota(jnp.int32, (SWA_BLOCK, SWA_BLOCK), 0)
    kj = lax.broadcasted_iota(jnp.int32, (SWA_BLOCK, SWA_BLOCK), 1)
    tri = kj <= qi

    @pl.when((pl.program_id(0) == 0) & (t == 0))
    def _():
        mem_w = MEM_HEADS * MEM_HEAD_DIM
        mb = mem_ref[...].astype(BF16)
        mkv_ref[:, :mem_w] = (_dot(mb, wkv_ref[:, :mem_w].astype(BF16)) * MEM_HEAD_DIM ** -0.5).astype(BF16)
        mkv_ref[:, mem_w:] = _dot(mb, wkv_ref[:, mem_w:].astype(BF16)).astype(BF16)

        bucket = bucket_ref[...]
        for h in range(SWA_HEADS):
            acc = jnp.zeros((SWA_BLOCK, SWA_BLOCK), F32)
            for bk in range(NUM_BUCKETS):
                acc = jnp.where(bucket == bk, relb_ref[bk, h], acc)
            bias_ref[h] = acc
            bias_ref[SWA_HEADS + h] = jnp.where(tri, acc, SWA_MASKED)
        vz_ref[:, :, 128:256] = jnp.ones((2 * SWA_KV_HEADS, SWA_TQ + SWA_BLOCK, 128), BF16)

    lane = lax.broadcasted_iota(jnp.int32, (SWA_BLOCK, 2 * SWA_HEAD_DIM), 1)
    low = lane < SWA_HEAD_DIM
    scale = SWA_HEAD_DIM ** -0.5
    for blk in range(SWA_NQ + 1):
        rows = slice(blk * SWA_BLOCK, (blk + 1) * SWA_BLOCK)
        if blk == 0:
            kk = kvp_ref[:, 0:128].astype(F32) * scale
            vv = kvp_ref[:, 128:256].astype(F32)
        else:
            src = slice((blk - 1) * SWA_BLOCK, blk * SWA_BLOCK)
            kk = kvc_ref[src, 0:128].astype(F32) * scale
            vv = kvc_ref[src, 128:256].astype(F32)
        kk_sw = pltpu.roll(kk, SWA_HEAD_DIM, 1)
        vv_sw = pltpu.roll(vv, SWA_HEAD_DIM, 1)
        for g in range(SWA_KV_HEADS):
            k_lo, k_hi = (kk, kk_sw) if g == 0 else (kk_sw, kk)
            v_lo, v_hi = (vv, vv_sw) if g == 0 else (vv_sw, vv)
            kz_ref[2 * g + 0, rows, :] = jnp.where(low, k_lo, 0.0).astype(BF16)
            kz_ref[2 * g + 1, rows, :] = jnp.where(low, 0.0, k_hi).astype(BF16)
            vz_ref[2 * g + 0, rows, 0:128] = jnp.where(low, v_lo, 0.0).astype(BF16)
            vz_ref[2 * g + 1, rows, 0:128] = jnp.where(low, 0.0, v_hi).astype(BF16)

    def sub_block(i, carry):
        r0 = pl.multiple_of(i * SWA_BLOCK, SWA_BLOCK)
        qrows = pl.ds(r0, SWA_BLOCK)
        krows = pl.ds(r0, 2 * SWA_BLOCK)
        bias_off = jnp.where((t == 0) & (i == 0), SWA_HEADS, 0)
        for pair in range(SWA_HEADS // 2):
            g = (2 * pair) // SWA_GROUP
            qp = q_ref[qrows, pair * 128:(pair + 1) * 128]
            o_pair = None
            for half in range(2):
                h = 2 * pair + half
                slot = 2 * g + half
                s2 = lax.dot_general(qp, kz_ref[slot, krows, :], _NT, preferred_element_type=F32)
                s = jnp.where(tri, s2[:, SWA_BLOCK:], s2[:, :SWA_BLOCK]) + bias_ref[bias_off + h]
                sink = sink_ref[h]
                m = jnp.maximum(jnp.max(s, axis=-1, keepdims=True), sink)
                p = jnp.exp(s - m)
                p2 = jnp.concatenate([jnp.where(tri, 0.0, p), jnp.where(tri, p, 0.0)], axis=1).astype(BF16)
                od = _dot(p2, vz_ref[slot, krows, :])
                den = od[:, 128:] + jnp.exp(sink - m)
                o_h = od[:, :128] * (1.0 / den)
                o_pair = o_h if o_pair is None else o_pair + o_h
            o_ref[qrows, pair * 128:(pair + 1) * 128] = o_pair.astype(BF16)
        return carry

    lax.fori_loop(0, SWA_NQ, sub_block, 0, unroll=4)


def _swa(z, rel_bias, sinks, mem2d, w_mem_kv, batch, seq):
    nt = seq // SWA_TQ
    bucket = jnp.asarray(_swa_bucket_table())
    kv_col = Z_KV // 256
    smem = pl.BlockSpec(memory_space=pltpu.SMEM)
    mkv_shape = (mem2d.shape[0], w_mem_kv.shape[1])
    return pl.pallas_call(
        _swa_kernel,
        out_shape=[jax.ShapeDtypeStruct((batch * seq, D_MODEL), BF16), jax.ShapeDtypeStruct(mkv_shape, BF16)],
        grid=(batch, nt),
        in_specs=[smem, smem, _resident(bucket.shape),
                  pl.BlockSpec((SWA_TQ, D_MODEL), lambda b, t: (b * nt + t, Z_SQ // D_MODEL)),
                  pl.BlockSpec((SWA_TQ, 256), lambda b, t: (b * nt + t, kv_col)),
                  pl.BlockSpec((SWA_BLOCK, 256),
                               lambda b, t: ((b * nt + t) * SWA_NQ - jnp.minimum(t, 1), kv_col)),
                  _resident(mem2d.shape), _resident(w_mem_kv.shape)],
        out_specs=[pl.BlockSpec((SWA_TQ, D_MODEL), lambda b, t: (b * nt + t, 0)),
                   pl.BlockSpec(mkv_shape, lambda b, t: (0, 0))],
        scratch_shapes=[pltpu.VMEM((2 * SWA_HEADS, SWA_BLOCK, SWA_BLOCK), F32),
                        pltpu.VMEM((2 * SWA_KV_HEADS, SWA_TQ + SWA_BLOCK, 128), BF16),
                        pltpu.VMEM((2 * SWA_KV_HEADS, SWA_TQ + SWA_BLOCK, 256), BF16)],
        compiler_params=pltpu.CompilerParams(
            dimension_semantics=("arbitrary", "arbitrary"), vmem_limit_bytes=VMEM_LIMIT_BYTES),
        name="swa",
    )(rel_bias, sinks, bucket, z, z, z, mem2d, w_mem_kv)


TAIL_TM = 512
TAIL_SUB = 256
TAIL_TF = 1024


def _layer_norm(y, g, b):
    mu = jnp.mean(y, axis=-1, keepdims=True)
    yc = y - mu
    var = jnp.mean(yc * yc, axis=-1, keepdims=True)
    return yc * lax.rsqrt(var + LN_EPS) * g + b


def _tail_kernel(alpha, x_ref, oa_ref, ob_ref, mq_ref, gl_ref, kv_ref, wa_ref, wb_ref, wc_ref, wo_ref,
                 g1_ref, b1_ref, wu_ref, wd_ref, g2_ref, b2_ref, o_ref):
    subs = [slice(r, r + TAIL_SUB) for r in range(0, TAIL_TM, TAIL_SUB)]
    mem_w = MEM_HEADS * MEM_HEAD_DIM
    hcols = [slice(h * MEM_HEAD_DIM, (h + 1) * MEM_HEAD_DIM) for h in range(MEM_HEADS)]

    def gate(rows, i):
        return jax.nn.sigmoid(gl_ref[rows, i * D_MODEL:(i + 1) * D_MODEL].astype(F32))

    scores = [[lax.dot_general(mq_ref[rows, c], kv_ref[:, c], _NT, preferred_element_type=F32) for c in hcols]
              for rows in subs]
    merged = [gate(rows, 0) * _dot(oa_ref[rows, :], wa_ref[...]) for rows in subs]
    probs = []
    for s_heads in scores:
        ps = []
        for s in s_heads:
            p = jnp.exp(s - jnp.max(s, axis=-1, keepdims=True))
            ps.append((p * (1.0 / jnp.sum(p, axis=-1, keepdims=True))).astype(BF16))
        probs.append(ps)
    o_c = [jnp.concatenate([_dot(p, kv_ref[:, mem_w + h * MEM_HEAD_DIM:mem_w + (h + 1) * MEM_HEAD_DIM]).astype(BF16)
                            for h, p in enumerate(ps)], axis=1) for ps in probs]
    merged = [m + gate(rows, 1) * _dot(ob_ref[rows, :], wb_ref[...]) for m, rows in zip(merged, subs)]
    merged = [m + gate(rows, 2) * _dot(oc, wc_ref[...]) for m, rows, oc in zip(merged, subs, o_c)]

    h1 = []
    for m, rows in zip(merged, subs):
        mix = _dot(m.astype(BF16), wo_ref[...])
        h1.append(_layer_norm(alpha * x_ref[rows, :] + mix, g1_ref[...], b1_ref[...]))

    hb = [h.astype(BF16) for h in h1]
    ff = [None] * len(subs)
    for f0 in range(0, D_FF, TAIL_TF):
        us = [jnp.maximum(_dot(hb[i], wu_ref[:, f0:f0 + TAIL_TF]), 0.0) for i in range(len(subs))]
        for i, u in enumerate(us):
            d = _dot((u * u).astype(BF16), wd_ref[f0:f0 + TAIL_TF, :])
            ff[i] = d if ff[i] is None else ff[i] + d
    for i, rows in enumerate(subs):
        o_ref[rows, :] = _layer_norm(alpha * h1[i] + ff[i], g2_ref[...], b2_ref[...])


def _tail(alpha, x2d, o_a, o_b, z, kv, wa, wb, wc, wo, ln1_g, ln1_b, wu, wd, ln2_g, ln2_b, batch, seq):
    t = x2d.shape[0]
    per_batch = seq // TAIL_TM
    row = lambda i: (i, 0)
    weights = (wa, wb, wc, wo, ln1_g, ln1_b, wu, wd, ln2_g, ln2_b)
    return pl.pallas_call(
        functools.partial(_tail_kernel, alpha),
        out_shape=jax.ShapeDtypeStruct((t, D_MODEL), F32),
        grid=(t // TAIL_TM,),
        in_specs=[pl.BlockSpec((TAIL_TM, D_MODEL), row),
                  pl.BlockSpec((TAIL_TM, D_MODEL), row),
                  pl.BlockSpec((TAIL_TM, D_MODEL), row),
                  pl.BlockSpec((TAIL_TM, D_MODEL), lambda i: (i, Z_MQ // D_MODEL)),
                  pl.BlockSpec((TAIL_TM, 3 * D_MODEL), lambda i: (i, Z_GL // (3 * D_MODEL))),
                  pl.BlockSpec((kv.shape[0] // batch, kv.shape[1]), lambda i: (i // per_batch, 0))]
                 + [_resident(w.shape) for w in weights],
        out_specs=pl.BlockSpec((TAIL_TM, D_MODEL), row),
        compiler_params=pltpu.CompilerParams(
            dimension_semantics=("arbitrary",), vmem_limit_bytes=VMEM_LIMIT_BYTES),
        name="tail",
    )(x2d, o_a, o_b, z, z, kv, *weights)


def kernel(x, mem, w_in, lb_logits, hg_norm_gain, swa_sinks, rel_bias, w_mem_kv, w_branch_hg, w_branch_swa,
           w_branch_mem, w_out, ln1_g, ln1_b, w_up, w_down, ln2_g, ln2_b):
    batch, seq, _ = x.shape
    depth = w_in.shape[0]
    assert depth == 1 and lb_logits.shape[0] == depth + 1
    alpha = (2.0 * depth) ** 0.25
    out_dtype = x.dtype

    w_in_b = w_in[0].astype(BF16)
    x2d = x.reshape(batch * seq, D_MODEL).astype(F32)
    later = [w[0].astype(F32) for w in (w_branch_hg, w_branch_swa, w_branch_mem, w_out, w_up, w_down)]
    z, z_lo, z_k, (wa, wb, wc, wo, wu, wd) = _proj_in(
        x2d, w_in_b, lb_logits.astype(F32), hg_norm_gain.astype(F32), later)
    o_a = _hgrn(z, z_lo, z_k, batch, seq)
    o_b, kv = _swa(z, rel_bias.astype(F32), swa_sinks[0].astype(F32),
                   mem.reshape(batch * mem.shape[1], D_MODEL).astype(F32), w_mem_kv[0].astype(F32), batch, seq)
    h2 = _tail(alpha, x2d, o_a, o_b, z, kv, wa, wb, wc, wo, ln1_g.astype(F32), ln1_b.astype(F32),
               wu, wd, ln2_g.astype(F32), ln2_b.astype(F32), batch, seq)
    return h2.reshape(batch, seq, D_MODEL).astype(out_dtype)
```

```python
import functools
import math

import numpy as np
import jax
import jax.numpy as jnp
from jax import lax
from jax.experimental import pallas as pl
from jax.experimental.pallas import tpu as pltpu

F32 = jnp.float32
BF16 = jnp.bfloat16

D_MODEL = 1024
HG_HEADS = 8
HG_DK = 128
HG_DV = 128
HG_CHUNK = 64
SWA_HEADS = 16
SWA_KV_HEADS = 2
SWA_GROUP = SWA_HEADS // SWA_KV_HEADS
SWA_HEAD_DIM = 64
SWA_BLOCK = 128
MEM_HEADS = 4
MEM_HEAD_DIM = 256
NUM_BUCKETS = 32
MAX_DISTANCE = 128
D_FF = 4 * D_MODEL
LN_EPS = 1e-5
RMS_EPS = 1e-6
LOG2E = 1.4426950408889634

Z_HQ, Z_HF, Z_HI, Z_HG, Z_SQ, Z_MQ = (i * D_MODEL for i in range(6))
Z_GL = 6 * D_MODEL
Z_KV = 9 * D_MODEL
Z_COLS = Z_KV + 2 * SWA_KV_HEADS * SWA_HEAD_DIM

VMEM_LIMIT_BYTES = 56 * 1024 * 1024

_NT = (((1,), (1,)), ((), ()))
_TN = (((0,), (0,)), ((), ()))


def _dot(a, b):
    return jnp.dot(a, b, preferred_element_type=F32)


def _resident(shape):
    nd = len(shape)
    return pl.BlockSpec(shape, lambda *_: (0,) * nd, pipeline_mode=pl.Buffered(1))


PROJ_TM = 512
PROJ_TN = 1024


def _proj_col_chunks():
    kv_w = 2 * SWA_KV_HEADS * SWA_HEAD_DIM
    src_kv = Z_MQ
    segments = ((0, 0, Z_MQ), (src_kv + kv_w, Z_MQ, Z_KV - Z_MQ), (src_kv, Z_KV, kv_w))
    chunks = []
    for src, dst, width in segments:
        for off in range(0, width, PROJ_TN):
            chunks.append((src + off, dst + off, min(PROJ_TN, width - off)))
    return chunks


def _proj_in_kernel(n_cast, x_ref, w_ref, lbl_ref, gain_ref, *refs):
    cast_in, (z_ref, lo_ref, k_ref), cast_out = refs[:n_cast], refs[n_cast:n_cast + 3], refs[n_cast + 3:]
    l0 = lbl_ref[0:1, :]
    l1 = lbl_ref[1:2, :]
    lmax = jnp.maximum(l0, l1)
    e0 = jnp.exp(l0 - lmax)
    e1 = jnp.exp(l1 - lmax)
    lb = e0 / (e0 + e1)
    f_mid = 0.5 * (1.0 + lb)
    f_amp = 0.5 * (1.0 - lb)
    gain_s = gain_ref[...] * (HG_DV ** 0.5)

    xb = x_ref[...].astype(BF16)
    for src, dst, width in _proj_col_chunks():
        res = _dot(xb, w_ref[:, src:src + width])
        if dst == Z_HF:
            f = f_mid + f_amp * jnp.tanh(0.5 * res)
            logf = jnp.log(f)
            hi = logf.astype(BF16)
            z_ref[:, dst:dst + width] = hi
            lo_ref[...] = (logf - hi.astype(F32)).astype(BF16)
            k_ref[...] = 1.0 - f
        elif dst == Z_HG:
            hg = 0.5 * res
            z_ref[:, dst:dst + width] = (gain_s * (hg + hg * jnp.tanh(hg))).astype(BF16)
        else:
            z_ref[:, dst:dst + width] = res.astype(BF16)
    for src_ref, dst_ref in zip(cast_in, cast_out):
        dst_ref[...] = src_ref[...].astype(BF16)


def _proj_in(x2d, w_in_b, lb_logits, gain, later_weights):
    t = x2d.shape[0]
    steps = t // PROJ_TM
    slab = lambda w: pl.BlockSpec((w.shape[0] // steps, w.shape[1]), lambda i: (i, 0))
    row = lambda width: pl.BlockSpec((PROJ_TM, width), lambda i: (i, 0))
    outs = pl.pallas_call(
        functools.partial(_proj_in_kernel, len(later_weights)),
        out_shape=[jax.ShapeDtypeStruct((t, Z_COLS), BF16), jax.ShapeDtypeStruct((t, D_MODEL), BF16),
                   jax.ShapeDtypeStruct((t, D_MODEL), F32)]
                  + [jax.ShapeDtypeStruct(w.shape, BF16) for w in later_weights],
        grid=(steps,),
        in_specs=[row(D_MODEL), _resident((D_MODEL, Z_COLS)), _resident(lb_logits.shape), _resident(gain.shape)]
                 + [slab(w) for w in later_weights],
        out_specs=[row(Z_COLS), row(D_MODEL), row(D_MODEL)] + [slab(w) for w in later_weights],
        compiler_params=pltpu.CompilerParams(
            dimension_semantics=("arbitrary",), vmem_limit_bytes=VMEM_LIMIT_BYTES),
        name="proj_in",
    )(x2d, w_in_b, lb_logits, gain, *later_weights)
    return outs[0], outs[1], outs[2], outs[3:]


HG_TB = 1024
HG_GROUP = 2


def _hgrn_kernel(tril_ref, zq_ref, zhi_ref, zlo_ref, zk_ref, zi_ref, zg_ref, o_ref, st_ref, stb_ref):
    @pl.when(pl.program_id(1) == 0)
    def _():
        st_ref[...] = jnp.zeros_like(st_ref)
        stb_ref[...] = jnp.zeros_like(stb_ref)

    tril2 = tril_ref[...]
    causal = tril2[:, :HG_CHUNK] > 0
    heads = range(HG_HEADS)
    hcols = [slice(h * HG_DK, (h + 1) * HG_DK) for h in heads]

    def chunk_group(g, carry):
        rows_of = [pl.ds(pl.multiple_of((g * HG_GROUP + j) * HG_CHUNK, HG_CHUNK), HG_CHUNK)
                   for j in range(HG_GROUP)]
        units = [(rows, h) for rows in rows_of for h in heads]

        bs = [_dot(tril2, jnp.concatenate([zhi_ref[rows, hcols[h]], zlo_ref[rows, hcols[h]]], axis=0))
              for rows, h in units]

        attns, upds, qins, ebls = [], [], [], []
        for (rows, h), b in zip(units, bs):
            ebl = jnp.exp(b[HG_CHUNK - 1:HG_CHUNK, :])
            eb = jnp.exp2(b * LOG2E)
            q_in = (zq_ref[rows, hcols[h]].astype(F32) * eb).astype(BF16)
            kin = zk_ref[rows, hcols[h]] * (1.0 / eb)
            k_in = kin.astype(BF16)
            k_out = (kin * ebl).astype(BF16)
            attns.append(lax.dot_general(q_in, k_in, _NT, preferred_element_type=F32))
            upds.append(lax.dot_general(zi_ref[rows, hcols[h]], k_out, _TN, preferred_element_type=F32))
            qins.append(q_in)
            ebls.append(ebl)

        outs = []
        for u, (rows, h) in enumerate(units):
            attn = jnp.where(causal, attns[u], 0.0).astype(BF16)
            lhs = jnp.concatenate([qins[u], attn], axis=1)
            rhs = jnp.concatenate([stb_ref[h], zi_ref[rows, hcols[h]]], axis=0)
            outs.append(_dot(lhs, rhs))
            st = st_ref[h] * ebls[u] + upds[u]
            st_ref[h] = st
            stb_ref[h] = st.astype(BF16).T

        for (rows, h), o in zip(units, outs):
            ss = jnp.sum(o * o, axis=-1, keepdims=True)
            gate = zg_ref[rows, hcols[h]].astype(F32)
            o_ref[rows, hcols[h]] = (o * lax.rsqrt(ss + HG_DV * RMS_EPS) * gate).astype(BF16)
        return carry

    lax.fori_loop(0, HG_TB // (HG_CHUNK * HG_GROUP), chunk_group, 0, unroll=True)


def _hgrn(z, z_lo, z_k, batch, seq):
    nt = seq // HG_TB
    tril = np.tril(np.ones((HG_CHUNK, HG_CHUNK), np.float32))
    tril = jnp.asarray(np.concatenate([tril, tril], axis=1), BF16)
    tile = lambda b, t: (b * nt + t, 0)

    def zspec(col):
        return pl.BlockSpec((HG_TB, D_MODEL), lambda b, t, col=col: (b * nt + t, col // D_MODEL))

    return pl.pallas_call(
        _hgrn_kernel,
        out_shape=jax.ShapeDtypeStruct((batch * seq, D_MODEL), BF16),
        grid=(batch, nt),
        in_specs=[_resident(tril.shape), zspec(Z_HQ), zspec(Z_HF),
                  pl.BlockSpec((HG_TB, D_MODEL), tile), pl.BlockSpec((HG_TB, D_MODEL), tile),
                  zspec(Z_HI), zspec(Z_HG)],
        out_specs=pl.BlockSpec((HG_TB, D_MODEL), tile),
        scratch_shapes=[pltpu.VMEM((HG_HEADS, HG_DV, HG_DK), F32),
                        pltpu.VMEM((HG_HEADS, HG_DV, HG_DK), BF16)],
        compiler_params=pltpu.CompilerParams(
            dimension_semantics=("arbitrary", "arbitrary"), vmem_limit_bytes=VMEM_LIMIT_BYTES),
        name="hgrn",
    )(tril, z, z, z_lo, z_k, z, z)


def _swa_bucket_table():
    qi = np.arange(SWA_BLOCK)[:, None]
    kj = np.arange(SWA_BLOCK)[None, :]
    n = np.where(kj <= qi, qi - kj, qi + SWA_BLOCK - kj).astype(np.int32)
    max_exact = NUM_BUCKETS // 2
    nf = np.maximum(n, 1).astype(np.float32)
    large = max_exact + (np.log(nf / np.float32(max_exact)) / np.float32(math.log(MAX_DISTANCE / max_exact))
                         * np.float32(NUM_BUCKETS - max_exact)).astype(np.int32)
    large = np.minimum(large, NUM_BUCKETS - 1)
    return np.where(n < max_exact, n, large).astype(np.int32)


SWA_TQ = 1024
SWA_NQ = SWA_TQ // SWA_BLOCK
SWA_MASKED = -1e30


def _swa_kernel(relb_ref, sink_ref, bucket_ref, q_ref, kvc_ref, kvp_ref, mem_ref, wkv_ref,
                o_ref, mkv_ref, bias_ref, kz_ref, vz_ref):
    t = pl.program_id(1)
    qi = lax.broadcasted_iota(jnp.int32, (SWA_BLOCK, SWA_BLOCK), 0)
    kj = lax.broadcasted_iota(jnp.int32, (SWA_BLOCK, SWA_BLOCK), 1)
    tri = kj <= qi

    @pl.when((pl.program_id(0) == 0) & (t == 0))
    def _():
        mem_w = MEM_HEADS * MEM_HEAD_DIM
        mb = mem_ref[...].astype(BF16)
        mkv_ref[:, :mem_w] = (_dot(mb, wkv_ref[:, :mem_w].astype(BF16)) * MEM_HEAD_DIM ** -0.5).astype(BF16)
        mkv_ref[:, mem_w:] = _dot(mb, wkv_ref[:, mem_w:].astype(BF16)).astype(BF16)

        bucket = bucket_ref[...]
        for h in range(SWA_HEADS):
            acc = jnp.zeros((SWA_BLOCK, SWA_BLOCK), F32)
            for bk in range(NUM_BUCKETS):
                acc = jnp.where(bucket == bk, relb_ref[bk, h], acc)
            bias_ref[h] = acc
            bias_ref[SWA_HEADS + h] = jnp.where(tri, acc, SWA_MASKED)
        vz_ref[:, :, 128:256] = jnp.ones((2 * SWA_KV_HEADS, SWA_TQ + SWA_BLOCK, 128), BF16)

    lane = lax.broadcasted_iota(jnp.int32, (SWA_BLOCK, 2 * SWA_HEAD_DIM), 1)
    low = lane < SWA_HEAD_DIM
    scale = SWA_HEAD_DIM ** -0.5
    for blk in range(SWA_NQ + 1):
        rows = slice(blk * SWA_BLOCK, (blk + 1) * SWA_BLOCK)
        if blk == 0:
            kk = kvp_ref[:, 0:128].astype(F32) * scale
            vv = kvp_ref[:, 128:256].astype(F32)
        else:
            src = slice((blk - 1) * SWA_BLOCK, blk * SWA_BLOCK)
            kk = kvc_ref[src, 0:128].astype(F32) * scale
            vv = kvc_ref[src, 128:256].astype(F32)
        kk_sw = pltpu.roll(kk, SWA_HEAD_DIM, 1)
        vv_sw = pltpu.roll(vv, SWA_HEAD_DIM, 1)
        for g in range(SWA_KV_HEADS):
            k_lo, k_hi = (kk, kk_sw) if g == 0 else (kk_sw, kk)
            v_lo, v_hi = (vv, vv_sw) if g == 0 else (vv_sw, vv)
            kz_ref[2 * g + 0, rows, :] = jnp.where(low, k_lo, 0.0).astype(BF16)
            kz_ref[2 * g + 1, rows, :] = jnp.where(low, 0.0, k_hi).astype(BF16)
            vz_ref[2 * g + 0, rows, 0:128] = jnp.where(low, v_lo, 0.0).astype(BF16)
            vz_ref[2 * g + 1, rows, 0:128] = jnp.where(low, 0.0, v_hi).astype(BF16)

    def sub_block(i, carry):
        r0 = pl.multiple_of(i * SWA_BLOCK, SWA_BLOCK)
        qrows = pl.ds(r0, SWA_BLOCK)
        krows = pl.ds(r0, 2 * SWA_BLOCK)
        bias_off = jnp.where((t == 0) & (i == 0), SWA_HEADS, 0)
        for pair in range(SWA_HEADS // 2):
            g = (2 * pair) // SWA_GROUP
            qp = q_ref[qrows, pair * 128:(pair + 1) * 128]
            o_pair = None
            for half in range(2):
                h = 2 * pair + half
                slot = 2 * g + half
                s2 = lax.dot_general(qp, kz_ref[slot, krows, :], _NT, preferred_element_type=F32)
                s = jnp.where(tri, s2[:, SWA_BLOCK:], s2[:, :SWA_BLOCK]) + bias_ref[bias_off + h]
                sink = sink_ref[h]
                m = jnp.maximum(jnp.max(s, axis=-1, keepdims=True), sink)
                p = jnp.exp(s - m)
                p2 = jnp.concatenate([jnp.where(tri, 0.0, p), jnp.where(tri, p, 0.0)], axis=1).astype(BF16)
                od = _dot(p2, vz_ref[slot, krows, :])
                den = od[:, 128:] + jnp.exp(sink - m)
                o_h = od[:, :128] * (1.0 / den)
                o_pair = o_h if o_pair is None else o_pair + o_h
            o_ref[qrows, pair * 128:(pair + 1) * 128] = o_pair.astype(BF16)
        return carry

    lax.fori_loop(0, SWA_NQ, sub_block, 0, unroll=True)


def _swa(z, rel_bias, sinks, mem2d, w_mem_kv, batch, seq):
    nt = seq // SWA_TQ
    bucket = jnp.asarray(_swa_bucket_table())
    kv_col = Z_KV // 256
    smem = pl.BlockSpec(memory_space=pltpu.SMEM)
    mkv_shape = (mem2d.shape[0], w_mem_kv.shape[1])
    return pl.pallas_call(
        _swa_kernel,
        out_shape=[jax.ShapeDtypeStruct((batch * seq, D_MODEL), BF16), jax.ShapeDtypeStruct(mkv_shape, BF16)],
        grid=(batch, nt),
        in_specs=[smem, smem, _resident(bucket.shape),
                  pl.BlockSpec((SWA_TQ, D_MODEL), lambda b, t: (b * nt + t, Z_SQ // D_MODEL)),
                  pl.BlockSpec((SWA_TQ, 256), lambda b, t: (b * nt + t, kv_col)),
                  pl.BlockSpec((SWA_BLOCK, 256),
                               lambda b, t: ((b * nt + t) * SWA_NQ - jnp.minimum(t, 1), kv_col)),
                  _resident(mem2d.shape), _resident(w_mem_kv.shape)],
        out_specs=[pl.BlockSpec((SWA_TQ, D_MODEL), lambda b, t: (b * nt + t, 0)),
                   pl.BlockSpec(mkv_shape, lambda b, t: (0, 0))],
        scratch_shapes=[pltpu.VMEM((2 * SWA_HEADS, SWA_BLOCK, SWA_BLOCK), F32),
                        pltpu.VMEM((2 * SWA_KV_HEADS, SWA_TQ + SWA_BLOCK, 128), BF16),
                        pltpu.VMEM((2 * SWA_KV_HEADS, SWA_TQ + SWA_BLOCK, 256), BF16)],
        compiler_params=pltpu.CompilerParams(
            dimension_semantics=("arbitrary", "arbitrary"), vmem_limit_bytes=VMEM_LIMIT_BYTES),
        name="swa",
    )(rel_bias, sinks, bucket, z, z, z, mem2d, w_mem_kv)


TAIL_TM = 512
TAIL_SUB = 256
TAIL_TF = 1024


def _layer_norm(y, g, b):
    mu = jnp.mean(y, axis=-1, keepdims=True)
    yc = y - mu
    var = jnp.mean(yc * yc, axis=-1, keepdims=True)
    return yc * lax.rsqrt(var + LN_EPS) * g + b


def _tail_kernel(alpha, x_ref, oa_ref, ob_ref, mq_ref, gl_ref, kv_ref, wa_ref, wb_ref, wc_ref, wo_ref,
                 g1_ref, b1_ref, wu_ref, wd_ref, g2_ref, b2_ref, o_ref):
    subs = [slice(r, r + TAIL_SUB) for r in range(0, TAIL_TM, TAIL_SUB)]
    mem_w = MEM_HEADS * MEM_HEAD_DIM
    hcols = [slice(h * MEM_HEAD_DIM, (h + 1) * MEM_HEAD_DIM) for h in range(MEM_HEADS)]

    def gate(rows, i):
        return jax.nn.sigmoid(gl_ref[rows, i * D_MODEL:(i + 1) * D_MODEL].astype(F32))

    scores = [[lax.dot_general(mq_ref[rows, c], kv_ref[:, c], _NT, preferred_element_type=F32) for c in hcols]
              for rows in subs]
    merged = [gate(rows, 0) * _dot(oa_ref[rows, :], wa_ref[...]) for rows in subs]
    probs = []
    for s_heads in scores:
        ps = []
        for s in s_heads:
            p = jnp.exp(s - jnp.max(s, axis=-1, keepdims=True))
            ps.append((p * (1.0 / jnp.sum(p, axis=-1, keepdims=True))).astype(BF16))
        probs.append(ps)
    o_c = [jnp.concatenate([_dot(p, kv_ref[:, mem_w + h * MEM_HEAD_DIM:mem_w + (h + 1) * MEM_HEAD_DIM]).astype(BF16)
                            for h, p in enumerate(ps)], axis=1) for ps in probs]
    merged = [m + gate(rows, 1) * _dot(ob_ref[rows, :], wb_ref[...]) for m, rows in zip(merged, subs)]
    merged = [m + gate(rows, 2) * _dot(oc, wc_ref[...]) for m, rows, oc in zip(merged, subs, o_c)]

    h1 = []
    for m, rows in zip(merged, subs):
        mix = _dot(m.astype(BF16), wo_ref[...])
        h1.append(_layer_norm(alpha * x_ref[rows, :] + mix, g1_ref[...], b1_ref[...]))

    hb = [h.astype(BF16) for h in h1]
    ff = [None] * len(subs)
    for f0 in range(0, D_FF, TAIL_TF):
        us = [jnp.maximum(_dot(hb[i], wu_ref[:, f0:f0 + TAIL_TF]), 0.0) for i in range(len(subs))]
        for i, u in enumerate(us):
            d = _dot((u * u).astype(BF16), wd_ref[f0:f0 + TAIL_TF, :])
            ff[i] = d if ff[i] is None else ff[i] + d
    for i, rows in enumerate(subs):
        o_ref[rows, :] = _layer_norm(alpha * h1[i] + ff[i], g2_ref[...], b2_ref[...])


def _tail(alpha, x2d, o_a, o_b, z, kv, wa, wb, wc, wo, ln1_g, ln1_b, wu, wd, ln2_g, ln2_b, batch, seq):
    t = x2d.shape[0]
    per_batch = seq // TAIL_TM
    row = lambda i: (i, 0)
    weights = (wa, wb, wc, wo, ln1_g, ln1_b, wu, wd, ln2_g, ln2_b)
    return pl.pallas_call(
        functools.partial(_tail_kernel, alpha),
        out_shape=jax.ShapeDtypeStruct((t, D_MODEL), F32),
        grid=(t // TAIL_TM,),
        in_specs=[pl.BlockSpec((TAIL_TM, D_MODEL), row),
                  pl.BlockSpec((TAIL_TM, D_MODEL), row),
                  pl.BlockSpec((TAIL_TM, D_MODEL), row),
                  pl.BlockSpec((TAIL_TM, D_MODEL), lambda i: (i, Z_MQ // D_MODEL)),
                  pl.BlockSpec((TAIL_TM, 3 * D_MODEL), lambda i: (i, Z_GL // (3 * D_MODEL))),
                  pl.BlockSpec((kv.shape[0] // batch, kv.shape[1]), lambda i: (i // per_batch, 0))]
                 + [_resident(w.shape) for w in weights],
        out_specs=pl.BlockSpec((TAIL_TM, D_MODEL), row),
        compiler_params=pltpu.CompilerParams(
            dimension_semantics=("arbitrary",), vmem_limit_bytes=VMEM_LIMIT_BYTES),
        name="tail",
    )(x2d, o_a, o_b, z, z, kv, *weights)


def kernel(x, mem, w_in, lb_logits, hg_norm_gain, swa_sinks, rel_bias, w_mem_kv, w_branch_hg, w_branch_swa,
           w_branch_mem, w_out, ln1_g, ln1_b, w_up, w_down, ln2_g, ln2_b):
    batch, seq, _ = x.shape
    depth = w_in.shape[0]
    assert depth == 1 and lb_logits.shape[0] == depth + 1
    alpha = (2.0 * depth) ** 0.25
    out_dtype = x.dtype

    w_in_b = w_in[0].astype(BF16)
    x2d = x.reshape(batch * seq, D_MODEL).astype(F32)
    later = [w[0].astype(F32) for w in (w_branch_hg, w_branch_swa, w_branch_mem, w_out, w_up, w_down)]
    z, z_lo, z_k, (wa, wb, wc, wo, wu, wd) = _proj_in(
        x2d, w_in_b, lb_logits.astype(F32), hg_norm_gain.astype(F32), later)
    o_a = _hgrn(z, z_lo, z_k, batch, seq)
    o_b, kv = _swa(z, rel_bias.astype(F32), swa_sinks[0].astype(F32),
                   mem.reshape(batch * mem.shape[1], D_MODEL).astype(F32), w_mem_kv[0].astype(F32), batch, seq)
    h2 = _tail(alpha, x2d, o_a, o_b, z, kv, wa, wb, wc, wo, ln1_g.astype(F32), ln1_b.astype(F32),
               wu, wd, ln2_g.astype(F32), ln2_b.astype(F32), batch, seq)
    return h2.reshape(batch, seq, D_MODEL).astype(out_dtype)
```

```python
import functools
import math

import numpy as np
import jax
import jax.numpy as jnp
from jax import lax
from jax.experimental import pallas as pl
from jax.experimental.pallas import tpu as pltpu

F32 = jnp.float32
BF16 = jnp.bfloat16

D_MODEL = 1024
HG_HEADS = 8
HG_DK = 128
HG_DV = 128
HG_CHUNK = 64
SWA_HEADS = 16
SWA_KV_HEADS = 2
SWA_GROUP = SWA_HEADS // SWA_KV_HEADS
SWA_HEAD_DIM = 64
SWA_BLOCK = 128
MEM_HEADS = 4
MEM_HEAD_DIM = 256
NUM_BUCKETS = 32
MAX_DISTANCE = 128
D_FF = 4 * D_MODEL
LN_EPS = 1e-5
RMS_EPS = 1e-6
LOG2E = 1.4426950408889634

Z_HQ, Z_HF, Z_HI, Z_HG, Z_SQ, Z_MQ = (i * D_MODEL for i in range(6))
Z_GL = 6 * D_MODEL
Z_KV = 9 * D_MODEL
Z_COLS = Z_KV + 2 * SWA_KV_HEADS * SWA_HEAD_DIM

VMEM_LIMIT_BYTES = 56 * 1024 * 1024

_NT = (((1,), (1,)), ((), ()))
_TN = (((0,), (0,)), ((), ()))


def _dot(a, b):
    return jnp.dot(a, b, preferred_element_type=F32)


def _resident(shape):
    nd = len(shape)
    return pl.BlockSpec(shape, lambda *_: (0,) * nd, pipeline_mode=pl.Buffered(1))


PROJ_TM = 512
PROJ_TN = 1024


def _proj_col_chunks():
    kv_w = 2 * SWA_KV_HEADS * SWA_HEAD_DIM
    src_kv = Z_MQ
    segments = ((0, 0, Z_MQ), (src_kv + kv_w, Z_MQ, Z_KV - Z_MQ), (src_kv, Z_KV, kv_w))
    chunks = []
    for src, dst, width in segments:
        for off in range(0, width, PROJ_TN):
            chunks.append((src + off, dst + off, min(PROJ_TN, width - off)))
    return chunks


def _proj_in_kernel(n_cast, x_ref, w_ref, lbl_ref, gain_ref, *refs):
    cast_in, (z_ref, lo_ref, k_ref, kt_ref), cast_out = refs[:n_cast], refs[n_cast:n_cast + 4], refs[n_cast + 4:]
    l0 = lbl_ref[0:1, :]
    l1 = lbl_ref[1:2, :]
    lmax = jnp.maximum(l0, l1)
    e0 = jnp.exp(l0 - lmax)
    e1 = jnp.exp(l1 - lmax)
    lb = e0 / (e0 + e1)
    f_mid = 0.5 * (1.0 + lb)
    f_amp = 0.5 * (1.0 - lb)
    gain_s = gain_ref[...] * (HG_DV ** 0.5)

    xb = x_ref[...].astype(BF16)
    for src, dst, width in _proj_col_chunks():
        res = _dot(xb, w_ref[:, src:src + width])
        if dst == Z_HF:
            f = f_mid + f_amp * jnp.tanh(0.5 * res)
            logf = jnp.log(f)
            hi = logf.astype(BF16)
            z_ref[:, dst:dst + width] = hi
            lo_ref[...] = (logf - hi.astype(F32)).astype(BF16)
            k_ref[...] = 1.0 - f
        elif dst == Z_HG:
            hg = 0.5 * res
            z_ref[:, dst:dst + width] = (gain_s * (hg + hg * jnp.tanh(hg))).astype(BF16)
        else:
            z_ref[:, dst:dst + width] = res.astype(BF16)
            if dst == Z_KV:
                kw = SWA_KV_HEADS * SWA_HEAD_DIM
                kt_ref[...] = (res[:, :kw] * SWA_HEAD_DIM ** -0.5).T.astype(BF16)
    for src_ref, dst_ref in zip(cast_in, cast_out):
        dst_ref[...] = src_ref[...].astype(BF16)


def _proj_in(x2d, w_in_b, lb_logits, gain, later_weights):
    t = x2d.shape[0]
    steps = t // PROJ_TM
    kt_rows = SWA_KV_HEADS * SWA_HEAD_DIM
    slab = lambda w: pl.BlockSpec((w.shape[0] // steps, w.shape[1]), lambda i: (i, 0))
    row = lambda width: pl.BlockSpec((PROJ_TM, width), lambda i: (i, 0))
    outs = pl.pallas_call(
        functools.partial(_proj_in_kernel, len(later_weights)),
        out_shape=[jax.ShapeDtypeStruct((t, Z_COLS), BF16), jax.ShapeDtypeStruct((t, D_MODEL), BF16),
                   jax.ShapeDtypeStruct((t, D_MODEL), F32), jax.ShapeDtypeStruct((kt_rows, t), BF16)]
                  + [jax.ShapeDtypeStruct(w.shape, BF16) for w in later_weights],
        grid=(steps,),
        in_specs=[row(D_MODEL), _resident((D_MODEL, Z_COLS)), _resident(lb_logits.shape), _resident(gain.shape)]
                 + [slab(w) for w in later_weights],
        out_specs=[row(Z_COLS), row(D_MODEL), row(D_MODEL), pl.BlockSpec((kt_rows, PROJ_TM), lambda i: (0, i))]
                  + [slab(w) for w in later_weights],
        compiler_params=pltpu.CompilerParams(
            dimension_semantics=("arbitrary",), vmem_limit_bytes=VMEM_LIMIT_BYTES),
        name="proj_in",
    )(x2d, w_in_b, lb_logits, gain, *later_weights)
    return outs[0], outs[1], outs[2], outs[3], outs[4:]


HG_TB = 1024
HG_GROUP = 2


def _hgrn_kernel(tril_ref, zq_ref, zhi_ref, zlo_ref, zk_ref, zi_ref, zg_ref, o_ref, st_ref, stb_ref):
    @pl.when(pl.program_id(1) == 0)
    def _():
        st_ref[...] = jnp.zeros_like(st_ref)
        stb_ref[...] = jnp.zeros_like(stb_ref)

    tril2 = tril_ref[...]
    causal = tril2[:, :HG_CHUNK] > 0
    heads = range(HG_HEADS)
    hcols = [slice(h * HG_DK, (h + 1) * HG_DK) for h in heads]

    def chunk_group(g, carry):
        rows_of = [pl.ds(pl.multiple_of((g * HG_GROUP + j) * HG_CHUNK, HG_CHUNK), HG_CHUNK)
                   for j in range(HG_GROUP)]
        units = [(rows, h) for rows in rows_of for h in heads]

        bs = [_dot(tril2, jnp.concatenate([zhi_ref[rows, hcols[h]], zlo_ref[rows, hcols[h]]], axis=0))
              for rows, h in units]

        attns, upds, qins, ebls = [], [], [], []
        for (rows, h), b in zip(units, bs):
            ebl = jnp.exp(b[HG_CHUNK - 1:HG_CHUNK, :])
            eb = jnp.exp2(b * LOG2E)
            q_in = (zq_ref[rows, hcols[h]].astype(F32) * eb).astype(BF16)
            kin = zk_ref[rows, hcols[h]] * (1.0 / eb)
            k_in = kin.astype(BF16)
            k_out = (kin * ebl).astype(BF16)
            attns.append(lax.dot_general(q_in, k_in, _NT, preferred_element_type=F32))
            upds.append(lax.dot_general(zi_ref[rows, hcols[h]], k_out, _TN, preferred_element_type=F32))
            qins.append(q_in)
            ebls.append(ebl)

        outs = []
        for u, (rows, h) in enumerate(units):
            attn = jnp.where(causal, attns[u], 0.0).astype(BF16)
            lhs = jnp.concatenate([qins[u], attn], axis=1)
            rhs = jnp.concatenate([stb_ref[h], zi_ref[rows, hcols[h]]], axis=0)
            outs.append(_dot(lhs, rhs))
            st = st_ref[h] * ebls[u] + upds[u]
            st_ref[h] = st
            stb_ref[h] = st.astype(BF16).T

        for (rows, h), o in zip(units, outs):
            ss = jnp.sum(o * o, axis=-1, keepdims=True)
            gate = zg_ref[rows, hcols[h]].astype(F32)
            o_ref[rows, hcols[h]] = (o * lax.rsqrt(ss + HG_DV * RMS_EPS) * gate).astype(BF16)
        return carry

    lax.fori_loop(0, HG_TB // (HG_CHUNK * HG_GROUP), chunk_group, 0, unroll=True)


def _hgrn(z, z_lo, z_k, batch, seq):
    nt = seq // HG_TB
    tril = np.tril(np.ones((HG_CHUNK, HG_CHUNK), np.float32))
    tril = jnp.asarray(np.concatenate([tril, tril], axis=1), BF16)
    tile = lambda b, t: (b * nt + t, 0)

    def zspec(col):
        return pl.BlockSpec((HG_TB, D_MODEL), lambda b, t, col=col: (b * nt + t, col // D_MODEL))

    return pl.pallas_call(
        _hgrn_kernel,
        out_shape=jax.ShapeDtypeStruct((batch * seq, D_MODEL), BF16),
        grid=(batch, nt),
        in_specs=[_resident(tril.shape), zspec(Z_HQ), zspec(Z_HF),
                  pl.BlockSpec((HG_TB, D_MODEL), tile), pl.BlockSpec((HG_TB, D_MODEL), tile),
                  zspec(Z_HI), zspec(Z_HG)],
        out_specs=pl.BlockSpec((HG_TB, D_MODEL), tile),
        scratch_shapes=[pltpu.VMEM((HG_HEADS, HG_DV, HG_DK), F32),
                        pltpu.VMEM((HG_HEADS, HG_DV, HG_DK), BF16)],
        compiler_params=pltpu.CompilerParams(
            dimension_semantics=("arbitrary", "arbitrary"), vmem_limit_bytes=VMEM_LIMIT_BYTES),
        name="hgrn",
    )(tril, z, z, z_lo, z_k, z, z)


def _swa_bucket_table():
    qi = np.arange(SWA_BLOCK)[:, None]
    kj = np.arange(SWA_BLOCK)[None, :]
    n = np.where(kj <= qi, qi - kj, qi + SWA_BLOCK - kj).astype(np.int32)
    max_exact = NUM_BUCKETS // 2
    nf = np.maximum(n, 1).astype(np.float32)
    large = max_exact + (np.log(nf / np.float32(max_exact)) / np.float32(math.log(MAX_DISTANCE / max_exact))
                         * np.float32(NUM_BUCKETS - max_exact)).astype(np.int32)
    large = np.minimum(large, NUM_BUCKETS - 1)
    return np.where(n < max_exact, n, large).astype(np.int32)


SWA_TQ = 1024
SWA_NQ = SWA_TQ // SWA_BLOCK
SWA_MASKED = -1e30


def _swa_kernel(relb_ref, sink_ref, bucket_ref, q_ref, kvc_ref, kvp_ref, ktc_ref, ktp_ref, mem_ref, wkv_ref,
                o_ref, mkv_ref, bias_ref, kz_ref, vz_ref):
    t = pl.program_id(1)
    qi = lax.broadcasted_iota(jnp.int32, (SWA_BLOCK, SWA_BLOCK), 0)
    kj = lax.broadcasted_iota(jnp.int32, (SWA_BLOCK, SWA_BLOCK), 1)
    tri = kj <= qi

    @pl.when((pl.program_id(0) == 0) & (t == 0))
    def _():
        mem_w = MEM_HEADS * MEM_HEAD_DIM
        mb = mem_ref[...].astype(BF16)
        mkv_ref[:, :mem_w] = (_dot(mb, wkv_ref[:, :mem_w].astype(BF16)) * MEM_HEAD_DIM ** -0.5).astype(BF16)
        mkv_ref[:, mem_w:] = _dot(mb, wkv_ref[:, mem_w:].astype(BF16)).astype(BF16)

        bucket = bucket_ref[...]
        for h in range(SWA_HEADS):
            acc = jnp.zeros((SWA_BLOCK, SWA_BLOCK), F32)
            for bk in range(NUM_BUCKETS):
                acc = jnp.where(bucket == bk, relb_ref[bk, h], acc)
            bias_ref[h] = acc
            bias_ref[SWA_HEADS + h] = jnp.where(tri, acc, SWA_MASKED)
        vz_ref[:, :, 128:256] = jnp.ones((2 * SWA_KV_HEADS, SWA_TQ + SWA_BLOCK, 128), BF16)
        kz_ref[...] = jnp.zeros_like(kz_ref)

    for g in range(SWA_KV_HEADS):
        dims = slice(g * SWA_HEAD_DIM, (g + 1) * SWA_HEAD_DIM)
        for half in range(2):
            dst = slice(half * SWA_HEAD_DIM, (half + 1) * SWA_HEAD_DIM)
            kz_ref[2 * g + half, dst, 0:SWA_BLOCK] = ktp_ref[dims, :]
            kz_ref[2 * g + half, dst, SWA_BLOCK:] = ktc_ref[dims, :]
    lane = lax.broadcasted_iota(jnp.int32, (SWA_BLOCK, 2 * SWA_HEAD_DIM), 1)
    low = lane < SWA_HEAD_DIM
    for blk in range(SWA_NQ + 1):
        rows = slice(blk * SWA_BLOCK, (blk + 1) * SWA_BLOCK)
        if blk == 0:
            vv = kvp_ref[:, 128:256].astype(F32)
        else:
            vv = kvc_ref[(blk - 1) * SWA_BLOCK:blk * SWA_BLOCK, 128:256].astype(F32)
        vv_sw = pltpu.roll(vv, SWA_HEAD_DIM, 1)
        for g in range(SWA_KV_HEADS):
            v_lo, v_hi = (vv, vv_sw) if g == 0 else (vv_sw, vv)
            vz_ref[2 * g + 0, rows, 0:128] = jnp.where(low, v_lo, 0.0).astype(BF16)
            vz_ref[2 * g + 1, rows, 0:128] = jnp.where(low, 0.0, v_hi).astype(BF16)

    for i in range(SWA_NQ):
        qrows = slice(i * SWA_BLOCK, (i + 1) * SWA_BLOCK)
        krows = slice(i * SWA_BLOCK, (i + 2) * SWA_BLOCK)
        bias_off = jnp.where(t == 0, SWA_HEADS, 0) if i == 0 else 0
        for pair in range(SWA_HEADS // 2):
            g = (2 * pair) // SWA_GROUP
            qp = q_ref[qrows, pair * 128:(pair + 1) * 128]
            o_pair = None
            for half in range(2):
                h = 2 * pair + half
                slot = 2 * g + half
                s2 = _dot(qp, kz_ref[slot, :, krows])
                s = jnp.where(tri, s2[:, SWA_BLOCK:], s2[:, :SWA_BLOCK]) + bias_ref[bias_off + h]
                sink = sink_ref[h]
                m = jnp.maximum(jnp.max(s, axis=-1, keepdims=True), sink)
                p = jnp.exp(s - m)
                p2 = jnp.concatenate([jnp.where(tri, 0.0, p), jnp.where(tri, p, 0.0)], axis=1).astype(BF16)
                od = _dot(p2, vz_ref[slot, krows, :])
                den = od[:, 128:] + jnp.exp(sink - m)
                o_h = od[:, :128] * (1.0 / den)
                o_pair = o_h if o_pair is None else o_pair + o_h
            o_ref[qrows, pair * 128:(pair + 1) * 128] = o_pair.astype(BF16)


def _swa(z, kt, rel_bias, sinks, mem2d, w_mem_kv, batch, seq):
    nt = seq // SWA_TQ
    bucket = jnp.asarray(_swa_bucket_table())
    kv_col = Z_KV // 256
    smem = pl.BlockSpec(memory_space=pltpu.SMEM)
    mkv_shape = (mem2d.shape[0], w_mem_kv.shape[1])
    return pl.pallas_call(
        _swa_kernel,
        out_shape=[jax.ShapeDtypeStruct((batch * seq, D_MODEL), BF16), jax.ShapeDtypeStruct(mkv_shape, BF16)],
        grid=(batch, nt),
        in_specs=[smem, smem, _resident(bucket.shape),
                  pl.BlockSpec((SWA_TQ, D_MODEL), lambda b, t: (b * nt + t, Z_SQ // D_MODEL)),
                  pl.BlockSpec((SWA_TQ, 256), lambda b, t: (b * nt + t, kv_col)),
                  pl.BlockSpec((SWA_BLOCK, 256),
                               lambda b, t: ((b * nt + t) * SWA_NQ - jnp.minimum(t, 1), kv_col)),
                  pl.BlockSpec((kt.shape[0], SWA_TQ), lambda b, t: (0, b * nt + t)),
                  pl.BlockSpec((kt.shape[0], SWA_BLOCK),
                               lambda b, t: (0, (b * nt + t) * SWA_NQ - jnp.minimum(t, 1))),
                  _resident(mem2d.shape), _resident(w_mem_kv.shape)],
        out_specs=[pl.BlockSpec((SWA_TQ, D_MODEL), lambda b, t: (b * nt + t, 0)),
                   pl.BlockSpec(mkv_shape, lambda b, t: (0, 0))],
        scratch_shapes=[pltpu.VMEM((2 * SWA_HEADS, SWA_BLOCK, SWA_BLOCK), F32),
                        pltpu.VMEM((2 * SWA_KV_HEADS, 2 * SWA_HEAD_DIM, SWA_TQ + SWA_BLOCK), BF16),
                        pltpu.VMEM((2 * SWA_KV_HEADS, SWA_TQ + SWA_BLOCK, 256), BF16)],
        compiler_params=pltpu.CompilerParams(
            dimension_semantics=("arbitrary", "arbitrary"), vmem_limit_bytes=VMEM_LIMIT_BYTES),
        name="swa",
    )(rel_bias, sinks, bucket, z, z, z, kt, kt, mem2d, w_mem_kv)


TAIL_TM = 512
TAIL_SUB = 256
TAIL_TF = 1024


def _layer_norm(y, g, b):
    mu = jnp.mean(y, axis=-1, keepdims=True)
    yc = y - mu
    var = jnp.mean(yc * yc, axis=-1, keepdims=True)
    return yc * lax.rsqrt(var + LN_EPS) * g + b


def _tail_kernel(alpha, x_ref, oa_ref, ob_ref, mq_ref, gl_ref, kv_ref, wa_ref, wb_ref, wc_ref, wo_ref,
                 g1_ref, b1_ref, wu_ref, wd_ref, g2_ref, b2_ref, o_ref):
    subs = [slice(r, r + TAIL_SUB) for r in range(0, TAIL_TM, TAIL_SUB)]
    mem_w = MEM_HEADS * MEM_HEAD_DIM
    hcols = [slice(h * MEM_HEAD_DIM, (h + 1) * MEM_HEAD_DIM) for h in range(MEM_HEADS)]

    def gate(rows, i):
        return jax.nn.sigmoid(gl_ref[rows, i * D_MODEL:(i + 1) * D_MODEL].astype(F32))

    scores = [[lax.dot_general(mq_ref[rows, c], kv_ref[:, c], _NT, preferred_element_type=F32) for c in hcols]
              for rows in subs]
    merged = [gate(rows, 0) * _dot(oa_ref[rows, :], wa_ref[...]) for rows in subs]
    probs = []
    for s_heads in scores:
        ps = []
        for s in s_heads:
            p = jnp.exp(s - jnp.max(s, axis=-1, keepdims=True))
            ps.append((p * (1.0 / jnp.sum(p, axis=-1, keepdims=True))).astype(BF16))
        probs.append(ps)
    o_c = [jnp.concatenate([_dot(p, kv_ref[:, mem_w + h * MEM_HEAD_DIM:mem_w + (h + 1) * MEM_HEAD_DIM]).astype(BF16)
                            for h, p in enumerate(ps)], axis=1) for ps in probs]
    merged = [m + gate(rows, 1) * _dot(ob_ref[rows, :], wb_ref[...]) for m, rows in zip(merged, subs)]
    merged = [m + gate(rows, 2) * _dot(oc, wc_ref[...]) for m, rows, oc in zip(merged, subs, o_c)]

    h1 = []
    for m, rows in zip(merged, subs):
        mix = _dot(m.astype(BF16), wo_ref[...])
        h1.append(_layer_norm(alpha * x_ref[rows, :] + mix, g1_ref[...], b1_ref[...]))

    hb = [h.astype(BF16) for h in h1]
    ff = [None] * len(subs)
    for f0 in range(0, D_FF, TAIL_TF):
        us = [jnp.maximum(_dot(hb[i], wu_ref[:, f0:f0 + TAIL_TF]), 0.0) for i in range(len(subs))]
        for i, u in enumerate(us):
            d = _dot((u * u).astype(BF16), wd_ref[f0:f0 + TAIL_TF, :])
            ff[i] = d if ff[i] is None else ff[i] + d
    for i, rows in enumerate(subs):
        o_ref[rows, :] = _layer_norm(alpha * h1[i] + ff[i], g2_ref[...], b2_ref[...])


def _tail(alpha, x2d, o_a, o_b, z, kv, wa, wb, wc, wo, ln1_g, ln1_b, wu, wd, ln2_g, ln2_b, batch, seq):
    t = x2d.shape[0]
    per_batch = seq // TAIL_TM
    row = lambda i: (i, 0)
    weights = (wa, wb, wc, wo, ln1_g, ln1_b, wu, wd, ln2_g, ln2_b)
    return pl.pallas_call(
        functools.partial(_tail_kernel, alpha),
        out_shape=jax.ShapeDtypeStruct((t, D_MODEL), F32),
        grid=(t // TAIL_TM,),
        in_specs=[pl.BlockSpec((TAIL_TM, D_MODEL), row),
                  pl.BlockSpec((TAIL_TM, D_MODEL), row),
                  pl.BlockSpec((TAIL_TM, D_MODEL), row),
                  pl.BlockSpec((TAIL_TM, D_MODEL), lambda i: (i, Z_MQ // D_MODEL)),
                  pl.BlockSpec((TAIL_TM, 3 * D_MODEL), lambda i: (i, Z_GL // (3 * D_MODEL))),
                  pl.BlockSpec((kv.shape[0] // batch, kv.shape[1]), lambda i: (i // per_batch, 0))]
                 + [_resident(w.shape) for w in weights],
        out_specs=pl.BlockSpec((TAIL_TM, D_MODEL), row),
        compiler_params=pltpu.CompilerParams(
            dimension_semantics=("arbitrary",), vmem_limit_bytes=VMEM_LIMIT_BYTES),
        name="tail",
    )(x2d, o_a, o_b, z, z, kv, *weights)


def kernel(x, mem, w_in, lb_logits, hg_norm_gain, swa_sinks, rel_bias, w_mem_kv, w_branch_hg, w_branch_swa,
           w_branch_mem, w_out, ln1_g, ln1_b, w_up, w_down, ln2_g, ln2_b):
    batch, seq, _ = x.shape
    depth = w_in.shape[0]
    assert depth == 1 and lb_logits.shape[0] == depth + 1
    alpha = (2.0 * depth) ** 0.25
    out_dtype = x.dtype

    w_in_b = w_in[0].astype(BF16)
    x2d = x.reshape(batch * seq, D_MODEL).astype(F32)
    later = [w[0].astype(F32) for w in (w_branch_hg, w_branch_swa, w_branch_mem, w_out, w_up, w_down)]
    z, z_lo, z_k, kt, (wa, wb, wc, wo, wu, wd) = _proj_in(
        x2d, w_in_b, lb_logits.astype(F32), hg_norm_gain.astype(F32), later)
    o_a = _hgrn(z, z_lo, z_k, batch, seq)
    o_b, kv = _swa(z, kt, rel_bias.astype(F32), swa_sinks[0].astype(F32),
                   mem.reshape(batch * mem.shape[1], D_MODEL).astype(F32), w_mem_kv[0].astype(F32), batch, seq)
    h2 = _tail(alpha, x2d, o_a, o_b, z, kv, wa, wb, wc, wo, ln1_g.astype(F32), ln1_b.astype(F32),
               wu, wd, ln2_g.astype(F32), ln2_b.astype(F32), batch, seq)
    return h2.reshape(batch, seq, D_MODEL).astype(out_dtype)
```

```python
import functools
import math

import numpy as np
import jax
import jax.numpy as jnp
from jax import lax
from jax.experimental import pallas as pl
from jax.experimental.pallas import tpu as pltpu

F32 = jnp.float32
BF16 = jnp.bfloat16

D_MODEL = 1024
HG_HEADS = 8
HG_DK = 128
HG_DV = 128
HG_CHUNK = 64
SWA_HEADS = 16
SWA_KV_HEADS = 2
SWA_GROUP = SWA_HEADS // SWA_KV_HEADS
SWA_HEAD_DIM = 64
SWA_BLOCK = 128
MEM_HEADS = 4
MEM_HEAD_DIM = 256
NUM_BUCKETS = 32
MAX_DISTANCE = 128
D_FF = 4 * D_MODEL
LN_EPS = 1e-5
RMS_EPS = 1e-6
LOG2E = 1.4426950408889634

Z_HQ, Z_HF, Z_HI, Z_HG, Z_SQ, Z_MQ = (i * D_MODEL for i in range(6))
Z_GL = 6 * D_MODEL
Z_KV = 9 * D_MODEL
Z_COLS = Z_KV + 2 * SWA_KV_HEADS * SWA_HEAD_DIM

VMEM_LIMIT_BYTES = 56 * 1024 * 1024

_NT = (((1,), (1,)), ((), ()))
_TN = (((0,), (0,)), ((), ()))


def _dot(a, b):
    return jnp.dot(a, b, preferred_element_type=F32)


def _resident(shape):
    nd = len(shape)
    return pl.BlockSpec(shape, lambda *_: (0,) * nd, pipeline_mode=pl.Buffered(1))


PROJ_TM = 512
PROJ_TN = 1024


def _proj_col_chunks():
    kv_w = 2 * SWA_KV_HEADS * SWA_HEAD_DIM
    src_kv = Z_MQ
    segments = ((0, 0, Z_MQ), (src_kv + kv_w, Z_MQ, Z_KV - Z_MQ), (src_kv, Z_KV, kv_w))
    chunks = []
    for src, dst, width in segments:
        for off in range(0, width, PROJ_TN):
            chunks.append((src + off, dst + off, min(PROJ_TN, width - off)))
    return chunks


PROJ_W_ROWS = 32


def _proj_in_kernel(n_cast, x_ref, w_hbm, lbl_ref, gain_ref, *refs):
    cast_in, (z_ref, lo_ref, k_ref, kt_ref), cast_out = refs[:n_cast], refs[n_cast:n_cast + 4], refs[n_cast + 4:2 * n_cast + 4]
    w_ref, stage_ref, sem_ref = refs[2 * n_cast + 4:]

    def w_copy(c):
        rows = slice(c * PROJ_W_ROWS, (c + 1) * PROJ_W_ROWS)
        return pltpu.make_async_copy(w_hbm.at[rows, :], stage_ref.at[c % 2], sem_ref.at[c % 2])

    @pl.when(pl.program_id(0) == 0)
    def _():
        n_chunks = D_MODEL // PROJ_W_ROWS
        w_copy(0).start()
        for c in range(n_chunks):
            if c + 1 < n_chunks:
                w_copy(c + 1).start()
            w_copy(c).wait()
            w_ref[c * PROJ_W_ROWS:(c + 1) * PROJ_W_ROWS, :] = stage_ref[c % 2].astype(BF16)

    l0 = lbl_ref[0:1, :]
    l1 = lbl_ref[1:2, :]
    lmax = jnp.maximum(l0, l1)
    e0 = jnp.exp(l0 - lmax)
    e1 = jnp.exp(l1 - lmax)
    lb = e0 / (e0 + e1)
    f_mid = 0.5 * (1.0 + lb)
    f_amp = 0.5 * (1.0 - lb)
    gain_s = gain_ref[...] * (HG_DV ** 0.5)

    xb = x_ref[...].astype(BF16)
    for src, dst, width in _proj_col_chunks():
        res = _dot(xb, w_ref[:, src:src + width])
        if dst == Z_HF:
            f = f_mid + f_amp * jnp.tanh(0.5 * res)
            logf = jnp.log(f)
            hi = logf.astype(BF16)
            z_ref[:, dst:dst + width] = hi
            lo_ref[...] = (logf - hi.astype(F32)).astype(BF16)
            k_ref[...] = 1.0 - f
        elif dst == Z_HG:
            hg = 0.5 * res
            z_ref[:, dst:dst + width] = (gain_s * (hg + hg * jnp.tanh(hg))).astype(BF16)
        else:
            z_ref[:, dst:dst + width] = res.astype(BF16)
            if dst == Z_KV:
                kw = SWA_KV_HEADS * SWA_HEAD_DIM
                kt_ref[...] = (res[:, :kw] * SWA_HEAD_DIM ** -0.5).T.astype(BF16)
    for src_ref, dst_ref in zip(cast_in, cast_out):
        dst_ref[...] = src_ref[...].astype(BF16)


def _proj_in(x2d, w_in_f32, lb_logits, gain, later_weights):
    t = x2d.shape[0]
    steps = t // PROJ_TM
    kt_rows = SWA_KV_HEADS * SWA_HEAD_DIM
    slab = lambda w: pl.BlockSpec((w.shape[0] // steps, w.shape[1]), lambda i: (i, 0))
    row = lambda width: pl.BlockSpec((PROJ_TM, width), lambda i: (i, 0))
    outs = pl.pallas_call(
        functools.partial(_proj_in_kernel, len(later_weights)),
        out_shape=[jax.ShapeDtypeStruct((t, Z_COLS), BF16), jax.ShapeDtypeStruct((t, D_MODEL), BF16),
                   jax.ShapeDtypeStruct((t, D_MODEL), F32), jax.ShapeDtypeStruct((kt_rows, t), BF16)]
                  + [jax.ShapeDtypeStruct(w.shape, BF16) for w in later_weights],
        grid=(steps,),
        in_specs=[row(D_MODEL), pl.BlockSpec(memory_space=pl.ANY),
                  _resident(lb_logits.shape), _resident(gain.shape)] + [slab(w) for w in later_weights],
        out_specs=[row(Z_COLS), row(D_MODEL), row(D_MODEL), pl.BlockSpec((kt_rows, PROJ_TM), lambda i: (0, i))]
                  + [slab(w) for w in later_weights],
        scratch_shapes=[pltpu.VMEM((D_MODEL, Z_COLS), BF16),
                        pltpu.VMEM((2, PROJ_W_ROWS, Z_COLS), F32),
                        pltpu.SemaphoreType.DMA((2,))],
        compiler_params=pltpu.CompilerParams(
            dimension_semantics=("arbitrary",), vmem_limit_bytes=VMEM_LIMIT_BYTES),
        name="proj_in",
    )(x2d, w_in_f32, lb_logits, gain, *later_weights)
    return outs[0], outs[1], outs[2], outs[3], outs[4:]


HG_TB = 1024
HG_GROUP = 2


def _hgrn_kernel(tril_ref, zq_ref, zhi_ref, zlo_ref, zk_ref, zi_ref, zg_ref, o_ref, st_ref, stb_ref):
    @pl.when(pl.program_id(1) == 0)
    def _():
        st_ref[...] = jnp.zeros_like(st_ref)
        stb_ref[...] = jnp.zeros_like(stb_ref)

    tril2 = tril_ref[...]
    causal = tril2[:, :HG_CHUNK] > 0
    heads = range(HG_HEADS)
    hcols = [slice(h * HG_DK, (h + 1) * HG_DK) for h in heads]

    def chunk_group(g, carry):
        rows_of = [pl.ds(pl.multiple_of((g * HG_GROUP + j) * HG_CHUNK, HG_CHUNK), HG_CHUNK)
                   for j in range(HG_GROUP)]
        units = [(rows, h) for rows in rows_of for h in heads]

        bs = [_dot(tril2, jnp.concatenate([zhi_ref[rows, hcols[h]], zlo_ref[rows, hcols[h]]], axis=0))
              for rows, h in units]

        attns, upds, qins, ebls = [], [], [], []
        for (rows, h), b in zip(units, bs):
            ebl = jnp.exp(b[HG_CHUNK - 1:HG_CHUNK, :])
            eb = jnp.exp2(b * LOG2E)
            q_in = (zq_ref[rows, hcols[h]].astype(F32) * eb).astype(BF16)
            kin = zk_ref[rows, hcols[h]] * (1.0 / eb)
            k_in = kin.astype(BF16)
            k_out = (kin * ebl).astype(BF16)
            attns.append(lax.dot_general(q_in, k_in, _NT, preferred_element_type=F32))
            upds.append(lax.dot_general(zi_ref[rows, hcols[h]], k_out, _TN, preferred_element_type=F32))
            qins.append(q_in)
            ebls.append(ebl)

        outs = []
        for u, (rows, h) in enumerate(units):
            attn = jnp.where(causal, attns[u], 0.0).astype(BF16)
            lhs = jnp.concatenate([qins[u], attn], axis=1)
            rhs = jnp.concatenate([stb_ref[h], zi_ref[rows, hcols[h]]], axis=0)
            outs.append(_dot(lhs, rhs))
            st = st_ref[h] * ebls[u] + upds[u]
            st_ref[h] = st
            stb_ref[h] = st.astype(BF16).T

        for (rows, h), o in zip(units, outs):
            ss = jnp.sum(o * o, axis=-1, keepdims=True)
            gate = zg_ref[rows, hcols[h]].astype(F32)
            o_ref[rows, hcols[h]] = (o * lax.rsqrt(ss + HG_DV * RMS_EPS) * gate).astype(BF16)
        return carry

    lax.fori_loop(0, HG_TB // (HG_CHUNK * HG_GROUP), chunk_group, 0, unroll=True)


def _hgrn(z, z_lo, z_k, batch, seq):
    nt = seq // HG_TB
    tril = np.tril(np.ones((HG_CHUNK, HG_CHUNK), np.float32))
    tril = jnp.asarray(np.concatenate([tril, tril], axis=1), BF16)
    tile = lambda b, t: (b * nt + t, 0)

    def zspec(col):
        return pl.BlockSpec((HG_TB, D_MODEL), lambda b, t, col=col: (b * nt + t, col // D_MODEL))

    return pl.pallas_call(
        _hgrn_kernel,
        out_shape=jax.ShapeDtypeStruct((batch * seq, D_MODEL), BF16),
        grid=(batch, nt),
        in_specs=[_resident(tril.shape), zspec(Z_HQ), zspec(Z_HF),
                  pl.BlockSpec((HG_TB, D_MODEL), tile), pl.BlockSpec((HG_TB, D_MODEL), tile),
                  zspec(Z_HI), zspec(Z_HG)],
        out_specs=pl.BlockSpec((HG_TB, D_MODEL), tile),
        scratch_shapes=[pltpu.VMEM((HG_HEADS, HG_DV, HG_DK), F32),
                        pltpu.VMEM((HG_HEADS, HG_DV, HG_DK), BF16)],
        compiler_params=pltpu.CompilerParams(
            dimension_semantics=("arbitrary", "arbitrary"), vmem_limit_bytes=VMEM_LIMIT_BYTES),
        name="hgrn",
    )(tril, z, z, z_lo, z_k, z, z)


def _swa_bucket_table():
    qi = np.arange(SWA_BLOCK)[:, None]
    kj = np.arange(SWA_BLOCK)[None, :]
    n = np.where(kj <= qi, qi - kj, qi + SWA_BLOCK - kj).astype(np.int32)
    max_exact = NUM_BUCKETS // 2
    nf = np.maximum(n, 1).astype(np.float32)
    large = max_exact + (np.log(nf / np.float32(max_exact)) / np.float32(math.log(MAX_DISTANCE / max_exact))
                         * np.float32(NUM_BUCKETS - max_exact)).astype(np.int32)
    large = np.minimum(large, NUM_BUCKETS - 1)
    return np.where(n < max_exact, n, large).astype(np.int32)


SWA_TQ = 1024
SWA_NQ = SWA_TQ // SWA_BLOCK
SWA_MASKED = -1e30


def _swa_kernel(relb_ref, sink_ref, bucket_ref, q_ref, kvc_ref, kvp_ref, ktc_ref, ktp_ref, mem_ref, wkv_ref,
                o_ref, mkv_ref, bias_ref, kz_ref, vz_ref):
    t = pl.program_id(1)
    qi = lax.broadcasted_iota(jnp.int32, (SWA_BLOCK, SWA_BLOCK), 0)
    kj = lax.broadcasted_iota(jnp.int32, (SWA_BLOCK, SWA_BLOCK), 1)
    tri = kj <= qi

    @pl.when((pl.program_id(0) == 0) & (t == 0))
    def _():
        mem_w = MEM_HEADS * MEM_HEAD_DIM
        mb = mem_ref[...].astype(BF16)
        mkv_ref[:, :mem_w] = (_dot(mb, wkv_ref[:, :mem_w].astype(BF16)) * MEM_HEAD_DIM ** -0.5).astype(BF16)
        mkv_ref[:, mem_w:] = _dot(mb, wkv_ref[:, mem_w:].astype(BF16)).astype(BF16)

        bucket = bucket_ref[...]
        for h in range(SWA_HEADS):
            acc = jnp.zeros((SWA_BLOCK, SWA_BLOCK), F32)
            for bk in range(NUM_BUCKETS):
                acc = jnp.where(bucket == bk, relb_ref[bk, h], acc)
            bias_ref[h] = acc
            bias_ref[SWA_HEADS + h] = jnp.where(tri, acc, SWA_MASKED)
        vz_ref[:, :, 128:256] = jnp.ones((2 * SWA_KV_HEADS, SWA_TQ + SWA_BLOCK, 128), BF16)
        kz_ref[...] = jnp.zeros_like(kz_ref)

    for g in range(SWA_KV_HEADS):
        dims = slice(g * SWA_HEAD_DIM, (g + 1) * SWA_HEAD_DIM)
        for half in range(2):
            dst = slice(half * SWA_HEAD_DIM, (half + 1) * SWA_HEAD_DIM)
            kz_ref[2 * g + half, dst, 0:SWA_BLOCK] = ktp_ref[dims, :]
            kz_ref[2 * g + half, dst, SWA_BLOCK:] = ktc_ref[dims, :]
    lane = lax.broadcasted_iota(jnp.int32, (SWA_BLOCK, 2 * SWA_HEAD_DIM), 1)
    low = lane < SWA_HEAD_DIM
    for blk in range(SWA_NQ + 1):
        rows = slice(blk * SWA_BLOCK, (blk + 1) * SWA_BLOCK)
        if blk == 0:
            vv = kvp_ref[:, 128:256].astype(F32)
        else:
            vv = kvc_ref[(blk - 1) * SWA_BLOCK:blk * SWA_BLOCK, 128:256].astype(F32)
        vv_sw = pltpu.roll(vv, SWA_HEAD_DIM, 1)
        for g in range(SWA_KV_HEADS):
            v_lo, v_hi = (vv, vv_sw) if g == 0 else (vv_sw, vv)
            vz_ref[2 * g + 0, rows, 0:128] = jnp.where(low, v_lo, 0.0).astype(BF16)
            vz_ref[2 * g + 1, rows, 0:128] = jnp.where(low, 0.0, v_hi).astype(BF16)

    for i in range(SWA_NQ):
        qrows = slice(i * SWA_BLOCK, (i + 1) * SWA_BLOCK)
        krows = slice(i * SWA_BLOCK, (i + 2) * SWA_BLOCK)
        bias_off = jnp.where(t == 0, SWA_HEADS, 0) if i == 0 else 0
        for pair in range(SWA_HEADS // 2):
            g = (2 * pair) // SWA_GROUP
            qp = q_ref[qrows, pair * 128:(pair + 1) * 128]
            o_pair = None
            for half in range(2):
                h = 2 * pair + half
                slot = 2 * g + half
                s2 = _dot(qp, kz_ref[slot, :, krows])
                s = jnp.where(tri, s2[:, SWA_BLOCK:], s2[:, :SWA_BLOCK]) + bias_ref[bias_off + h]
                sink = sink_ref[h]
                m = jnp.maximum(jnp.max(s, axis=-1, keepdims=True), sink)
                p = jnp.exp(s - m)
                p2 = jnp.concatenate([jnp.where(tri, 0.0, p), jnp.where(tri, p, 0.0)], axis=1).astype(BF16)
                od = _dot(p2, vz_ref[slot, krows, :])
                den = od[:, 128:] + jnp.exp(sink - m)
                o_h = od[:, :128] * (1.0 / den)
                o_pair = o_h if o_pair is None else o_pair + o_h
            o_ref[qrows, pair * 128:(pair + 1) * 128] = o_pair.astype(BF16)


def _swa(z, kt, rel_bias, sinks, mem2d, w_mem_kv, batch, seq):
    nt = seq // SWA_TQ
    bucket = jnp.asarray(_swa_bucket_table())
    kv_col = Z_KV // 256
    smem = pl.BlockSpec(memory_space=pltpu.SMEM)
    mkv_shape = (mem2d.shape[0], w_mem_kv.shape[1])
    return pl.pallas_call(
        _swa_kernel,
        out_shape=[jax.ShapeDtypeStruct((batch * seq, D_MODEL), BF16), jax.ShapeDtypeStruct(mkv_shape, BF16)],
        grid=(batch, nt),
        in_specs=[smem, smem, _resident(bucket.shape),
                  pl.BlockSpec((SWA_TQ, D_MODEL), lambda b, t: (b * nt + t, Z_SQ // D_MODEL)),
                  pl.BlockSpec((SWA_TQ, 256), lambda b, t: (b * nt + t, kv_col)),
                  pl.BlockSpec((SWA_BLOCK, 256),
                               lambda b, t: ((b * nt + t) * SWA_NQ - jnp.minimum(t, 1), kv_col)),
                  pl.BlockSpec((kt.shape[0], SWA_TQ), lambda b, t: (0, b * nt + t)),
                  pl.BlockSpec((kt.shape[0], SWA_BLOCK),
                               lambda b, t: (0, (b * nt + t) * SWA_NQ - jnp.minimum(t, 1))),
                  _resident(mem2d.shape), _resident(w_mem_kv.shape)],
        out_specs=[pl.BlockSpec((SWA_TQ, D_MODEL), lambda b, t: (b * nt + t, 0)),
                   pl.BlockSpec(mkv_shape, lambda b, t: (0, 0))],
        scratch_shapes=[pltpu.VMEM((2 * SWA_HEADS, SWA_BLOCK, SWA_BLOCK), F32),
                        pltpu.VMEM((2 * SWA_KV_HEADS, 2 * SWA_HEAD_DIM, SWA_TQ + SWA_BLOCK), BF16),
                        pltpu.VMEM((2 * SWA_KV_HEADS, SWA_TQ + SWA_BLOCK, 256), BF16)],
        compiler_params=pltpu.CompilerParams(
            dimension_semantics=("arbitrary", "arbitrary"), vmem_limit_bytes=VMEM_LIMIT_BYTES),
        name="swa",
    )(rel_bias, sinks, bucket, z, z, z, kt, kt, mem2d, w_mem_kv)


TAIL_TM = 512
TAIL_SUB = 256
TAIL_TF = 1024


def _layer_norm(y, g, b):
    mu = jnp.mean(y, axis=-1, keepdims=True)
    yc = y - mu
    var = jnp.mean(yc * yc, axis=-1, keepdims=True)
    return yc * lax.rsqrt(var + LN_EPS) * g + b


def _tail_kernel(alpha, x_ref, oa_ref, ob_ref, mq_ref, gl_ref, kv_ref, wa_ref, wb_ref, wc_ref, wo_ref,
                 g1_ref, b1_ref, wu_ref, wd_ref, g2_ref, b2_ref, o_ref):
    subs = [slice(r, r + TAIL_SUB) for r in range(0, TAIL_TM, TAIL_SUB)]
    mem_w = MEM_HEADS * MEM_HEAD_DIM
    hcols = [slice(h * MEM_HEAD_DIM, (h + 1) * MEM_HEAD_DIM) for h in range(MEM_HEADS)]

    def gate(rows, i):
        return jax.nn.sigmoid(gl_ref[rows, i * D_MODEL:(i + 1) * D_MODEL].astype(F32))

    scores = [[lax.dot_general(mq_ref[rows, c], kv_ref[:, c], _NT, preferred_element_type=F32) for c in hcols]
              for rows in subs]
    merged = [gate(rows, 0) * _dot(oa_ref[rows, :], wa_ref[...]) for rows in subs]
    probs = []
    for s_heads in scores:
        ps = []
        for s in s_heads:
            p = jnp.exp(s - jnp.max(s, axis=-1, keepdims=True))
            ps.append((p * (1.0 / jnp.sum(p, axis=-1, keepdims=True))).astype(BF16))
        probs.append(ps)
    o_c = [jnp.concatenate([_dot(p, kv_ref[:, mem_w + h * MEM_HEAD_DIM:mem_w + (h + 1) * MEM_HEAD_DIM]).astype(BF16)
                            for h, p in enumerate(ps)], axis=1) for ps in probs]
    merged = [m + gate(rows, 1) * _dot(ob_ref[rows, :], wb_ref[...]) for m, rows in zip(merged, subs)]
    merged = [m + gate(rows, 2) * _dot(oc, wc_ref[...]) for m, rows, oc in zip(merged, subs, o_c)]

    h1 = []
    for m, rows in zip(merged, subs):
        mix = _dot(m.astype(BF16), wo_ref[...])
        h1.append(_layer_norm(alpha * x_ref[rows, :] + mix, g1_ref[...], b1_ref[...]))

    hb = [h.astype(BF16) for h in h1]
    ff = [None] * len(subs)
    for f0 in range(0, D_FF, TAIL_TF):
        us = [jnp.maximum(_dot(hb[i], wu_ref[:, f0:f0 + TAIL_TF]), 0.0) for i in range(len(subs))]
        for i, u in enumerate(us):
            d = _dot((u * u).astype(BF16), wd_ref[f0:f0 + TAIL_TF, :])
            ff[i] = d if ff[i] is None else ff[i] + d
    for i, rows in enumerate(subs):
        o_ref[rows, :] = _layer_norm(alpha * h1[i] + ff[i], g2_ref[...], b2_ref[...])


def _tail(alpha, x2d, o_a, o_b, z, kv, wa, wb, wc, wo, ln1_g, ln1_b, wu, wd, ln2_g, ln2_b, batch, seq):
    t = x2d.shape[0]
    per_batch = seq // TAIL_TM
    row = lambda i: (i, 0)
    weights = (wa, wb, wc, wo, ln1_g, ln1_b, wu, wd, ln2_g, ln2_b)
    return pl.pallas_call(
        functools.partial(_tail_kernel, alpha),
        out_shape=jax.ShapeDtypeStruct((t, D_MODEL), F32),
        grid=(t // TAIL_TM,),
        in_specs=[pl.BlockSpec((TAIL_TM, D_MODEL), row),
                  pl.BlockSpec((TAIL_TM, D_MODEL), row),
                  pl.BlockSpec((TAIL_TM, D_MODEL), row),
                  pl.BlockSpec((TAIL_TM, D_MODEL), lambda i: (i, Z_MQ // D_MODEL)),
                  pl.BlockSpec((TAIL_TM, 3 * D_MODEL), lambda i: (i, Z_GL // (3 * D_MODEL))),
                  pl.BlockSpec((kv.shape[0] // batch, kv.shape[1]), lambda i: (i // per_batch, 0))]
                 + [_resident(w.shape) for w in weights],
        out_specs=pl.BlockSpec((TAIL_TM, D_MODEL), row),
        compiler_params=pltpu.CompilerParams(
            dimension_semantics=("arbitrary",), vmem_limit_bytes=VMEM_LIMIT_BYTES),
        name="tail",
    )(x2d, o_a, o_b, z, z, kv, *weights)


def kernel(x, mem, w_in, lb_logits, hg_norm_gain, swa_sinks, rel_bias, w_mem_kv, w_branch_hg, w_branch_swa,
           w_branch_mem, w_out, ln1_g, ln1_b, w_up, w_down, ln2_g, ln2_b):
    batch, seq, _ = x.shape
    depth = w_in.shape[0]
    assert depth == 1 and lb_logits.shape[0] == depth + 1
    alpha = (2.0 * depth) ** 0.25
    out_dtype = x.dtype

    x2d = x.reshape(batch * seq, D_MODEL).astype(F32)
    later = [w[0].astype(F32) for w in (w_branch_hg, w_branch_swa, w_branch_mem, w_out, w_up, w_down)]
    z, z_lo, z_k, kt, (wa, wb, wc, wo, wu, wd) = _proj_in(
        x2d, w_in[0].astype(F32), lb_logits.astype(F32), hg_norm_gain.astype(F32), later)
    o_a = _hgrn(z, z_lo, z_k, batch, seq)
    o_b, kv = _swa(z, kt, rel_bias.astype(F32), swa_sinks[0].astype(F32),
                   mem.reshape(batch * mem.shape[1], D_MODEL).astype(F32), w_mem_kv[0].astype(F32), batch, seq)
    h2 = _tail(alpha, x2d, o_a, o_b, z, kv, wa, wb, wc, wo, ln1_g.astype(F32), ln1_b.astype(F32),
               wu, wd, ln2_g.astype(F32), ln2_b.astype(F32), batch, seq)
    return h2.reshape(batch, seq, D_MODEL).astype(out_dtype)
```

```python
import functools
import math

import numpy as np
import jax
import jax.numpy as jnp
from jax import lax
from jax.experimental import pallas as pl
from jax.experimental.pallas import tpu as pltpu

F32 = jnp.float32
BF16 = jnp.bfloat16

D_MODEL = 1024
HG_HEADS = 8
HG_DK = 128
HG_DV = 128
HG_CHUNK = 64
SWA_HEADS = 16
SWA_KV_HEADS = 2
SWA_GROUP = SWA_HEADS // SWA_KV_HEADS
SWA_HEAD_DIM = 64
SWA_BLOCK = 128
MEM_HEADS = 4
MEM_HEAD_DIM = 256
NUM_BUCKETS = 32
MAX_DISTANCE = 128
D_FF = 4 * D_MODEL
LN_EPS = 1e-5
RMS_EPS = 1e-6
LOG2E = 1.4426950408889634

Z_HQ, Z_HF, Z_HI, Z_HG, Z_SQ, Z_MQ = (i * D_MODEL for i in range(6))
Z_GL = 6 * D_MODEL
Z_KV = 9 * D_MODEL
Z_COLS = Z_KV + 2 * SWA_KV_HEADS * SWA_HEAD_DIM

VMEM_LIMIT_BYTES = 60 * 1024 * 1024

_NT = (((1,), (1,)), ((), ()))
_TN = (((0,), (0,)), ((), ()))


def _dot(a, b):
    return jnp.dot(a, b, preferred_element_type=F32)


def _resident(shape):
    nd = len(shape)
    return pl.BlockSpec(shape, lambda *_: (0,) * nd, pipeline_mode=pl.Buffered(1))


PROJ_TM = 512
PROJ_TN = 1024


def _proj_col_chunks():
    kv_w = 2 * SWA_KV_HEADS * SWA_HEAD_DIM
    src_kv = Z_MQ
    segments = ((0, 0, Z_MQ), (src_kv + kv_w, Z_MQ, Z_KV - Z_MQ), (src_kv, Z_KV, kv_w))
    chunks = []
    for src, dst, width in segments:
        for off in range(0, width, PROJ_TN):
            chunks.append((src + off, dst + off, min(PROJ_TN, width - off)))
    return chunks


PROJ_W_ROWS = 16
PROJ_W_SLOTS = 8


def _proj_in_kernel(n_cast, x_ref, w_hbm, lbl_ref, gain_ref, *refs):
    cast_in, (z_ref, lo_ref, k_ref, kt_ref), cast_out = refs[:n_cast], refs[n_cast:n_cast + 4], refs[n_cast + 4:2 * n_cast + 4]
    w_ref, stage_ref, sem_ref = refs[2 * n_cast + 4:]

    def w_copy(c):
        rows = slice(c * PROJ_W_ROWS, (c + 1) * PROJ_W_ROWS)
        slot = c % PROJ_W_SLOTS
        return pltpu.make_async_copy(w_hbm.at[rows, :], stage_ref.at[slot], sem_ref.at[slot])

    @pl.when(pl.program_id(0) == 0)
    def _():
        n_chunks = D_MODEL // PROJ_W_ROWS
        ahead = PROJ_W_SLOTS - 1
        for c in range(ahead):
            w_copy(c).start()
        for c in range(n_chunks):
            if c + ahead < n_chunks:
                w_copy(c + ahead).start()
            w_copy(c).wait()
            w_ref[c * PROJ_W_ROWS:(c + 1) * PROJ_W_ROWS, :] = stage_ref[c % PROJ_W_SLOTS].astype(BF16)

    l0 = lbl_ref[0:1, :]
    l1 = lbl_ref[1:2, :]
    lmax = jnp.maximum(l0, l1)
    e0 = jnp.exp(l0 - lmax)
    e1 = jnp.exp(l1 - lmax)
    lb = e0 / (e0 + e1)
    f_mid = 0.5 * (1.0 + lb)
    f_amp = 0.5 * (1.0 - lb)
    gain_s = gain_ref[...] * (HG_DV ** 0.5)

    xb = x_ref[...].astype(BF16)
    for src, dst, width in _proj_col_chunks():
        res = _dot(xb, w_ref[:, src:src + width])
        if dst == Z_HF:
            f = f_mid + f_amp * jnp.tanh(0.5 * res)
            logf = jnp.log(f)
            hi = logf.astype(BF16)
            z_ref[:, dst:dst + width] = hi
            lo_ref[...] = (logf - hi.astype(F32)).astype(BF16)
            k_ref[...] = 1.0 - f
        elif dst == Z_HG:
            hg = 0.5 * res
            z_ref[:, dst:dst + width] = (gain_s * (hg + hg * jnp.tanh(hg))).astype(BF16)
        else:
            z_ref[:, dst:dst + width] = res.astype(BF16)
            if dst == Z_KV:
                kw = SWA_KV_HEADS * SWA_HEAD_DIM
                kt_ref[...] = (res[:, :kw] * SWA_HEAD_DIM ** -0.5).T.astype(BF16)
    for src_ref, dst_ref in zip(cast_in, cast_out):
        dst_ref[...] = src_ref[...].astype(BF16)


def _proj_in(x2d, w_in_f32, lb_logits, gain, later_weights):
    t = x2d.shape[0]
    steps = t // PROJ_TM
    kt_rows = SWA_KV_HEADS * SWA_HEAD_DIM
    slab = lambda w: pl.BlockSpec((w.shape[0] // steps, w.shape[1]), lambda i: (i, 0))
    row = lambda width: pl.BlockSpec((PROJ_TM, width), lambda i: (i, 0))
    outs = pl.pallas_call(
        functools.partial(_proj_in_kernel, len(later_weights)),
        out_shape=[jax.ShapeDtypeStruct((t, Z_COLS), BF16), jax.ShapeDtypeStruct((t, D_MODEL), BF16),
                   jax.ShapeDtypeStruct((t, D_MODEL), F32), jax.ShapeDtypeStruct((kt_rows, t), BF16)]
                  + [jax.ShapeDtypeStruct(w.shape, BF16) for w in later_weights],
        grid=(steps,),
        in_specs=[row(D_MODEL), pl.BlockSpec(memory_space=pl.ANY),
                  _resident(lb_logits.shape), _resident(gain.shape)] + [slab(w) for w in later_weights],
        out_specs=[row(Z_COLS), row(D_MODEL), row(D_MODEL), pl.BlockSpec((kt_rows, PROJ_TM), lambda i: (0, i))]
                  + [slab(w) for w in later_weights],
        scratch_shapes=[pltpu.VMEM((D_MODEL, Z_COLS), BF16),
                        pltpu.VMEM((PROJ_W_SLOTS, PROJ_W_ROWS, Z_COLS), F32),
                        pltpu.SemaphoreType.DMA((PROJ_W_SLOTS,))],
        compiler_params=pltpu.CompilerParams(
            dimension_semantics=("arbitrary",), vmem_limit_bytes=VMEM_LIMIT_BYTES),
        name="proj_in",
    )(x2d, w_in_f32, lb_logits, gain, *later_weights)
    return outs[0], outs[1], outs[2], outs[3], outs[4:]


HG_TB = 1024
HG_GROUP = 2


def _hgrn_kernel(tril_ref, zq_ref, zhi_ref, zlo_ref, zk_ref, zi_ref, zg_ref, o_ref, st_ref, stb_ref):
    @pl.when(pl.program_id(1) == 0)
    def _():
        st_ref[...] = jnp.zeros_like(st_ref)
        stb_ref[...] = jnp.zeros_like(stb_ref)

    tril2 = tril_ref[...]
    causal = tril2[:, :HG_CHUNK] > 0
    heads = range(HG_HEADS)
    hcols = [slice(h * HG_DK, (h + 1) * HG_DK) for h in heads]

    def chunk_group(g, carry):
        rows_of = [pl.ds(pl.multiple_of((g * HG_GROUP + j) * HG_CHUNK, HG_CHUNK), HG_CHUNK)
                   for j in range(HG_GROUP)]
        units = [(rows, h) for rows in rows_of for h in heads]

        bs = [_dot(tril2, jnp.concatenate([zhi_ref[rows, hcols[h]], zlo_ref[rows, hcols[h]]], axis=0))
              for rows, h in units]

        attns, upds, qins, ebls = [], [], [], []
        for (rows, h), b in zip(units, bs):
            ebl = jnp.exp(b[HG_CHUNK - 1:HG_CHUNK, :])
            eb = jnp.exp2(b * LOG2E)
            q_in = (zq_ref[rows, hcols[h]].astype(F32) * eb).astype(BF16)
            kin = zk_ref[rows, hcols[h]] * (1.0 / eb)
            k_in = kin.astype(BF16)
            k_out = (kin * ebl).astype(BF16)
            attns.append(lax.dot_general(q_in, k_in, _NT, preferred_element_type=F32))
            upds.append(lax.dot_general(zi_ref[rows, hcols[h]], k_out, _TN, preferred_element_type=F32))
            qins.append(q_in)
            ebls.append(ebl)

        outs = []
        for u, (rows, h) in enumerate(units):
            attn = jnp.where(causal, attns[u], 0.0).astype(BF16)
            lhs = jnp.concatenate([qins[u], attn], axis=1)
            rhs = jnp.concatenate([stb_ref[h], zi_ref[rows, hcols[h]]], axis=0)
            outs.append(_dot(lhs, rhs))
            st = st_ref[h] * ebls[u] + upds[u]
            st_ref[h] = st
            stb_ref[h] = st.astype(BF16).T

        for (rows, h), o in zip(units, outs):
            ss = jnp.sum(o * o, axis=-1, keepdims=True)
            gate = zg_ref[rows, hcols[h]].astype(F32)
            o_ref[rows, hcols[h]] = (o * lax.rsqrt(ss + HG_DV * RMS_EPS) * gate).astype(BF16)
        return carry

    lax.fori_loop(0, HG_TB // (HG_CHUNK * HG_GROUP), chunk_group, 0, unroll=True)


def _hgrn(z, z_lo, z_k, batch, seq):
    nt = seq // HG_TB
    tril = np.tril(np.ones((HG_CHUNK, HG_CHUNK), np.float32))
    tril = jnp.asarray(np.concatenate([tril, tril], axis=1), BF16)
    tile = lambda b, t: (b * nt + t, 0)

    def zspec(col):
        return pl.BlockSpec((HG_TB, D_MODEL), lambda b, t, col=col: (b * nt + t, col // D_MODEL))

    return pl.pallas_call(
        _hgrn_kernel,
        out_shape=jax.ShapeDtypeStruct((batch * seq, D_MODEL), BF16),
        grid=(batch, nt),
        in_specs=[_resident(tril.shape), zspec(Z_HQ), zspec(Z_HF),
                  pl.BlockSpec((HG_TB, D_MODEL), tile), pl.BlockSpec((HG_TB, D_MODEL), tile),
                  zspec(Z_HI), zspec(Z_HG)],
        out_specs=pl.BlockSpec((HG_TB, D_MODEL), tile),
        scratch_shapes=[pltpu.VMEM((HG_HEADS, HG_DV, HG_DK), F32),
                        pltpu.VMEM((HG_HEADS, HG_DV, HG_DK), BF16)],
        compiler_params=pltpu.CompilerParams(
            dimension_semantics=("arbitrary", "arbitrary"), vmem_limit_bytes=VMEM_LIMIT_BYTES),
        name="hgrn",
    )(tril, z, z, z_lo, z_k, z, z)


def _swa_bucket_table():
    qi = np.arange(SWA_BLOCK)[:, None]
    kj = np.arange(SWA_BLOCK)[None, :]
    n = np.where(kj <= qi, qi - kj, qi + SWA_BLOCK - kj).astype(np.int32)
    max_exact = NUM_BUCKETS // 2
    nf = np.maximum(n, 1).astype(np.float32)
    large = max_exact + (np.log(nf / np.float32(max_exact)) / np.float32(math.log(MAX_DISTANCE / max_exact))
                         * np.float32(NUM_BUCKETS - max_exact)).astype(np.int32)
    large = np.minimum(large, NUM_BUCKETS - 1)
    return np.where(n < max_exact, n, large).astype(np.int32)


SWA_TQ = 1024
SWA_NQ = SWA_TQ // SWA_BLOCK
SWA_MASKED = -1e30


def _swa_kernel(relb_ref, sink_ref, bucket_ref, q_ref, kvc_ref, kvp_ref, ktc_ref, ktp_ref, mem_ref, wkv_ref,
                o_ref, mkv_ref, bias_ref, kz_ref, vz_ref):
    t = pl.program_id(1)
    qi = lax.broadcasted_iota(jnp.int32, (SWA_BLOCK, SWA_BLOCK), 0)
    kj = lax.broadcasted_iota(jnp.int32, (SWA_BLOCK, SWA_BLOCK), 1)
    tri = kj <= qi

    @pl.when((pl.program_id(0) == 0) & (t == 0))
    def _():
        mem_w = MEM_HEADS * MEM_HEAD_DIM
        mb = mem_ref[...].astype(BF16)
        mkv_ref[:, :mem_w] = (_dot(mb, wkv_ref[:, :mem_w].astype(BF16)) * MEM_HEAD_DIM ** -0.5).astype(BF16)
        mkv_ref[:, mem_w:] = _dot(mb, wkv_ref[:, mem_w:].astype(BF16)).astype(BF16)

        bucket = bucket_ref[...]
        for h in range(SWA_HEADS):
            acc = jnp.zeros((SWA_BLOCK, SWA_BLOCK), F32)
            for bk in range(NUM_BUCKETS):
                acc = jnp.where(bucket == bk, relb_ref[bk, h], acc)
            bias_ref[h] = acc
            bias_ref[SWA_HEADS + h] = jnp.where(tri, acc, SWA_MASKED)
        vz_ref[:, :, 128:256] = jnp.ones((2 * SWA_KV_HEADS, SWA_TQ + SWA_BLOCK, 128), BF16)
        kz_ref[...] = jnp.zeros_like(kz_ref)

    for g in range(SWA_KV_HEADS):
        dims = slice(g * SWA_HEAD_DIM, (g + 1) * SWA_HEAD_DIM)
        for half in range(2):
            dst = slice(half * SWA_HEAD_DIM, (half + 1) * SWA_HEAD_DIM)
            kz_ref[2 * g + half, dst, 0:SWA_BLOCK] = ktp_ref[dims, :]
            kz_ref[2 * g + half, dst, SWA_BLOCK:] = ktc_ref[dims, :]
    lane = lax.broadcasted_iota(jnp.int32, (SWA_BLOCK, 2 * SWA_HEAD_DIM), 1)
    low = lane < SWA_HEAD_DIM
    for blk in range(SWA_NQ + 1):
        rows = slice(blk * SWA_BLOCK, (blk + 1) * SWA_BLOCK)
        if blk == 0:
            vv = kvp_ref[:, 128:256].astype(F32)
        else:
            vv = kvc_ref[(blk - 1) * SWA_BLOCK:blk * SWA_BLOCK, 128:256].astype(F32)
        vv_sw = pltpu.roll(vv, SWA_HEAD_DIM, 1)
        for g in range(SWA_KV_HEADS):
            v_lo, v_hi = (vv, vv_sw) if g == 0 else (vv_sw, vv)
            vz_ref[2 * g + 0, rows, 0:128] = jnp.where(low, v_lo, 0.0).astype(BF16)
            vz_ref[2 * g + 1, rows, 0:128] = jnp.where(low, 0.0, v_hi).astype(BF16)

    for i in range(SWA_NQ):
        qrows = slice(i * SWA_BLOCK, (i + 1) * SWA_BLOCK)
        krows = slice(i * SWA_BLOCK, (i + 2) * SWA_BLOCK)
        bias_off = jnp.where(t == 0, SWA_HEADS, 0) if i == 0 else 0
        for pair in range(SWA_HEADS // 2):
            g = (2 * pair) // SWA_GROUP
            qp = q_ref[qrows, pair * 128:(pair + 1) * 128]
            o_pair = None
            for half in range(2):
                h = 2 * pair + half
                slot = 2 * g + half
                s2 = _dot(qp, kz_ref[slot, :, krows])
                s = jnp.where(tri, s2[:, SWA_BLOCK:], s2[:, :SWA_BLOCK]) + bias_ref[bias_off + h]
                sink = sink_ref[h]
                m = jnp.maximum(jnp.max(s, axis=-1, keepdims=True), sink)
                p = jnp.exp(s - m)
                p2 = jnp.concatenate([jnp.where(tri, 0.0, p), jnp.where(tri, p, 0.0)], axis=1).astype(BF16)
                od = _dot(p2, vz_ref[slot, krows, :])
                den = od[:, 128:] + jnp.exp(sink - m)
                o_h = od[:, :128] * (1.0 / den)
                o_pair = o_h if o_pair is None else o_pair + o_h
            o_ref[qrows, pair * 128:(pair + 1) * 128] = o_pair.astype(BF16)


def _swa(z, kt, rel_bias, sinks, mem2d, w_mem_kv, batch, seq):
    nt = seq // SWA_TQ
    bucket = jnp.asarray(_swa_bucket_table())
    kv_col = Z_KV // 256
    smem = pl.BlockSpec(memory_space=pltpu.SMEM)
    mkv_shape = (mem2d.shape[0], w_mem_kv.shape[1])
    return pl.pallas_call(
        _swa_kernel,
        out_shape=[jax.ShapeDtypeStruct((batch * seq, D_MODEL), BF16), jax.ShapeDtypeStruct(mkv_shape, BF16)],
        grid=(batch, nt),
        in_specs=[smem, smem, _resident(bucket.shape),
                  pl.BlockSpec((SWA_TQ, D_MODEL), lambda b, t: (b * nt + t, Z_SQ // D_MODEL)),
                  pl.BlockSpec((SWA_TQ, 256), lambda b, t: (b * nt + t, kv_col)),
                  pl.BlockSpec((SWA_BLOCK, 256),
                               lambda b, t: ((b * nt + t) * SWA_NQ - jnp.minimum(t, 1), kv_col)),
                  pl.BlockSpec((kt.shape[0], SWA_TQ), lambda b, t: (0, b * nt + t)),
                  pl.BlockSpec((kt.shape[0], SWA_BLOCK),
                               lambda b, t: (0, (b * nt + t) * SWA_NQ - jnp.minimum(t, 1))),
                  _resident(mem2d.shape), _resident(w_mem_kv.shape)],
        out_specs=[pl.BlockSpec((SWA_TQ, D_MODEL), lambda b, t: (b * nt + t, 0)),
                   pl.BlockSpec(mkv_shape, lambda b, t: (0, 0))],
        scratch_shapes=[pltpu.VMEM((2 * SWA_HEADS, SWA_BLOCK, SWA_BLOCK), F32),
                        pltpu.VMEM((2 * SWA_KV_HEADS, 2 * SWA_HEAD_DIM, SWA_TQ + SWA_BLOCK), BF16),
                        pltpu.VMEM((2 * SWA_KV_HEADS, SWA_TQ + SWA_BLOCK, 256), BF16)],
        compiler_params=pltpu.CompilerParams(
            dimension_semantics=("arbitrary", "arbitrary"), vmem_limit_bytes=VMEM_LIMIT_BYTES),
        name="swa",
    )(rel_bias, sinks, bucket, z, z, z, kt, kt, mem2d, w_mem_kv)


TAIL_TM = 512
TAIL_SUB = 256
TAIL_TF = 1024


def _layer_norm(y, g, b):
    mu = jnp.mean(y, axis=-1, keepdims=True)
    yc = y - mu
    var = jnp.mean(yc * yc, axis=-1, keepdims=True)
    return yc * lax.rsqrt(var + LN_EPS) * g + b


def _tail_kernel(alpha, x_ref, oa_ref, ob_ref, mq_ref, gl_ref, kv_ref, wa_ref, wb_ref, wc_ref, wo_ref,
                 g1_ref, b1_ref, wu_ref, wd_ref, g2_ref, b2_ref, o_ref):
    subs = [slice(r, r + TAIL_SUB) for r in range(0, TAIL_TM, TAIL_SUB)]
    mem_w = MEM_HEADS * MEM_HEAD_DIM
    hcols = [slice(h * MEM_HEAD_DIM, (h + 1) * MEM_HEAD_DIM) for h in range(MEM_HEADS)]

    def gate(rows, i):
        return jax.nn.sigmoid(gl_ref[rows, i * D_MODEL:(i + 1) * D_MODEL].astype(F32))

    scores = [[lax.dot_general(mq_ref[rows, c], kv_ref[:, c], _NT, preferred_element_type=F32) for c in hcols]
              for rows in subs]
    merged = [gate(rows, 0) * _dot(oa_ref[rows, :], wa_ref[...]) for rows in subs]
    probs = []
    for s_heads in scores:
        ps = []
        for s in s_heads:
            p = jnp.exp(s - jnp.max(s, axis=-1, keepdims=True))
            ps.append((p * (1.0 / jnp.sum(p, axis=-1, keepdims=True))).astype(BF16))
        probs.append(ps)
    o_c = [jnp.concatenate([_dot(p, kv_ref[:, mem_w + h * MEM_HEAD_DIM:mem_w + (h + 1) * MEM_HEAD_DIM]).astype(BF16)
                            for h, p in enumerate(ps)], axis=1) for ps in probs]
    merged = [m + gate(rows, 1) * _dot(ob_ref[rows, :], wb_ref[...]) for m, rows in zip(merged, subs)]
    merged = [m + gate(rows, 2) * _dot(oc, wc_ref[...]) for m, rows, oc in zip(merged, subs, o_c)]

    h1 = []
    for m, rows in zip(merged, subs):
        mix = _dot(m.astype(BF16), wo_ref[...])
        h1.append(_layer_norm(alpha * x_ref[rows, :] + mix, g1_ref[...], b1_ref[...]))

    hb = [h.astype(BF16) for h in h1]
    ff = [None] * len(subs)
    for f0 in range(0, D_FF, TAIL_TF):
        us = [jnp.maximum(_dot(hb[i], wu_ref[:, f0:f0 + TAIL_TF]), 0.0) for i in range(len(subs))]
        for i, u in enumerate(us):
            d = _dot((u * u).astype(BF16), wd_ref[f0:f0 + TAIL_TF, :])
            ff[i] = d if ff[i] is None else ff[i] + d
    for i, rows in enumerate(subs):
        o_ref[rows, :] = _layer_norm(alpha * h1[i] + ff[i], g2_ref[...], b2_ref[...])


def _tail(alpha, x2d, o_a, o_b, z, kv, wa, wb, wc, wo, ln1_g, ln1_b, wu, wd, ln2_g, ln2_b, batch, seq):
    t = x2d.shape[0]
    per_batch = seq // TAIL_TM
    row = lambda i: (i, 0)
    weights = (wa, wb, wc, wo, ln1_g, ln1_b, wu, wd, ln2_g, ln2_b)
    return pl.pallas_call(
        functools.partial(_tail_kernel, alpha),
        out_shape=jax.ShapeDtypeStruct((t, D_MODEL), F32),
        grid=(t // TAIL_TM,),
        in_specs=[pl.BlockSpec((TAIL_TM, D_MODEL), row),
                  pl.BlockSpec((TAIL_TM, D_MODEL), row),
                  pl.BlockSpec((TAIL_TM, D_MODEL), row),
                  pl.BlockSpec((TAIL_TM, D_MODEL), lambda i: (i, Z_MQ // D_MODEL)),
                  pl.BlockSpec((TAIL_TM, 3 * D_MODEL), lambda i: (i, Z_GL // (3 * D_MODEL))),
                  pl.BlockSpec((kv.shape[0] // batch, kv.shape[1]), lambda i: (i // per_batch, 0))]
                 + [_resident(w.shape) for w in weights],
        out_specs=pl.BlockSpec((TAIL_TM, D_MODEL), row),
        compiler_params=pltpu.CompilerParams(
            dimension_semantics=("arbitrary",), vmem_limit_bytes=VMEM_LIMIT_BYTES),
        name="tail",
    )(x2d, o_a, o_b, z, z, kv, *weights)


def kernel(x, mem, w_in, lb_logits, hg_norm_gain, swa_sinks, rel_bias, w_mem_kv, w_branch_hg, w_branch_swa,
           w_branch_mem, w_out, ln1_g, ln1_b, w_up, w_down, ln2_g, ln2_b):
    batch, seq, _ = x.shape
    depth = w_in.shape[0]
    assert depth == 1 and lb_logits.shape[0] == depth + 1
    alpha = (2.0 * depth) ** 0.25
    out_dtype = x.dtype

    x2d = x.reshape(batch * seq, D_MODEL).astype(F32)
    later = [w[0].astype(F32) for w in (w_branch_hg, w_branch_swa, w_branch_mem, w_out, w_up, w_down)]
    z, z_lo, z_k, kt, (wa, wb, wc, wo, wu, wd) = _proj_in(
        x2d, w_in[0].astype(F32), lb_logits.astype(F32), hg_norm_gain.astype(F32), later)
    o_a = _hgrn(z, z_lo, z_k, batch, seq)
    o_b, kv = _swa(z, kt, rel_bias.astype(F32), swa_sinks[0].astype(F32),
                   mem.reshape(batch * mem.shape[1], D_MODEL).astype(F32), w_mem_kv[0].astype(F32), batch, seq)
    h2 = _tail(alpha, x2d, o_a, o_b, z, kv, wa, wb, wc, wo, ln1_g.astype(F32), ln1_b.astype(F32),
               wu, wd, ln2_g.astype(F32), ln2_b.astype(F32), batch, seq)
    return h2.reshape(batch, seq, D_MODEL).astype(out_dtype)
```
